```python
import jax, jax.numpy as jnp
from jax import lax
import numpy as np

D_MODEL = 2048
BATCH = 2
SEQ = 4096
DEPTH = 2

HEAD_DIM = 64
A_HEADS = 12
A_PATTERNS = ((128, 1), (512, 4), (2048, 16))
B_HEADS = 12
B_KV_HEADS = 3
B_BRANCHES = 3
CMP_BLOCK = 32
CMP_STRIDE = 16
CMP_HIDDEN = 128
SLC_BLOCK = 64
SLC_TOPK = 16
WIN = 512
C_HEADS = 8
N_MIX_HEADS = A_HEADS + B_HEADS + C_HEADS
MIX_WIDTH = N_MIX_HEADS * HEAD_DIM
A_QKV_W = 3 * A_HEADS * HEAD_DIM
B_Q_W = B_HEADS * HEAD_DIM
B_KV_W = 2 * B_BRANCHES * B_KV_HEADS * HEAD_DIM
B_GATE_W = B_BRANCHES * B_HEADS
C_QKV_W = 3 * C_HEADS * HEAD_DIM
IN_WIDTH = A_QKV_W + B_Q_W + B_KV_W + B_GATE_W + C_QKV_W
IN_SPLITS = (A_QKV_W, A_QKV_W + B_Q_W, A_QKV_W + B_Q_W + B_KV_W, A_QKV_W + B_Q_W + B_KV_W + B_GATE_W)
N_EXPERTS = 32
TOP_K = 4
D_FF = 2048
SWIGLU_LIMIT = 7.0
SWIGLU_ALPHA = 1.702
ROPE_THETA = 10000.0
EPS = 1e-6
Q_BLOCK = 128
MOE_BLOCK = 128
NEG_BIG = -1e30
TINY = 1e-30
FORCE = 1e4

kernel_name = 'hybrid_dilated_nsa_stickbreak_moe_block'

F32 = jnp.float32


def rms_norm(x, g):
    xf = x.astype(F32)
    y = xf * lax.rsqrt(jnp.mean(xf * xf, axis=-1, keepdims=True) + EPS)
    return (y * g.astype(F32)).astype(x.dtype)


def rope_tables(seq):
    inv = 1.0 / (ROPE_THETA ** (jnp.arange(0, HEAD_DIM, 2, dtype=F32) / HEAD_DIM))
    ang = jnp.arange(seq, dtype=F32)[:, None] * inv[None, :]
    return jnp.cos(ang), jnp.sin(ang)


def apply_rope(x, cos, sin):
    x1, x2 = jnp.split(x.astype(F32), 2, axis=-1)
    return jnp.concatenate([x1 * cos - x2 * sin, x2 * cos + x1 * sin], axis=-1).astype(x.dtype)


def to_heads(t, n):
    B, T, _ = t.shape
    return t.reshape(B, T, n, HEAD_DIM).transpose(0, 2, 1, 3)


def attend(s, valid, v, spec):
    s = jnp.where(valid, s, NEG_BIG)
    m = jnp.max(s, axis=-1, keepdims=True)
    p = jnp.exp(s - m) * valid
    p = p / jnp.maximum(jnp.sum(p, axis=-1, keepdims=True), TINY)
    return jnp.einsum(spec, p, v), p


def dilated_attention(q, k, v):
    B, H, T, dh = q.shape
    scale = dh ** -0.5
    kf, vf = k.astype(F32), v.astype(F32)

    def block(i):
        q0 = i * Q_BLOCK
        t = q0 + jnp.arange(Q_BLOCK)
        qb = lax.dynamic_slice_in_dim(q, q0, Q_BLOCK, axis=2).astype(F32)
        outs, maxes, dens = [], [], []
        for window, dil in A_PATTERNS:
            offs = jnp.arange(window // dil + 1) * dil
            idx = t[:, None] - offs[None, :]
            valid = idx >= 0
            idx = jnp.maximum(idx, 0)
            kg, vg = kf[:, :, idx], vf[:, :, idx]
            s = jnp.einsum('bhqd,bhqnd->bhqn', qb, kg) * scale
            s = jnp.where(valid, s, NEG_BIG)
            m = jnp.max(s, axis=-1, keepdims=True)
            p = jnp.exp(s - m)
            l = jnp.sum(p, axis=-1, keepdims=True)
            outs.append(jnp.einsum('bhqn,bhqnd->bhqd', p, vg) / l)
            maxes.append(m)
            dens.append(l)
        m_all = jnp.max(jnp.stack(maxes), axis=0)
        w = jnp.stack([l * jnp.exp(m - m_all) for m, l in zip(maxes, dens)])
        o = jnp.sum(w * jnp.stack(outs), axis=0) / jnp.sum(w, axis=0)
        return o.astype(q.dtype)

    out = lax.map(block, jnp.arange(T // Q_BLOCK))
    return out.transpose(1, 2, 0, 3, 4).reshape(B, H, T, dh)


def compress(xk, pe, w1, w2):
    T = xk.shape[2]
    n_cmp = (T - CMP_BLOCK) // CMP_STRIDE + 1
    idx = np.arange(n_cmp)[:, None] * CMP_STRIDE + np.arange(CMP_BLOCK)[None, :]
    blocks = xk[:, :, idx] + pe
    flat = blocks.reshape(blocks.shape[0], blocks.shape[1], n_cmp, CMP_BLOCK * HEAD_DIM)
    return jax.nn.gelu(flat @ w1) @ w2


def cmp_to_slc_matrix(T):
    n_cmp = (T - CMP_BLOCK) // CMP_STRIDE + 1
    n_slc = T // SLC_BLOCK
    cs = np.arange(n_cmp) * CMP_STRIDE
    ss = np.arange(n_slc) * SLC_BLOCK
    ov = (cs[:, None] < ss[None, :] + SLC_BLOCK) & (cs[:, None] + CMP_BLOCK > ss[None, :])
    return jnp.asarray(ov.astype(np.float32))


def nsa_attention(q, kc, vc, ks, vs, kw, vw, gates):
    B, Hq, T, dh = q.shape
    G = ks.shape[1]
    R = Hq // G
    n_cmp = kc.shape[2]
    n_slc = T // SLC_BLOCK
    top = min(SLC_TOPK, n_slc)
    scale = dh ** -0.5
    cmp_end = jnp.arange(n_cmp) * CMP_STRIDE + (CMP_BLOCK - 1)
    overlap = cmp_to_slc_matrix(T)
    slc_id = jnp.arange(n_slc)
    kcf, vcf = kc.astype(F32), vc.astype(F32)
    ksb = ks.reshape(B, G, n_slc, SLC_BLOCK, dh)
    vsb = vs.reshape(B, G, n_slc, SLC_BLOCK, dh)
    kwp = jnp.pad(kw, ((0, 0), (0, 0), (WIN, 0), (0, 0)))
    vwp = jnp.pad(vw, ((0, 0), (0, 0), (WIN, 0), (0, 0)))
    b_ix = jnp.arange(B)[:, None, None, None]
    g_ix = jnp.arange(G)[None, :, None, None]

    def block(i):
        q0 = i * Q_BLOCK
        t = q0 + jnp.arange(Q_BLOCK)
        qg = lax.dynamic_slice_in_dim(q, q0, Q_BLOCK, axis=2).astype(F32).reshape(B, G, R, Q_BLOCK, dh)
        s = jnp.einsum('bgrqd,bgnd->bgrqn', qg, kcf) * scale
        o_cmp, p_cmp = attend(s, cmp_end[None, :] <= t[:, None], vcf, 'bgrqn,bgnd->bgrqd')
        imp = jnp.einsum('bgrqn,nm->bgqm', p_cmp, overlap)
        tb = t // SLC_BLOCK
        forced = (slc_id[None, :] == 0) | (slc_id[None, :] == tb[:, None]) | (slc_id[None, :] == tb[:, None] - 1)
        imp = jnp.where(forced, imp + FORCE, imp)
        imp = jnp.where(slc_id[None, :] > tb[:, None], -FORCE, imp)
        _, sel = lax.top_k(imp, top)
        kg = ksb[b_ix, g_ix, sel].reshape(B, G, Q_BLOCK, top * SLC_BLOCK, dh).astype(F32)
        vg = vsb[b_ix, g_ix, sel].reshape(B, G, Q_BLOCK, top * SLC_BLOCK, dh).astype(F32)
        pos = (sel[..., None] * SLC_BLOCK + jnp.arange(SLC_BLOCK)).reshape(B, G, Q_BLOCK, top * SLC_BLOCK)
        valid = (pos <= t[None, None, :, None])[:, :, None]
        s = jnp.einsum('bgrqd,bgqkd->bgrqk', qg, kg) * scale
        o_slc, _ = attend(s, valid, vg, 'bgrqk,bgqkd->bgrqd')
        kwb = lax.dynamic_slice_in_dim(kwp, q0, Q_BLOCK + WIN, axis=2).astype(F32)
        vwb = lax.dynamic_slice_in_dim(vwp, q0, Q_BLOCK + WIN, axis=2).astype(F32)
        kp = q0 - WIN + jnp.arange(Q_BLOCK + WIN)
        valid = (kp[None, :] <= t[:, None]) & (kp[None, :] > t[:, None] - WIN) & (kp[None, :] >= 0)
        s = jnp.einsum('bgrqd,bgkd->bgrqk', qg, kwb) * scale
        o_win, _ = attend(s, valid, vwb, 'bgrqk,bgkd->bgrqd')
        gb = lax.dynamic_slice_in_dim(gates, q0, Q_BLOCK, axis=2).reshape(B, G, R, Q_BLOCK, B_BRANCHES)
        o = gb[..., 0:1] * o_cmp + gb[..., 1:2] * o_slc + gb[..., 2:3] * o_win
        return o.reshape(B, Hq, Q_BLOCK, dh).astype(q.dtype)

    out = lax.map(block, jnp.arange(T // Q_BLOCK))
    return out.transpose(1, 2, 0, 3, 4).reshape(B, Hq, T, dh)


def stick_breaking_attention(q, k, v):
    B, H, T, dh = q.shape
    scale = dh ** -0.5
    kf, vf = k.astype(F32), v.astype(F32)
    key_pos = jnp.arange(T)

    def block(i):
        q0 = i * Q_BLOCK
        t = q0 + jnp.arange(Q_BLOCK)
        qb = lax.dynamic_slice_in_dim(q, q0, Q_BLOCK, axis=2).astype(F32)
        z = jnp.einsum('bhqd,bhkd->bhqk', qb, kf) * scale
        before = key_pos[None, :] < t[:, None]
        log_1m = jnp.where(before, jax.nn.log_sigmoid(-z), 0.0)
        suffix = lax.cumsum(log_1m, axis=3, reverse=True) - log_1m
        a = jnp.where(before, jnp.exp(jax.nn.log_sigmoid(z) + suffix), 0.0)
        return jnp.einsum('bhqk,bhkd->bhqd', a, vf).astype(q.dtype)

    out = lax.map(block, jnp.arange(T // Q_BLOCK))
    return out.transpose(1, 2, 0, 3, 4).reshape(B, H, T, dh)


def hybrid_mixer(h, w_in, cmp_pe, cmp_w1, cmp_w2, mix_norm, w_out, cos, sin):
    B, T, _ = h.shape
    proj = h @ w_in
    a_qkv, b_q, b_kv, b_gate, c_qkv = jnp.split(proj, IN_SPLITS, axis=-1)
    aq, ak, av = [to_heads(u, A_HEADS) for u in jnp.split(a_qkv, 3, axis=-1)]
    o_a = dilated_attention(apply_rope(aq, cos, sin), apply_rope(ak, cos, sin), av)
    bq = apply_rope(to_heads(b_q, B_HEADS), cos, sin)
    k_c, v_c, k_s, v_s, k_w, v_w = [to_heads(u, B_KV_HEADS) for u in jnp.split(b_kv, 6, axis=-1)]
    n_cmp = (T - CMP_BLOCK) // CMP_STRIDE + 1
    cmp_end = np.arange(n_cmp) * CMP_STRIDE + CMP_BLOCK - 1
    kc = apply_rope(compress(k_c, cmp_pe[0], cmp_w1[0], cmp_w2[0]), cos[cmp_end], sin[cmp_end])
    vc = compress(v_c, cmp_pe[1], cmp_w1[1], cmp_w2[1])
    gates = jax.nn.sigmoid(b_gate.reshape(B, T, B_HEADS, B_BRANCHES).astype(F32)).transpose(0, 2, 1, 3)
    o_b = nsa_attention(bq, kc, vc, apply_rope(k_s, cos, sin), v_s, apply_rope(k_w, cos, sin), v_w, gates)
    cq, ck, cv = [to_heads(u, C_HEADS) for u in jnp.split(c_qkv, 3, axis=-1)]
    o_c = stick_breaking_attention(cq, ck, cv)
    o = jnp.concatenate([o_a, o_b, o_c], axis=1).transpose(0, 2, 1, 3)
    of = o.astype(F32)
    of = of * lax.rsqrt(jnp.mean(of * of, axis=-1, keepdims=True) + EPS)
    o = (of.reshape(B, T, MIX_WIDTH) * mix_norm.astype(F32)).astype(h.dtype)
    return o @ w_out


def moe_ffn(h, w_router, b_router, w1, b1, w2, b2):
    B, T, D = h.shape
    N = B * T
    xt = h.reshape(N, D)
    logits = (xt @ w_router + b_router).astype(F32)
    top_val, top_idx = lax.top_k(logits, TOP_K)
    gate = jax.nn.softmax(top_val, axis=-1)
    NK = N * TOP_K
    e_flat = top_idx.reshape(NK)
    tok_flat = jnp.repeat(jnp.arange(N, dtype=jnp.int32), TOP_K)
    w_flat = gate.reshape(NK)
    order = jnp.argsort(e_flat)
    e_sorted, tok_sorted, w_sorted = e_flat[order], tok_flat[order], w_flat[order]
    counts = jnp.bincount(e_flat, length=N_EXPERTS)
    padded = (counts + MOE_BLOCK - 1) // MOE_BLOCK * MOE_BLOCK
    pad_end = jnp.cumsum(padded)
    pad_start = pad_end - padded
    grp_start = jnp.cumsum(counts) - counts
    dest = pad_start[e_sorted] + jnp.arange(NK) - grp_start[e_sorted]
    P = NK + N_EXPERTS * MOE_BLOCK
    n_blk = P // MOE_BLOCK
    buf_tok = jnp.zeros((P,), jnp.int32).at[dest].set(tok_sorted)
    buf_w = jnp.zeros((P,), F32).at[dest].set(w_sorted)
    blk_start = jnp.arange(n_blk) * MOE_BLOCK
    blk_expert = jnp.minimum(jnp.sum(blk_start[:, None] >= pad_end[None, :], axis=-1), N_EXPERTS - 1)

    def block(args):
        tok, e = args
        hc = xt[tok] @ w1[e] + b1[e]
        glu, lin = jnp.split(hc, 2, axis=-1)
        glu = jnp.minimum(glu, SWIGLU_LIMIT)
        lin = jnp.clip(lin, -SWIGLU_LIMIT, SWIGLU_LIMIT)
        act = glu * jax.nn.sigmoid(SWIGLU_ALPHA * glu) * (lin + 1)
        return act @ w2[e] + b2[e]

    y = lax.map(block, (buf_tok.reshape(n_blk, MOE_BLOCK), blk_expert)).reshape(P, D)
    y = y * buf_w[:, None].astype(y.dtype)
    out = jnp.zeros((N, D), y.dtype).at[buf_tok].add(y)
    return out.reshape(B, T, D)


def setup_inputs(seed: int = 0) -> dict:
    key = jax.random.key(seed)
    ks = jax.random.split(key, 19)
    L, D, E, F = DEPTH, D_MODEL, N_EXPERTS, D_FF

    def nrm(k, shape, scale):
        return scale * jax.random.normal(k, shape, F32)

    return {
        'x': nrm(ks[0], (BATCH, SEQ, D), 1.0),
        'c': nrm(ks[1], (BATCH, D), 1.0),
        'w_mod': nrm(ks[2], (L, D, 6 * D), 0.5 * D ** -0.5),
        'b_mod': nrm(ks[3], (L, 6 * D), 0.02),
        'norm_attn': 1.0 + nrm(ks[4], (L, D), 0.05),
        'norm_ffn': 1.0 + nrm(ks[5], (L, D), 0.05),
        'w_in': nrm(ks[6], (L, D, IN_WIDTH), D ** -0.5),
        'cmp_pe': nrm(ks[7], (L, 2, CMP_BLOCK, HEAD_DIM), 0.1),
        'cmp_w1': nrm(ks[8], (L, 2, CMP_BLOCK * HEAD_DIM, CMP_HIDDEN), (CMP_BLOCK * HEAD_DIM) ** -0.5),
        'cmp_w2': nrm(ks[9], (L, 2, CMP_HIDDEN, HEAD_DIM), CMP_HIDDEN ** -0.5),
        'mix_norm': 1.0 + nrm(ks[10], (L, MIX_WIDTH), 0.05),
        'w_out': nrm(ks[11], (L, MIX_WIDTH, D), MIX_WIDTH ** -0.5),
        'w_router': nrm(ks[12], (L, D, E), D ** -0.5),
        'b_router': nrm(ks[13], (L, E), 0.01),
        'w_exp1': nrm(ks[14], (L, E, D, 2 * F), D ** -0.5),
        'b_exp1': nrm(ks[15], (L, E, 2 * F), 0.02),
        'w_exp2': nrm(ks[16], (L, E, F, D), F ** -0.5),
        'b_exp2': nrm(ks[17], (L, E, D), 0.02),
        'norm_final': 1.0 + nrm(ks[18], (D,), 0.05),
    }


def reference(x, c, w_mod, b_mod, norm_attn, norm_ffn, w_in, cmp_pe, cmp_w1, cmp_w2,
              mix_norm, w_out, w_router, b_router, w_exp1, b_exp1, w_exp2, b_exp2, norm_final):
    T = x.shape[1]
    cos, sin = rope_tables(T)
    c_act = jax.nn.silu(c)
    for i in range(DEPTH):
        mod = c_act @ w_mod[i] + b_mod[i]
        sh1, sc1, g1, sh2, sc2, g2 = [m[:, None, :] for m in jnp.split(mod, 6, axis=-1)]
        h = rms_norm(x, norm_attn[i]) * (1 + sc1) + sh1
        x = x + g1 * hybrid_mixer(h, w_in[i], cmp_pe[i], cmp_w1[i], cmp_w2[i], mix_norm[i], w_out[i], cos, sin)
        h = rms_norm(x, norm_ffn[i]) * (1 + sc2) + sh2
        x = x + g2 * moe_ffn(h, w_router[i], b_router[i], w_exp1[i], b_exp1[i], w_exp2[i], b_exp2[i])
    return rms_norm(x, norm_final)
```

```python
import functools

import numpy as np
import jax
import jax.numpy as jnp
from jax import lax
from jax.experimental import pallas as pl
from jax.experimental.pallas import tpu as pltpu

F32 = jnp.float32
BF16 = jnp.bfloat16
HIGHEST = lax.Precision.HIGHEST

D_MODEL = 2048
DEPTH = 2
HEAD_DIM = 64
A_HEADS = 12
A_PATTERNS = ((128, 1), (512, 4), (2048, 16))
B_HEADS = 12
B_KV_HEADS = 3
B_BRANCHES = 3
CMP_BLOCK = 32
CMP_STRIDE = 16
CMP_HIDDEN = 128
SLC_BLOCK = 64
SLC_TOPK = 16
WIN = 512
C_HEADS = 8
A_QKV_W = 3 * A_HEADS * HEAD_DIM
B_Q_W = B_HEADS * HEAD_DIM
B_KV_W = 2 * B_BRANCHES * B_KV_HEADS * HEAD_DIM
B_GATE_W = B_BRANCHES * B_HEADS
C_QKV_W = 3 * C_HEADS * HEAD_DIM
N_EXPERTS = 32
TOP_K = 4
D_FF = 2048
SWIGLU_LIMIT = 7.0
SWIGLU_ALPHA = 1.702
ROPE_THETA = 10000.0
EPS = 1e-6
NEG_BIG = -1e30
TINY = 1e-30
FORCE = 1e4
LOWEST = -3.0e38

LANES = 128
VMEM_LIMIT = 56 * 1024 * 1024

TQ = 128
TK = 128
TM_PROJ = 512
TN_IN = 768
TN_OUT = 512
TM_ROUTE = 256
MOE_ROWS = 256
TF = 512
TN_MOE = 512
TC = 128

BLK_AQ, BLK_AK, BLK_AV = 0, 6, 12
BLK_BQ = 18
BLK_KCVC = 24
BLK_KS, BLK_VS, BLK_KW, BLK_VW = 27, 30, 33, 36
BLK_CQ, BLK_CK, BLK_CV = 39, 43, 47
BLK_GATE = 51
N_BLKS = 54
N_COLS = N_BLKS * LANES


def _cparams(n_axes):
    return pltpu.CompilerParams(dimension_semantics=("arbitrary",) * n_axes,
                                vmem_limit_bytes=VMEM_LIMIT)


def _layout():
    a0 = 0
    bq0 = A_QKV_W
    bkv0 = bq0 + B_Q_W
    bg0 = bkv0 + B_KV_W
    c0 = bg0 + B_GATE_W
    scale = HEAD_DIM ** -0.5
    runs, rope, colscale = [], [], []

    def add(start, stop, r, s):
        runs.append((start, stop))
        n = stop - start
        rope.extend([r] * n)
        colscale.extend([s] * n)

    def pad(n):
        runs.append((None, n))
        rope.extend([0.0] * n)
        colscale.extend([1.0] * n)

    hw = A_HEADS * HEAD_DIM
    add(a0, a0 + hw, 1.0, scale)
    add(a0 + hw, a0 + 2 * hw, 1.0, 1.0)
    add(a0 + 2 * hw, a0 + 3 * hw, 0.0, 1.0)
    add(bq0, bq0 + B_Q_W, 1.0, scale)
    gw = B_KV_HEADS * HEAD_DIM
    add(bkv0, bkv0 + 2 * gw, 0.0, 1.0)
    for arr in (2, 3, 4, 5):
        r = 1.0 if arr in (2, 4) else 0.0
        for g in range(B_KV_HEADS):
            s0 = bkv0 + arr * gw + g * HEAD_DIM
            add(s0, s0 + HEAD_DIM, r, 1.0)
            add(s0, s0 + HEAD_DIM, r, 1.0)
    cw = C_HEADS * HEAD_DIM
    add(c0, c0 + cw, 0.0, scale)
    add(c0 + cw, c0 + 3 * cw, 0.0, 1.0)
    per_g = B_GATE_W // B_KV_HEADS
    for g in range(B_KV_HEADS):
        add(bg0 + g * per_g, bg0 + (g + 1) * per_g, 0.0, 1.0)
        pad(LANES - per_g)
    assert len(rope) == N_COLS
    return runs, np.asarray(rope, np.float32)[None, :], np.asarray(colscale, np.float32)[None, :]


_RUNS, _ROPE_FLAG, _COL_SCALE = _layout()


def _permute_cols(w):
    parts = []
    for start, stop in _RUNS:
        if start is None:
            parts.append(jnp.zeros(w.shape[:-1] + (stop,), w.dtype))
        else:
            parts.append(w[..., start:stop])
    return jnp.concatenate(parts, axis=-1)


def _dilated_mult_table():
    n = A_PATTERNS[-1][0] // TK + 1
    d = (np.arange(n)[:, None, None] * TK + np.arange(TQ)[None, :, None] - np.arange(TK)[None, None, :])
    m = np.zeros(d.shape, np.float32)
    for window, dil in A_PATTERNS:
        m += ((d >= 0) & (d <= window) & (d % dil == 0)).astype(np.float32)
    return m


def _overlap_table(T):
    n_cmp = (T - CMP_BLOCK) // CMP_STRIDE + 1
    n_slc = T // SLC_BLOCK
    cs = np.arange(n_cmp) * CMP_STRIDE
    ss = np.arange(n_slc) * SLC_BLOCK
    ov = (cs[:, None] < ss[None, :] + SLC_BLOCK) & (cs[:, None] + CMP_BLOCK > ss[None, :])
    out = np.zeros((n_cmp + 1, LANES), np.float32)
    out[:n_cmp, :n_slc] = ov
    return out


def _mod_kernel(c_ref, w_ref, b_ref, o_ref):
    c = c_ref[...]
    ca = (c * jax.nn.sigmoid(c)).astype(BF16)
    y = jnp.dot(ca, w_ref[0].astype(BF16), preferred_element_type=F32)
    o_ref[0] = y + b_ref[0]


def _modulation(c, w_mod, b_mod):
    B, D = c.shape
    L, _, W = w_mod.shape
    tn = 1024
    rows = 8
    cp = jnp.zeros((rows, D), F32).at[:B].set(c)
    out = pl.pallas_call(
        _mod_kernel,
        grid=(L, W // tn),
        in_specs=[pl.BlockSpec((rows, D), lambda l, j: (0, 0)),
                  pl.BlockSpec((1, D, tn), lambda l, j: (l, 0, j)),
                  pl.BlockSpec((1, 1, tn), lambda l, j: (l, 0, j))],
        out_specs=pl.BlockSpec((1, rows, tn), lambda l, j: (l, 0, j)),
        out_shape=jax.ShapeDtypeStruct((L, rows, W), F32),
        compiler_params=_cparams(2),
        name="adaln_mod",
    )(cp, w_mod, b_mod.reshape(L, 1, W))
    return out[:, :B]


def _norm_mod(x, gain, sc, sh):
    ms = jnp.mean(x * x, axis=-1, keepdims=True)
    y = x * lax.rsqrt(ms + EPS) * gain
    return y * (1.0 + sc) + sh


def _rope128(y, cos, sin_signed):
    lane = lax.broadcasted_iota(jnp.int32, y.shape, 1)
    first_half = (lane % HEAD_DIM) < (HEAD_DIM // 2)
    rot = jnp.where(first_half, pltpu.roll(y, LANES - HEAD_DIM // 2, 1), pltpu.roll(y, HEAD_DIM // 2, 1))
    return y * cos + rot * sin_signed


def _inproj_kernel(x_ref, sc_ref, sh_ref, g_ref, w_ref, cs_ref, rf_ref, cos_ref, sin_ref, o_ref, h_scr):
    @pl.when(pl.program_id(1) == 0)
    def _():
        h_scr[...] = _norm_mod(x_ref[...], g_ref[...], sc_ref[0], sh_ref[0]).astype(BF16)

    y = jnp.dot(h_scr[...], w_ref[...], preferred_element_type=F32)
    cos = cos_ref[...]
    sin = sin_ref[...]
    for c in range(TN_IN // LANES):
        sl = slice(c * LANES, (c + 1) * LANES)
        yc = y[:, sl]
        yr = _rope128(yc, cos, sin)
        o_ref[:, sl] = jnp.where(rf_ref[:, sl] > 0.5, yr, yc) * cs_ref[:, sl]


def _in_projection(x2d, sc, sh, gain, w_re, cos128, sin128, T):
    N, D = x2d.shape
    nt = T // TM_PROJ
    return pl.pallas_call(
        _inproj_kernel,
        grid=(N // TM_PROJ, N_COLS // TN_IN),
        in_specs=[pl.BlockSpec((TM_PROJ, D), lambda i, j: (i, 0)),
                  pl.BlockSpec((1, 1, D), lambda i, j: (i // nt, 0, 0)),
                  pl.BlockSpec((1, 1, D), lambda i, j: (i // nt, 0, 0)),
                  pl.BlockSpec((1, D), lambda i, j: (0, 0)),
                  pl.BlockSpec((D, TN_IN), lambda i, j: (0, j)),
                  pl.BlockSpec((1, TN_IN), lambda i, j: (0, j)),
                  pl.BlockSpec((1, TN_IN), lambda i, j: (0, j)),
                  pl.BlockSpec((TM_PROJ, LANES), lambda i, j: (i % nt, 0)),
                  pl.BlockSpec((TM_PROJ, LANES), lambda i, j: (i % nt, 0))],
        out_specs=pl.BlockSpec((TM_PROJ, TN_IN), lambda i, j: (i, j)),
        out_shape=jax.ShapeDtypeStruct((N, N_COLS), F32),
        scratch_shapes=[pltpu.VMEM((TM_PROJ, D), BF16)],
        compiler_params=_cparams(2),
        name="in_proj",
    )(x2d, sc, sh, gain, w_re, jnp.asarray(_COL_SCALE), jnp.asarray(_ROPE_FLAG), cos128, sin128)


def _split_heads_q(q):
    lane = lax.broadcasted_iota(jnp.int32, q.shape, 1)
    lo = lane < HEAD_DIM
    return jnp.concatenate([jnp.where(lo, q, 0.0), jnp.where(lo, 0.0, q)], axis=0)


def _scores(q2, k):
    return lax.dot_general(q2, k, (((1,), (1,)), ((), ())), preferred_element_type=F32)


def _flash_update(s, valid, weight, m, l, acc, v):
    s = jnp.where(valid, s, NEG_BIG)
    m_new = jnp.maximum(m, jnp.max(s, axis=-1, keepdims=True))
    alpha = jnp.exp(m - m_new)
    p = jnp.exp(s - m_new) * weight
    l = alpha * l + jnp.sum(p, axis=-1, keepdims=True)
    acc = alpha * acc + jnp.dot(p.astype(BF16), v, preferred_element_type=F32)
    return m_new, l, acc


def _merge_heads_norm(o2, rows):
    lane = lax.broadcasted_iota(jnp.int32, (rows, LANES), 1)
    lo = lane < HEAD_DIM
    out = jnp.where(lo, o2[:rows], o2[rows:])
    sq = out * out
    ms0 = jnp.sum(jnp.where(lo, sq, 0.0), axis=-1, keepdims=True) * (1.0 / HEAD_DIM)
    ms1 = jnp.sum(jnp.where(lo, 0.0, sq), axis=-1, keepdims=True) * (1.0 / HEAD_DIM)
    return out * jnp.where(lo, lax.rsqrt(ms0 + EPS), lax.rsqrt(ms1 + EPS))


def _attn_a_kernel(q_ref, k_ref, v_ref, tbl_ref, o_ref):
    qi = pl.program_id(2)
    q2 = _split_heads_q(q_ref[0]).astype(BF16)
    n_d = tbl_ref.shape[0]

    def body(kt, carry):
        m, l, acc = carry
        start = pl.multiple_of(kt * TK, TK)
        k = k_ref[0, pl.ds(start, TK), :].astype(BF16)
        v = v_ref[0, pl.ds(start, TK), :].astype(BF16)
        s = _scores(q2, k)
        mult = tbl_ref[qi - kt]
        mult2 = jnp.concatenate([mult, mult], axis=0)
        return _flash_update(s, mult2 > 0.5, mult2, m, l, acc, v)

    init = (jnp.full((2 * TQ, 1), NEG_BIG, F32), jnp.zeros((2 * TQ, 1), F32), jnp.zeros((2 * TQ, LANES), F32))
    m, l, acc = lax.fori_loop(jnp.maximum(qi - (n_d - 1), 0), qi + 1, body, init)
    o_ref[0] = _merge_heads_norm(acc / l, TQ)


def _attention_a(proj, tbl):
    B, T, _ = proj.shape
    n_pairs = A_HEADS // 2
    return pl.pallas_call(
        _attn_a_kernel,
        grid=(B, n_pairs, T // TQ),
        in_specs=[pl.BlockSpec((1, TQ, LANES), lambda b, p, i: (b, i, BLK_AQ + p)),
                  pl.BlockSpec((1, T, LANES), lambda b, p, i: (b, 0, BLK_AK + p)),
                  pl.BlockSpec((1, T, LANES), lambda b, p, i: (b, 0, BLK_AV + p)),
                  pl.BlockSpec(tbl.shape, lambda b, p, i: (0, 0, 0))],
        out_specs=pl.BlockSpec((1, TQ, LANES), lambda b, p, i: (b, i, p)),
        out_shape=jax.ShapeDtypeStruct((B, T, n_pairs * LANES), F32),
        compiler_params=_cparams(3),
        name="attn_dilated",
    )(proj, proj, proj, tbl)


def _attn_c_kernel(q_ref, k_ref, v_ref, o_ref):
    qi = pl.program_id(2)
    q2 = _split_heads_q(q_ref[0]).astype(BF16)
    row = lax.broadcasted_iota(jnp.int32, (2 * TQ, TK), 0) % TQ
    col = lax.broadcasted_iota(jnp.int32, (2 * TQ, TK), 1)
    uj = lax.broadcasted_iota(jnp.int32, (TK, TK), 0)
    us = lax.broadcasted_iota(jnp.int32, (TK, TK), 1)
    upper = (uj >= us).astype(F32)

    def body(i, carry):
        run, acc = carry
        kt = qi - i
        start = pl.multiple_of(kt * TK, TK)
        k = k_ref[0, pl.ds(start, TK), :].astype(BF16)
        v = v_ref[0, pl.ds(start, TK), :].astype(BF16)
        z = _scores(q2, k)
        before = (kt * TK + col) < (qi * TQ + row)
        log_1m = -(jnp.maximum(z, 0.0) + jnp.log1p(jnp.exp(-jnp.abs(z))))
        log_1m = jnp.where(before, log_1m, 0.0)
        suffix = jnp.dot(log_1m, upper, precision=HIGHEST, preferred_element_type=F32)
        a = jnp.where(before, jnp.exp(z + suffix + run), 0.0)
        acc = acc + jnp.dot(a.astype(BF16), v, preferred_element_type=F32)
        return run + suffix[:, 0:1], acc

    init = (jnp.zeros((2 * TQ, 1), F32), jnp.zeros((2 * TQ, LANES), F32))
    _, acc = lax.fori_loop(0, qi + 1, body, init)
    o_ref[0] = _merge_heads_norm(acc, TQ)


def _attention_c(proj):
    B, T, _ = proj.shape
    n_pairs = C_HEADS // 2
    return pl.pallas_call(
        _attn_c_kernel,
        grid=(B, n_pairs, T // TQ),
        in_specs=[pl.BlockSpec((1, TQ, LANES), lambda b, p, i: (b, i, BLK_CQ + p)),
                  pl.BlockSpec((1, T, LANES), lambda b, p, i: (b, 0, BLK_CK + p)),
                  pl.BlockSpec((1, T, LANES), lambda b, p, i: (b, 0, BLK_CV + p))],
        out_specs=pl.BlockSpec((1, TQ, LANES), lambda b, p, i: (b, i, p)),
        out_shape=jax.ShapeDtypeStruct((B, T, n_pairs * LANES), F32),
        compiler_params=_cparams(3),
        name="attn_stickbreak",
    )(proj, proj, proj)


def _compress_kernel(x_ref, pe_ref, w1_ref, w2_ref, w2r_ref, cos_ref, sin_ref, o_ref):
    x = x_ref[0, 0, 0]
    half = x.shape[1]
    ha = jnp.dot(x + pe_ref[0, 0:1, :], w1_ref[0, :half, :], precision=HIGHEST, preferred_element_type=F32)
    hb = jnp.dot(x + pe_ref[0, 1:2, :], w1_ref[0, half:, :], precision=HIGHEST, preferred_element_type=F32)
    n = x.shape[0]
    h = ha + pltpu.roll(hb, n - 1, 0)
    g = jax.nn.gelu(h)
    y = jnp.dot(g, w2_ref[0], precision=HIGHEST, preferred_element_type=F32)
    yr = jnp.dot(g, w2r_ref[0], precision=HIGHEST, preferred_element_type=F32)
    out = y * cos_ref[0] + yr * sin_ref[0]
    rowi = lax.broadcasted_iota(jnp.int32, out.shape, 0)
    o_ref[0, 0, 0] = jnp.where(rowi < n - 1, out, 0.0)


def _compress(chunks, pe, w1, w2d, w2r, cosc, sinc):
    B, _, G, n, cw = chunks.shape
    return pl.pallas_call(
        _compress_kernel,
        grid=(B, 2, G),
        in_specs=[pl.BlockSpec((1, 1, 1, n, cw), lambda b, w, g: (b, w, g, 0, 0)),
                  pl.BlockSpec((1, 2, cw), lambda b, w, g: (w, 0, 0)),
                  pl.BlockSpec((1, 2 * cw, CMP_HIDDEN), lambda b, w, g: (w, 0, 0)),
                  pl.BlockSpec((1, CMP_HIDDEN, LANES), lambda b, w, g: (w, 0, 0)),
                  pl.BlockSpec((1, CMP_HIDDEN, LANES), lambda b, w, g: (w, 0, 0)),
                  pl.BlockSpec((1, n, LANES), lambda b, w, g: (w, 0, 0)),
                  pl.BlockSpec((1, n, LANES), lambda b, w, g: (w, 0, 0))],
        out_specs=pl.BlockSpec((1, 1, 1, n, LANES), lambda b, w, g: (b, w, g, 0, 0)),
        out_shape=jax.ShapeDtypeStruct((B, 2, G, n, LANES), F32),
        compiler_params=_cparams(3),
        name="nsa_compress",
    )(chunks, pe, w1, w2d, w2r, cosc, sinc)


def _nsa_kernel(q_ref, cmp_k_ref, cmp_v_ref, ks_ref, vs_ref, kw_ref, vw_ref, gate_ref, ov_ref, o_ref):
    qi = pl.program_id(2)
    R = B_HEADS // B_KV_HEADS
    rows = R * TQ
    q = q_ref[0]
    qf = jnp.concatenate([_split_heads_q(q[:, :LANES]), _split_heads_q(q[:, LANES:])], axis=0)
    q4 = qf.astype(BF16)
    t_row = qi * TQ + lax.broadcasted_iota(jnp.int32, (rows, 1), 0) % TQ
    t_q = qi * TQ + lax.broadcasted_iota(jnp.int32, (TQ, 1), 0)

    kc = cmp_k_ref[0, 0, 0]
    vc = cmp_v_ref[0, 0, 0]
    n_c = kc.shape[0]
    s = lax.dot_general(qf, kc, (((1,), (1,)), ((), ())), precision=HIGHEST, preferred_element_type=F32)
    cmp_end = lax.broadcasted_iota(jnp.int32, (rows, n_c), 1) * CMP_STRIDE + (CMP_BLOCK - 1)
    valid = (cmp_end <= t_row) & (cmp_end < (n_c - 1) * CMP_STRIDE + CMP_BLOCK - 1)
    validf = valid.astype(F32)
    s = jnp.where(valid, s, NEG_BIG)
    p = jnp.exp(s - jnp.max(s, axis=-1, keepdims=True)) * validf
    p = p / jnp.maximum(jnp.sum(p, axis=-1, keepdims=True), TINY)
    o_cmp = jnp.dot(p.astype(BF16), vc.astype(BF16), preferred_element_type=F32)

    p_sum = p[0:TQ] + p[TQ:2 * TQ] + p[2 * TQ:3 * TQ] + p[3 * TQ:4 * TQ]
    imp = jnp.dot(p_sum, ov_ref[...], precision=HIGHEST, preferred_element_type=F32)
    n_slc = ks_ref.shape[1] // SLC_BLOCK
    blk = lax.broadcasted_iota(jnp.int32, (TQ, LANES), 1)
    blk_f = blk.astype(F32)
    tb = t_q // SLC_BLOCK
    forced = (blk == 0) | (blk == tb) | (blk == tb - 1)
    imp = jnp.where(forced, imp + FORCE, imp)
    imp = jnp.where(blk > tb, -FORCE, imp)
    imp = jnp.where(blk < n_slc, imp, LOWEST)
    sel = jnp.zeros((TQ, LANES), F32)
    for _ in range(min(SLC_TOPK, n_slc)):
        mx = jnp.max(imp, axis=-1, keepdims=True)
        first = jnp.min(jnp.where(imp == mx, blk_f, float(LANES)), axis=-1, keepdims=True)
        pick = blk_f == first
        sel = jnp.where(pick, 1.0, sel)
        imp = jnp.where(pick, LOWEST, imp)
    sel = sel.astype(BF16)

    col = lax.broadcasted_iota(jnp.int32, (rows, TK), 1)
    e_row = lax.broadcasted_iota(jnp.int32, (LANES, TK), 0)
    e_col = lax.broadcasted_iota(jnp.int32, (LANES, TK), 1) // SLC_BLOCK
    blocks_per_tile = TK // SLC_BLOCK

    def flash_init():
        return (jnp.full((rows, 1), NEG_BIG, F32), jnp.zeros((rows, 1), F32), jnp.zeros((rows, LANES), F32))

    def slc_body(kt, carry):
        m, l, acc = carry
        start = pl.multiple_of(kt * TK, TK)
        k = ks_ref[0, pl.ds(start, TK), :].astype(BF16)
        v = vs_ref[0, pl.ds(start, TK), :].astype(BF16)
        sc = _scores(q4, k)
        expand = (e_row == kt * blocks_per_tile + e_col).astype(BF16)
        picked = jnp.dot(sel, expand, preferred_element_type=F32)
        picked4 = jnp.concatenate([picked] * R, axis=0)
        ok = (picked4 > 0.5) & ((kt * TK + col) <= t_row)
        return _flash_update(sc, ok, ok.astype(F32), m, l, acc, v)

    m, l, acc = lax.fori_loop(0, qi + 1, slc_body, flash_init())
    o_slc = acc / jnp.maximum(l, TINY)

    def win_body(kt, carry):
        m, l, acc = carry
        start = pl.multiple_of(kt * TK, TK)
        k = kw_ref[0, pl.ds(start, TK), :].astype(BF16)
        v = vw_ref[0, pl.ds(start, TK), :].astype(BF16)
        sc = _scores(q4, k)
        kp = kt * TK + col
        ok = (kp <= t_row) & (kp > t_row - WIN)
        return _flash_update(sc, ok, ok.astype(F32), m, l, acc, v)

    m, l, acc = lax.fori_loop(jnp.maximum(qi - WIN // TK, 0), qi + 1, win_body, flash_init())
    o_win = acc / jnp.maximum(l, TINY)

    gates = jax.nn.sigmoid(gate_ref[0])
    outs = []
    for r in range(R):
        rs = slice(r * TQ, (r + 1) * TQ)
        o = (gates[:, 3 * r:3 * r + 1] * o_cmp[rs] + gates[:, 3 * r + 1:3 * r + 2] * o_slc[rs]
             + gates[:, 3 * r + 2:3 * r + 3] * o_win[rs])
        ms = jnp.mean(o * o, axis=-1, keepdims=True)
        outs.append(o * lax.rsqrt(ms + EPS))
    lane = lax.broadcasted_iota(jnp.int32, (TQ, LANES), 1)
    lo = lane < HEAD_DIM
    o_ref[0] = jnp.concatenate([jnp.where(lo, outs[0], outs[1]), jnp.where(lo, outs[2], outs[3])], axis=1)


def _attention_b(proj, cmp_kv, ov):
    B, T, _ = proj.shape
    G = B_KV_HEADS
    n_c = cmp_kv.shape[3]
    qw = 2 * LANES
    return pl.pallas_call(
        _nsa_kernel,
        grid=(B, G, T // TQ),
        in_specs=[pl.BlockSpec((1, TQ, qw), lambda b, g, i: (b, i, BLK_BQ // 2 + g)),
                  pl.BlockSpec((1, 1, 1, n_c, LANES), lambda b, g, i: (b, 0, g, 0, 0)),
                  pl.BlockSpec((1, 1, 1, n_c, LANES), lambda b, g, i: (b, 1, g, 0, 0)),
                  pl.BlockSpec((1, T, LANES), lambda b, g, i: (b, 0, BLK_KS + g)),
                  pl.BlockSpec((1, T, LANES), lambda b, g, i: (b, 0, BLK_VS + g)),
                  pl.BlockSpec((1, T, LANES), lambda b, g, i: (b, 0, BLK_KW + g)),
                  pl.BlockSpec((1, T, LANES), lambda b, g, i: (b, 0, BLK_VW + g)),
                  pl.BlockSpec((1, TQ, LANES), lambda b, g, i: (b, i, BLK_GATE + g)),
                  pl.BlockSpec(ov.shape, lambda b, g, i: (0, 0))],
        out_specs=pl.BlockSpec((1, TQ, qw), lambda b, g, i: (b, i, g)),
        out_shape=jax.ShapeDtypeStruct((B, T, B_HEADS * HEAD_DIM), F32),
        compiler_params=_cparams(3),
        name="attn_nsa",
    )(proj, cmp_kv, cmp_kv, proj, proj, proj, proj, proj, ov)


def _outproj_kernel(oa_ref, ob_ref, oc_ref, mn_ref, w_ref, x_ref, g_ref, o_ref, h_scr):
    wa = oa_ref.shape[1]
    wb = ob_ref.shape[1]

    @pl.when(pl.program_id(1) == 0)
    def _():
        h_scr[:, :wa] = (oa_ref[...] * mn_ref[:, :wa]).astype(BF16)
        h_scr[:, wa:wa + wb] = (ob_ref[...] * mn_ref[:, wa:wa + wb]).astype(BF16)
        h_scr[:, wa + wb:] = (oc_ref[...] * mn_ref[:, wa + wb:]).astype(BF16)

    y = jnp.dot(h_scr[...], w_ref[...], preferred_element_type=F32)
    o_ref[...] = x_ref[...] + g_ref[0] * y


def _out_projection(oa, ob, oc, mix_norm, w_out_bf, x2d, g1, T):
    N, D = x2d.shape
    nt = T // TM_PROJ
    wa, wb, wc = oa.shape[1], ob.shape[1], oc.shape[1]
    return pl.pallas_call(
        _outproj_kernel,
        grid=(N // TM_PROJ, D // TN_OUT),
        in_specs=[pl.BlockSpec((TM_PROJ, wa), lambda i, j: (i, 0)),
                  pl.BlockSpec((TM_PROJ, wb), lambda i, j: (i, 0)),
                  pl.BlockSpec((TM_PROJ, wc), lambda i, j: (i, 0)),
                  pl.BlockSpec((1, D), lambda i, j: (0, 0)),
                  pl.BlockSpec((D, TN_OUT), lambda i, j: (0, j)),
                  pl.BlockSpec((TM_PROJ, TN_OUT), lambda i, j: (i, j)),
                  pl.BlockSpec((1, 1, TN_OUT), lambda i, j: (i // nt, 0, j))],
        out_specs=pl.BlockSpec((TM_PROJ, TN_OUT), lambda i, j: (i, j)),
        out_shape=jax.ShapeDtypeStruct((N, D), F32),
        scratch_shapes=[pltpu.VMEM((TM_PROJ, D), BF16)],
        compiler_params=_cparams(2),
        name="out_proj",
    )(oa, ob, oc, mix_norm, w_out_bf, x2d, g1)


def _router_kernel(x_ref, sc_ref, sh_ref, g_ref, wr_ref, br_ref, h_ref, idx_ref, gate_ref):
    h = _norm_mod(x_ref[...], g_ref[...], sc_ref[0], sh_ref[0])
    h_ref[...] = h
    logits = jnp.dot(h, wr_ref[...], precision=HIGHEST, preferred_element_type=F32) + br_ref[...]
    lane = lax.broadcasted_iota(jnp.int32, logits.shape, 1)
    lane_f = lane.astype(F32)
    cur = jnp.where(lane < N_EXPERTS, logits, LOWEST)
    idx_out = jnp.zeros(logits.shape, F32)
    e_out = jnp.zeros(logits.shape, F32)
    top0 = None
    denom = None
    for k in range(TOP_K):
        mx = jnp.max(cur, axis=-1, keepdims=True)
        first = jnp.min(jnp.where(cur == mx, lane_f, float(LANES)), axis=-1, keepdims=True)
        cur = jnp.where(lane_f == first, LOWEST, cur)
        if k == 0:
            top0 = mx
        e = jnp.exp(mx - top0)
        denom = e if k == 0 else denom + e
        idx_out = jnp.where(lane == k, first, idx_out)
        e_out = jnp.where(lane == k, e, e_out)
    idx_ref[...] = idx_out.astype(jnp.int32)
    gate_ref[...] = e_out / denom


def _router(x2d, sc, sh, gain, wr_pad, br_pad, T):
    N, D = x2d.shape
    nt = T // TM_ROUTE
    return pl.pallas_call(
        _router_kernel,
        grid=(N // TM_ROUTE,),
        in_specs=[pl.BlockSpec((TM_ROUTE, D), lambda i: (i, 0)),
                  pl.BlockSpec((1, 1, D), lambda i: (i // nt, 0, 0)),
                  pl.BlockSpec((1, 1, D), lambda i: (i // nt, 0, 0)),
                  pl.BlockSpec((1, D), lambda i: (0, 0)),
                  pl.BlockSpec((D, LANES), lambda i: (0, 0)),
                  pl.BlockSpec((1, LANES), lambda i: (0, 0))],
        out_specs=[pl.BlockSpec((TM_ROUTE, D), lambda i: (i, 0)),
                   pl.BlockSpec((TM_ROUTE, LANES), lambda i: (i, 0)),
                   pl.BlockSpec((TM_ROUTE, LANES), lambda i: (i, 0))],
        out_shape=[jax.ShapeDtypeStruct((N, D), F32),
                   jax.ShapeDtypeStruct((N, LANES), jnp.int32),
                   jax.ShapeDtypeStruct((N, LANES), F32)],
        compiler_params=_cparams(1),
        name="moe_router",
    )(x2d, sc, sh, gain, wr_pad, br_pad)


def _gather_kernel(tok_ref, h_hbm, o_ref, buf, sem):
    n = buf.shape[0]

    def row_copy(r, src_row):
        return pltpu.make_async_copy(h_hbm.at[pl.ds(src_row, 1), :], buf.at[pl.ds(r, 1), :], sem)

    def start(r, c):
        row_copy(r, tok_ref[0, 0, r]).start()
        return c

    def wait(r, c):
        row_copy(r, 0).wait()
        return c

    lax.fori_loop(0, n, start, 0)
    lax.fori_loop(0, n, wait, 0)
    o_ref[...] = buf[...].astype(BF16)


def _gather_rows(buf_tok, h):
    N, D = h.shape
    n_blk = buf_tok.shape[0]
    return pl.pallas_call(
        _gather_kernel,
        grid=(n_blk,),
        in_specs=[pl.BlockSpec((1, 1, MOE_ROWS), lambda i: (i, 0, 0), memory_space=pltpu.SMEM),
                  pl.BlockSpec(memory_space=pl.ANY)],
        out_specs=pl.BlockSpec((MOE_ROWS, D), lambda i: (i, 0)),
        out_shape=jax.ShapeDtypeStruct((n_blk * MOE_ROWS, D), BF16),
        scratch_shapes=[pltpu.VMEM((MOE_ROWS, D), F32), pltpu.SemaphoreType.DMA(())],
        compiler_params=_cparams(1),
        name="moe_gather",
    )(buf_tok, h)


def _expert_changed(be_ref, i):
    prev = be_ref[jnp.maximum(i - 1, 0)]
    return (i == 0) | (be_ref[i] != prev)


def _gmm1_kernel(be_ref, nu_ref, x_ref, wg_ref, wl_ref, bg_ref, bl_ref, o_ref, wg_scr, wl_scr):
    i = pl.program_id(1)

    @pl.when(i < nu_ref[0])
    def _():
        @pl.when(_expert_changed(be_ref, i))
        def _():
            wg_scr[...] = wg_ref[0].astype(BF16)
            wl_scr[...] = wl_ref[0].astype(BF16)

        x = x_ref[...]
        glu = jnp.dot(x, wg_scr[...], preferred_element_type=F32) + bg_ref[0]
        lin = jnp.dot(x, wl_scr[...], preferred_element_type=F32) + bl_ref[0]
        glu = jnp.minimum(glu, SWIGLU_LIMIT)
        lin = jnp.clip(lin, -SWIGLU_LIMIT, SWIGLU_LIMIT)
        o_ref[...] = (glu * jax.nn.sigmoid(SWIGLU_ALPHA * glu) * (lin + 1.0)).astype(BF16)

    @pl.when(i >= nu_ref[0])
    def _():
        o_ref[...] = jnp.zeros(o_ref.shape, o_ref.dtype)


def _gmm2_kernel(be_ref, nu_ref, a_ref, w_ref, b_ref, o_ref, w_scr):
    i = pl.program_id(1)

    @pl.when(i < nu_ref[0])
    def _():
        @pl.when(_expert_changed(be_ref, i))
        def _():
            w_scr[...] = w_ref[0].astype(BF16)

        o_ref[...] = jnp.dot(a_ref[...], w_scr[...], preferred_element_type=F32) + b_ref[0]

    @pl.when(i >= nu_ref[0])
    def _():
        o_ref[...] = jnp.zeros(o_ref.shape, o_ref.dtype)


def _experts(blk_expert, n_used, xs, w1, b1, w2, b2):
    P, D = xs.shape
    E, _, F2 = w1.shape
    F = F2 // 2
    n_blk = P // MOE_ROWS
    nf = F // TF
    act = pl.pallas_call(
        _gmm1_kernel,
        grid_spec=pltpu.PrefetchScalarGridSpec(
            num_scalar_prefetch=2,
            grid=(nf, n_blk),
            in_specs=[pl.BlockSpec((MOE_ROWS, D), lambda j, i, be, nu: (i, 0)),
                      pl.BlockSpec((1, D, TF), lambda j, i, be, nu: (be[i], 0, j)),
                      pl.BlockSpec((1, D, TF), lambda j, i, be, nu: (be[i], 0, nf + j)),
                      pl.BlockSpec((1, 1, TF), lambda j, i, be, nu: (be[i], 0, j)),
                      pl.BlockSpec((1, 1, TF), lambda j, i, be, nu: (be[i], 0, nf + j))],
            out_specs=pl.BlockSpec((MOE_ROWS, TF), lambda j, i, be, nu: (i, j)),
            scratch_shapes=[pltpu.VMEM((D, TF), BF16), pltpu.VMEM((D, TF), BF16)]),
        out_shape=jax.ShapeDtypeStruct((P, F), BF16),
        compiler_params=_cparams(2),
        name="moe_up",
    )(blk_expert, n_used, xs, w1, w1, b1.reshape(E, 1, F2), b1.reshape(E, 1, F2))
    return pl.pallas_call(
        _gmm2_kernel,
        grid_spec=pltpu.PrefetchScalarGridSpec(
            num_scalar_prefetch=2,
            grid=(D // TN_MOE, n_blk),
            in_specs=[pl.BlockSpec((MOE_ROWS, F), lambda j, i, be, nu: (i, 0)),
                      pl.BlockSpec((1, F, TN_MOE), lambda j, i, be, nu: (be[i], 0, j)),
                      pl.BlockSpec((1, 1, TN_MOE), lambda j, i, be, nu: (be[i], 0, j))],
            out_specs=pl.BlockSpec((MOE_ROWS, TN_MOE), lambda j, i, be, nu: (i, j)),
            scratch_shapes=[pltpu.VMEM((F, TN_MOE), BF16)]),
        out_shape=jax.ShapeDtypeStruct((P, D), F32),
        compiler_params=_cparams(2),
        name="moe_down",
    )(blk_expert, n_used, act, w2, b2.reshape(E, 1, D))


def _combine_kernel(pos_ref, y_hbm, x_ref, gate_ref, g2_ref, nf_ref, o_ref, buf, sem, *, final):
    n_tok = x_ref.shape[0]
    n_rows = TOP_K * n_tok

    def row_copy(a, src_row):
        dst = (a % TOP_K) * n_tok + a // TOP_K
        return pltpu.make_async_copy(y_hbm.at[pl.ds(src_row, 1), :], buf.at[pl.ds(dst, 1), :], sem)

    def start(a, c):
        row_copy(a, pos_ref[0, 0, a]).start()
        return c

    def wait(a, c):
        row_copy(a, 0).wait()
        return c

    lax.fori_loop(0, n_rows, start, 0)
    lax.fori_loop(0, n_rows, wait, 0)
    gate = gate_ref[...]
    moe = gate[:, 0:1] * buf[0:n_tok, :]
    for k in range(1, TOP_K):
        moe = moe + gate[:, k:k + 1] * buf[k * n_tok:(k + 1) * n_tok, :]
    x = x_ref[...] + g2_ref[0] * moe
    if final:
        ms = jnp.mean(x * x, axis=-1, keepdims=True)
        x = x * lax.rsqrt(ms + EPS) * nf_ref[...]
    o_ref[...] = x


def _combine(pos, y, x2d, gate, g2, norm_final, T, final):
    N, D = x2d.shape
    nt = T // TC
    return pl.pallas_call(
        functools.partial(_combine_kernel, final=final),
        grid=(N // TC,),
        in_specs=[pl.BlockSpec((1, 1, TOP_K * TC), lambda i: (i, 0, 0), memory_space=pltpu.SMEM),
                  pl.BlockSpec(memory_space=pl.ANY),
                  pl.BlockSpec((TC, D), lambda i: (i, 0)),
                  pl.BlockSpec((TC, LANES), lambda i: (i, 0)),
                  pl.BlockSpec((1, 1, D), lambda i: (i // nt, 0, 0)),
                  pl.BlockSpec((1, D), lambda i: (0, 0))],
        out_specs=pl.BlockSpec((TC, D), lambda i: (i, 0)),
        out_shape=jax.ShapeDtypeStruct((N, D), F32),
        scratch_shapes=[pltpu.VMEM((TOP_K * TC, D), F32), pltpu.SemaphoreType.DMA(())],
        compiler_params=_cparams(1),
        name="moe_combine_final" if final else "moe_combine",
    )(pos, y, x2d, gate, g2, norm_final)


def _routing_tables(top_idx):
    N = top_idx.shape[0]
    NK = N * TOP_K
    e_flat = top_idx.reshape(NK)
    onehot = (e_flat[:, None] == jnp.arange(N_EXPERTS, dtype=jnp.int32)[None, :]).astype(jnp.int32)
    csum = jnp.cumsum(onehot, axis=0)
    counts = csum[-1]
    rank = jnp.sum(csum * onehot, axis=1) - 1
    padded = (counts + MOE_ROWS - 1) // MOE_ROWS * MOE_ROWS
    pad_end = jnp.cumsum(padded)
    pad_start = pad_end - padded
    dest = jnp.sum(onehot * pad_start[None, :], axis=1) + rank
    P = NK + N_EXPERTS * MOE_ROWS
    n_blk = P // MOE_ROWS
    tok = jnp.arange(NK, dtype=jnp.int32) // TOP_K
    buf_tok = jnp.zeros((P,), jnp.int32).at[dest].set(tok)
    blk_start = jnp.arange(n_blk, dtype=jnp.int32) * MOE_ROWS
    blk_expert = jnp.minimum(jnp.sum(blk_start[:, None] >= pad_end[None, :], axis=-1), N_EXPERTS - 1)
    n_used = (pad_end[-1] // MOE_ROWS).reshape(1)
    return (dest.astype(jnp.int32), buf_tok.reshape(n_blk, 1, MOE_ROWS), blk_expert.astype(jnp.int32),
            n_used.astype(jnp.int32))


def _rope_tables(T):
    inv = 1.0 / (ROPE_THETA ** (jnp.arange(0, HEAD_DIM, 2, dtype=F32) / HEAD_DIM))
    ang = jnp.arange(T, dtype=F32)[:, None] * inv[None, :]
    return jnp.cos(ang), jnp.sin(ang)


def kernel(x, c, w_mod, b_mod, norm_attn, norm_ffn, w_in, cmp_pe, cmp_w1, cmp_w2, mix_norm, w_out,
           w_router, b_router, w_exp1, b_exp1, w_exp2, b_exp2, norm_final):
    B, T, D = x.shape
    N = B * T
    G = B_KV_HEADS
    assert D == D_MODEL and T % TM_PROJ == 0 and T % TQ == 0 and T % TM_ROUTE == 0

    cos, sin = _rope_tables(T)
    cos128 = jnp.tile(cos, (1, LANES // (HEAD_DIM // 2)))
    sin128 = jnp.tile(jnp.concatenate([-sin, sin], axis=1), (1, LANES // HEAD_DIM))
    n_cmp = (T - CMP_BLOCK) // CMP_STRIDE + 1
    n_chunk = T // CMP_STRIDE
    assert n_chunk == n_cmp + 1
    cmp_end = np.arange(n_chunk) * CMP_STRIDE + CMP_BLOCK - 1
    cmp_end = np.minimum(cmp_end, T - 1)
    cosc = jnp.stack([jnp.tile(cos[cmp_end], (1, 4)), jnp.ones((n_chunk, LANES), F32)])
    sinc = jnp.stack([jnp.tile(sin[cmp_end], (1, 4)), jnp.zeros((n_chunk, LANES), F32)])
    tbl = jnp.asarray(_dilated_mult_table())
    ov = jnp.asarray(_overlap_table(T))

    mod = _modulation(c, w_mod, b_mod).reshape(DEPTH, B, 6, 1, D)
    w_in_re = _permute_cols(w_in).astype(BF16)
    w_out_bf = w_out.astype(BF16)
    wr_pad = jnp.zeros((DEPTH, D, LANES), F32).at[:, :, :N_EXPERTS].set(w_router)
    br_pad = jnp.zeros((DEPTH, 1, LANES), F32).at[:, 0, :N_EXPERTS].set(b_router)
    half = HEAD_DIM // 2
    w2d = jnp.concatenate([cmp_w2, cmp_w2], axis=-1)
    w2rot = jnp.concatenate([-cmp_w2[..., half:], cmp_w2[..., :half]], axis=-1)
    w2r = jnp.concatenate([w2rot, w2rot], axis=-1)
    pe = cmp_pe.reshape(DEPTH, 2, 2, CMP_STRIDE * HEAD_DIM)

    xf = x.reshape(N, D)
    for i in range(DEPTH):
        sh1, sc1, g1, sh2, sc2, g2 = [mod[i, :, k] for k in range(6)]
        proj = _in_projection(xf, sc1, sh1, norm_attn[i][None, :], w_in_re[i], cos128, sin128, T)
        proj3 = proj.reshape(B, T, N_COLS)
        o_a = _attention_a(proj3, tbl)
        kcvc = proj3[:, :, BLK_KCVC * LANES:(BLK_KCVC + 3) * LANES]
        chunks = kcvc.reshape(B, T, 2, G, HEAD_DIM).transpose(0, 2, 3, 1, 4).reshape(
            B, 2, G, n_chunk, CMP_STRIDE * HEAD_DIM)
        cmp_kv = _compress(chunks, pe[i], cmp_w1[i], w2d[i], w2r[i], cosc, sinc)
        o_b = _attention_b(proj3, cmp_kv, ov)
        o_c = _attention_c(proj3)
        xf = _out_projection(o_a.reshape(N, -1), o_b.reshape(N, -1), o_c.reshape(N, -1),
                             mix_norm[i][None, :], w_out_bf[i], xf, g1, T)
        h2, idx128, gate128 = _router(xf, sc2, sh2, norm_ffn[i][None, :], wr_pad[i], br_pad[i], T)
        dest, buf_tok, blk_expert, n_used = _routing_tables(idx128[:, :TOP_K])
        xs = _gather_rows(buf_tok, h2)
        y = _experts(blk_expert, n_used, xs, w_exp1[i], b_exp1[i], w_exp2[i], b_exp2[i])
        xf = _combine(dest.reshape(N // TC, 1, TOP_K * TC), y, xf, gate128, g2,
                      norm_final[None, :], T, final=(i == DEPTH - 1))
    return xf.reshape(B, T, D)
```

```python
import functools

import numpy as np
import jax
import jax.numpy as jnp
from jax import lax
from jax.experimental import pallas as pl
from jax.experimental.pallas import tpu as pltpu

F32 = jnp.float32
BF16 = jnp.bfloat16
HIGHEST = lax.Precision.HIGHEST

D_MODEL = 2048
DEPTH = 2
HEAD_DIM = 64
A_HEADS = 12
A_PATTERNS = ((128, 1), (512, 4), (2048, 16))
B_HEADS = 12
B_KV_HEADS = 3
B_BRANCHES = 3
CMP_BLOCK = 32
CMP_STRIDE = 16
CMP_HIDDEN = 128
SLC_BLOCK = 64
SLC_TOPK = 16
WIN = 512
C_HEADS = 8
A_QKV_W = 3 * A_HEADS * HEAD_DIM
B_Q_W = B_HEADS * HEAD_DIM
B_KV_W = 2 * B_BRANCHES * B_KV_HEADS * HEAD_DIM
B_GATE_W = B_BRANCHES * B_HEADS
C_QKV_W = 3 * C_HEADS * HEAD_DIM
N_EXPERTS = 32
TOP_K = 4
D_FF = 2048
SWIGLU_LIMIT = 7.0
SWIGLU_ALPHA = 1.702
ROPE_THETA = 10000.0
EPS = 1e-6
NEG_BIG = -1e30
TINY = 1e-30
FORCE = 1e4
LOWEST = -3.0e38

LANES = 128
VMEM_LIMIT = 56 * 1024 * 1024

TQ_A, TK_A = 512, 512
TQ_B, TK_B = 256, 512
TQ_C, TK_C = 512, 256
TM_PROJ = 512
TN_IN = 768
TN_OUT = 512
TM_ROUTE = 256
MOE_ROWS = 256
TF = 512
TN_MOE = 512
TC = 128

BLK_AQ, BLK_AK, BLK_AV = 0, 6, 12
BLK_BQ = 18
BLK_KCVC = 24
BLK_KS, BLK_VS, BLK_KW, BLK_VW = 27, 30, 33, 36
BLK_CQ, BLK_CK, BLK_CV = 39, 43, 47
BLK_GATE = 51
N_BLKS = 54
N_COLS = N_BLKS * LANES


def _cparams(n_axes):
    return pltpu.CompilerParams(dimension_semantics=("arbitrary",) * n_axes,
                                vmem_limit_bytes=VMEM_LIMIT)


def _layout():
    a0 = 0
    bq0 = A_QKV_W
    bkv0 = bq0 + B_Q_W
    bg0 = bkv0 + B_KV_W
    c0 = bg0 + B_GATE_W
    scale = HEAD_DIM ** -0.5
    runs, rope, colscale = [], [], []

    def add(start, stop, r, s):
        runs.append((start, stop))
        n = stop - start
        rope.extend([r] * n)
        colscale.extend([s] * n)

    def pad(n):
        runs.append((None, n))
        rope.extend([0.0] * n)
        colscale.extend([1.0] * n)

    hw = A_HEADS * HEAD_DIM
    add(a0, a0 + hw, 1.0, scale)
    add(a0 + hw, a0 + 2 * hw, 1.0, 1.0)
    add(a0 + 2 * hw, a0 + 3 * hw, 0.0, 1.0)
    add(bq0, bq0 + B_Q_W, 1.0, scale)
    gw = B_KV_HEADS * HEAD_DIM
    add(bkv0, bkv0 + 2 * gw, 0.0, 1.0)
    for arr in (2, 3, 4, 5):
        r = 1.0 if arr in (2, 4) else 0.0
        for g in range(B_KV_HEADS):
            s0 = bkv0 + arr * gw + g * HEAD_DIM
            add(s0, s0 + HEAD_DIM, r, 1.0)
            add(s0, s0 + HEAD_DIM, r, 1.0)
    cw = C_HEADS * HEAD_DIM
    add(c0, c0 + cw, 0.0, scale)
    add(c0 + cw, c0 + 3 * cw, 0.0, 1.0)
    per_g = B_GATE_W // B_KV_HEADS
    for g in range(B_KV_HEADS):
        add(bg0 + g * per_g, bg0 + (g + 1) * per_g, 0.0, 1.0)
        pad(LANES - per_g)
    assert len(rope) == N_COLS
    return runs, np.asarray(rope, np.float32)[None, :], np.asarray(colscale, np.float32)[None, :]


_RUNS, _ROPE_FLAG, _COL_SCALE = _layout()


def _permute_cols(w):
    parts = []
    for start, stop in _RUNS:
        if start is None:
            parts.append(jnp.zeros(w.shape[:-1] + (stop,), w.dtype))
        else:
            parts.append(w[..., start:stop])
    return jnp.concatenate(parts, axis=-1)


def _dilated_bias_table():
    assert TQ_A == TK_A
    n = -(-A_PATTERNS[-1][0] // TK_A) + 1
    d = (np.arange(n)[:, None, None] * TK_A + np.arange(TQ_A)[None, :, None] - np.arange(TK_A)[None, None, :])
    m = np.zeros(d.shape, np.float64)
    for window, dil in A_PATTERNS:
        m += ((d >= 0) & (d <= window) & (d % dil == 0))
    return np.where(m > 0, np.log(np.maximum(m, 1.0)), NEG_BIG).astype(np.float32)


def _overlap_table_t(T):
    n_cmp = (T - CMP_BLOCK) // CMP_STRIDE + 1
    n_slc = T // SLC_BLOCK
    cs = np.arange(n_cmp) * CMP_STRIDE
    ss = np.arange(n_slc) * SLC_BLOCK
    ov = (cs[None, :] < ss[:, None] + SLC_BLOCK) & (cs[None, :] + CMP_BLOCK > ss[:, None])
    out = np.zeros((n_slc, n_cmp + 1), np.float32)
    out[:n_slc, :n_cmp] = ov
    return out


def _mod_kernel(c_ref, w_ref, b_ref, o_ref):
    c = c_ref[...]
    ca = (c * jax.nn.sigmoid(c)).astype(BF16)
    y = jnp.dot(ca, w_ref[0].astype(BF16), preferred_element_type=F32)
    o_ref[0] = y + b_ref[0]


def _modulation(c, w_mod, b_mod):
    B, D = c.shape
    L, _, W = w_mod.shape
    tn = 1024
    rows = 8
    cp = jnp.zeros((rows, D), F32).at[:B].set(c)
    out = pl.pallas_call(
        _mod_kernel,
        grid=(L, W // tn),
        in_specs=[pl.BlockSpec((rows, D), lambda l, j: (0, 0)),
                  pl.BlockSpec((1, D, tn), lambda l, j: (l, 0, j)),
                  pl.BlockSpec((1, 1, tn), lambda l, j: (l, 0, j))],
        out_specs=pl.BlockSpec((1, rows, tn), lambda l, j: (l, 0, j)),
        out_shape=jax.ShapeDtypeStruct((L, rows, W), F32),
        compiler_params=_cparams(2),
        name="adaln_mod",
    )(cp, w_mod, b_mod.reshape(L, 1, W))
    return out[:, :B]


def _norm_mod(x, gain, sc, sh):
    ms = jnp.mean(x * x, axis=-1, keepdims=True)
    y = x * lax.rsqrt(ms + EPS) * gain
    return y * (1.0 + sc) + sh


def _rope128(y, cos, sin_signed):
    lane = lax.broadcasted_iota(jnp.int32, y.shape, 1)
    first_half = (lane % HEAD_DIM) < (HEAD_DIM // 2)
    rot = jnp.where(first_half, pltpu.roll(y, LANES - HEAD_DIM // 2, 1), pltpu.roll(y, HEAD_DIM // 2, 1))
    return y * cos + rot * sin_signed


def _inproj_kernel(x_ref, sc_ref, sh_ref, g_ref, w_ref, cs_ref, rf_ref, cos_ref, sin_ref, o_ref, h_scr):
    @pl.when(pl.program_id(1) == 0)
    def _():
        h_scr[...] = _norm_mod(x_ref[...], g_ref[...], sc_ref[0], sh_ref[0]).astype(BF16)

    y = jnp.dot(h_scr[...], w_ref[...], preferred_element_type=F32)
    cos = cos_ref[...]
    sin = sin_ref[...]
    for c in range(TN_IN // LANES):
        sl = slice(c * LANES, (c + 1) * LANES)
        yc = y[:, sl]
        yr = _rope128(yc, cos, sin)
        o_ref[:, sl] = jnp.where(rf_ref[:, sl] > 0.5, yr, yc) * cs_ref[:, sl]


def _in_projection(x2d, sc, sh, gain, w_re, cos128, sin128, T):
    N, D = x2d.shape
    nt = T // TM_PROJ
    return pl.pallas_call(
        _inproj_kernel,
        grid=(N // TM_PROJ, N_COLS // TN_IN),
        in_specs=[pl.BlockSpec((TM_PROJ, D), lambda i, j: (i, 0)),
                  pl.BlockSpec((1, 1, D), lambda i, j: (i // nt, 0, 0)),
                  pl.BlockSpec((1, 1, D), lambda i, j: (i // nt, 0, 0)),
                  pl.BlockSpec((1, D), lambda i, j: (0, 0)),
                  pl.BlockSpec((D, TN_IN), lambda i, j: (0, j)),
                  pl.BlockSpec((1, TN_IN), lambda i, j: (0, j)),
                  pl.BlockSpec((1, TN_IN), lambda i, j: (0, j)),
                  pl.BlockSpec((TM_PROJ, LANES), lambda i, j: (i % nt, 0)),
                  pl.BlockSpec((TM_PROJ, LANES), lambda i, j: (i % nt, 0))],
        out_specs=pl.BlockSpec((TM_PROJ, TN_IN), lambda i, j: (i, j)),
        out_shape=jax.ShapeDtypeStruct((N, N_COLS), F32),
        scratch_shapes=[pltpu.VMEM((TM_PROJ, D), BF16)],
        compiler_params=_cparams(2),
        name="in_proj",
    )(x2d, sc, sh, gain, w_re, jnp.asarray(_COL_SCALE), jnp.asarray(_ROPE_FLAG), cos128, sin128)


def _split_heads_q(q):
    lane = lax.broadcasted_iota(jnp.int32, q.shape, 1)
    lo = lane < HEAD_DIM
    return jnp.concatenate([jnp.where(lo, q, 0.0), jnp.where(lo, 0.0, q)], axis=0)


def _scores(q2, k):
    return lax.dot_general(q2, k, (((1,), (1,)), ((), ())), preferred_element_type=F32)


def _flash_step(s, m, acc, v_ones):
    m_new = jnp.maximum(m, jnp.max(s, axis=-1, keepdims=True))
    p = jnp.exp(s - m_new).astype(BF16)
    acc = jnp.exp(m - m_new) * acc + jnp.dot(p, v_ones, preferred_element_type=F32)
    return m_new, acc


def _flash_init(rows):
    return jnp.full((rows, 1), NEG_BIG, F32), jnp.zeros((rows, LANES), F32)


def _flash_finish(acc):
    return acc / jnp.maximum(pltpu.roll(acc, HEAD_DIM, 1), TINY)


def _high_half(x):
    bits = lax.bitcast_convert_type(x, jnp.int32)
    return lax.bitcast_convert_type(bits & jnp.int32(-65536), F32)


def _tile(ref, kt, n):
    return ref[0, pl.ds(pl.multiple_of(kt * n, n), n), :]


def _two_head_norm(out):
    lane = lax.broadcasted_iota(jnp.int32, out.shape, 1)
    lo = lane < HEAD_DIM
    sq = out * out
    ms0 = jnp.sum(jnp.where(lo, sq, 0.0), axis=-1, keepdims=True) * (1.0 / HEAD_DIM)
    ms1 = jnp.sum(jnp.where(lo, 0.0, sq), axis=-1, keepdims=True) * (1.0 / HEAD_DIM)
    return out * jnp.where(lo, lax.rsqrt(ms0 + EPS), lax.rsqrt(ms1 + EPS))


def _attn_a_kernel(q_ref, k_ref, v_ref, tbl_ref, o_ref):
    qi = pl.program_id(2)
    q = q_ref[0]
    lane = lax.broadcasted_iota(jnp.int32, q.shape, 1)
    lo = lane < HEAD_DIM
    q0 = jnp.where(lo, q, 0.0).astype(BF16)
    q1 = jnp.where(lo, 0.0, q).astype(BF16)
    n_d = tbl_ref.shape[0]

    def body(i, carry):
        m0, acc0, m1, acc1 = carry
        k = _tile(k_ref, qi - i, TK_A).astype(BF16)
        v = _tile(v_ref, qi - i, TK_A)
        lo_k = lax.broadcasted_iota(jnp.int32, v.shape, 1) < HEAD_DIM
        bias = tbl_ref[i]
        m0, acc0 = _flash_step(_scores(q0, k) + bias, m0, acc0, jnp.where(lo_k, v, 1.0).astype(BF16))
        m1, acc1 = _flash_step(_scores(q1, k) + bias, m1, acc1, jnp.where(lo_k, 1.0, v).astype(BF16))
        return m0, acc0, m1, acc1

    init = _flash_init(TQ_A) + _flash_init(TQ_A)
    _, acc0, _, acc1 = lax.fori_loop(0, jnp.minimum(qi, n_d - 1) + 1, body, init)
    o_ref[0] = _two_head_norm(jnp.where(lo, _flash_finish(acc0), _flash_finish(acc1)))


def _attention_a(proj, tbl):
    B, T, _ = proj.shape
    n_pairs = A_HEADS // 2
    return pl.pallas_call(
        _attn_a_kernel,
        grid=(B, n_pairs, T // TQ_A),
        in_specs=[pl.BlockSpec((1, TQ_A, LANES), lambda b, p, i: (b, i, BLK_AQ + p)),
                  pl.BlockSpec((1, T, LANES), lambda b, p, i: (b, 0, BLK_AK + p)),
                  pl.BlockSpec((1, T, LANES), lambda b, p, i: (b, 0, BLK_AV + p)),
                  pl.BlockSpec(tbl.shape, lambda b, p, i: (0, 0, 0))],
        out_specs=pl.BlockSpec((1, TQ_A, LANES), lambda b, p, i: (b, i, p)),
        out_shape=jax.ShapeDtypeStruct((B, T, n_pairs * LANES), F32),
        compiler_params=_cparams(3),
        name="attn_dilated",
    )(proj, proj, proj, tbl)


def _attn_c_kernel(q_ref, k_ref, v_ref, o_ref):
    qi = pl.program_id(2)
    q = q_ref[0]
    lo = lax.broadcasted_iota(jnp.int32, q.shape, 1) < HEAD_DIM
    qs = (jnp.where(lo, q, 0.0).astype(BF16), jnp.where(lo, 0.0, q).astype(BF16))
    diff = (lax.broadcasted_iota(jnp.int32, (TQ_C, TK_C), 1) - lax.broadcasted_iota(jnp.int32, (TQ_C, TK_C), 0))
    uj = lax.broadcasted_iota(jnp.int32, (TK_C, TK_C), 0)
    us = lax.broadcasted_iota(jnp.int32, (TK_C, TK_C), 1)
    upper = (uj >= us).astype(BF16)
    tiles_per_q = TQ_C // TK_C

    def tile(kt, carry, masked):
        k = _tile(k_ref, kt, TK_C).astype(BF16)
        v = _tile(v_ref, kt, TK_C).astype(BF16)
        if masked:
            before = diff < qi * TQ_C - kt * TK_C
        out = []
        for qh, (run, acc) in zip(qs, carry):
            z = _scores(qh, k)
            sp = jnp.maximum(z, 0.0) + jnp.log(1.0 + jnp.exp(-jnp.abs(z)))
            if masked:
                sp = jnp.where(before, sp, 0.0)
            hi = _high_half(sp)
            suffix = (jnp.dot(hi.astype(BF16), upper, preferred_element_type=F32)
                      + jnp.dot((sp - hi).astype(BF16), upper, preferred_element_type=F32))
            a = jnp.exp(z - suffix - run)
            if masked:
                a = jnp.where(before, a, 0.0)
            out.append((run + suffix[:, 0:1], acc + jnp.dot(a.astype(BF16), v, preferred_element_type=F32)))
        return tuple(out)

    carry = ((jnp.zeros((TQ_C, 1), F32), jnp.zeros((TQ_C, LANES), F32)),) * 2
    last = (qi + 1) * tiles_per_q - 1
    for j in range(tiles_per_q):
        carry = tile(last - j, carry, True)
    carry = lax.fori_loop(tiles_per_q, last + 1, lambda i, c: tile(last - i, c, False), carry)
    o_ref[0] = _two_head_norm(jnp.where(lo, carry[0][1], carry[1][1]))


def _attention_c(proj):
    B, T, _ = proj.shape
    n_pairs = C_HEADS // 2
    return pl.pallas_call(
        _attn_c_kernel,
        grid=(B, n_pairs, T // TQ_C),
        in_specs=[pl.BlockSpec((1, TQ_C, LANES), lambda b, p, i: (b, i, BLK_CQ + p)),
                  pl.BlockSpec((1, T, LANES), lambda b, p, i: (b, 0, BLK_CK + p)),
                  pl.BlockSpec((1, T, LANES), lambda b, p, i: (b, 0, BLK_CV + p))],
        out_specs=pl.BlockSpec((1, TQ_C, LANES), lambda b, p, i: (b, i, p)),
        out_shape=jax.ShapeDtypeStruct((B, T, n_pairs * LANES), F32),
        compiler_params=_cparams(3),
        name="attn_stickbreak",
    )(proj, proj, proj)


def _compress_kernel(x_ref, pe_ref, w1_ref, w2_ref, w2r_ref, cos_ref, sin_ref, o_ref):
    x = x_ref[0, 0, 0]
    half = x.shape[1]
    ha = jnp.dot(x + pe_ref[0, 0:1, :], w1_ref[0, :half, :], precision=HIGHEST, preferred_element_type=F32)
    hb = jnp.dot(x + pe_ref[0, 1:2, :], w1_ref[0, half:, :], precision=HIGHEST, preferred_element_type=F32)
    n = x.shape[0]
    h = ha + pltpu.roll(hb, n - 1, 0)
    g = jax.nn.gelu(h)
    y = jnp.dot(g, w2_ref[0], precision=HIGHEST, preferred_element_type=F32)
    yr = jnp.dot(g, w2r_ref[0], precision=HIGHEST, preferred_element_type=F32)
    out = y * cos_ref[0] + yr * sin_ref[0]
    rowi = lax.broadcasted_iota(jnp.int32, out.shape, 0)
    o_ref[0, 0, 0] = jnp.where(rowi < n - 1, out, 0.0)


def _compress(chunks, pe, w1, w2d, w2r, cosc, sinc):
    B, _, G, n, cw = chunks.shape
    return pl.pallas_call(
        _compress_kernel,
        grid=(B, 2, G),
        in_specs=[pl.BlockSpec((1, 1, 1, n, cw), lambda b, w, g: (b, w, g, 0, 0)),
                  pl.BlockSpec((1, 2, cw), lambda b, w, g: (w, 0, 0)),
                  pl.BlockSpec((1, 2 * cw, CMP_HIDDEN), lambda b, w, g: (w, 0, 0)),
                  pl.BlockSpec((1, CMP_HIDDEN, LANES), lambda b, w, g: (w, 0, 0)),
                  pl.BlockSpec((1, CMP_HIDDEN, LANES), lambda b, w, g: (w, 0, 0)),
                  pl.BlockSpec((1, n, LANES), lambda b, w, g: (w, 0, 0)),
                  pl.BlockSpec((1, n, LANES), lambda b, w, g: (w, 0, 0))],
        out_specs=pl.BlockSpec((1, 1, 1, n, LANES), lambda b, w, g: (b, w, g, 0, 0)),
        out_shape=jax.ShapeDtypeStruct((B, 2, G, n, LANES), F32),
        compiler_params=_cparams(3),
        name="nsa_compress",
    )(chunks, pe, w1, w2d, w2r, cosc, sinc)


def _nsa_kernel(q_ref, cmp_k_ref, cmp_v_ref, ks_ref, vs_ref, kw_ref, vw_ref, gate_ref, ovt_ref, o_ref):
    qi = pl.program_id(2)
    R = B_HEADS // B_KV_HEADS
    TQ, TK = TQ_B, TK_B
    rows = R * TQ
    q0 = qi * TQ
    q = q_ref[0]
    qf = jnp.concatenate([_split_heads_q(q[:, :LANES]), _split_heads_q(q[:, LANES:])], axis=0)
    q4 = qf.astype(BF16)

    kc = cmp_k_ref[0, 0, 0]
    vc = cmp_v_ref[0, 0, 0].astype(BF16)
    n_c = kc.shape[0]
    last_end = (n_c - 1) * CMP_STRIDE + CMP_BLOCK - 1
    t_col = q0 + lax.broadcasted_iota(jnp.int32, (TQ, 1), 0)
    cmp_end = lax.broadcasted_iota(jnp.int32, (TQ, n_c), 1) * CMP_STRIDE + (CMP_BLOCK - 1)
    valid = (cmp_end <= t_col) & (cmp_end < last_end)
    t_lane = q0 + lax.broadcasted_iota(jnp.int32, (1, TQ), 1)
    nt_dims = (((1,), (1,)), ((), ()))
    p_sum = jnp.zeros((TQ, n_c), F32)
    o_cmp = []
    for r in range(R):
        qh = qf[r * TQ:(r + 1) * TQ]
        s = lax.dot_general(qh, kc, nt_dims, precision=HIGHEST, preferred_element_type=F32)
        s = jnp.where(valid, s, NEG_BIG)
        p = jnp.exp(s - jnp.max(s, axis=-1, keepdims=True)) * valid.astype(F32)
        p = p / jnp.maximum(jnp.sum(p, axis=-1, keepdims=True), TINY)
        o_cmp.append(jnp.dot(p.astype(BF16), vc, preferred_element_type=F32))
        p_sum = p_sum + p
    imp = lax.dot_general(ovt_ref[...], p_sum, nt_dims, precision=HIGHEST, preferred_element_type=F32)
    n_slc = imp.shape[0]
    blk = lax.broadcasted_iota(jnp.int32, (n_slc, TQ), 0)
    blk_f = blk.astype(F32)
    tb = t_lane // SLC_BLOCK
    forced = (blk == 0) | (blk == tb) | (blk == tb - 1)
    imp = jnp.where(forced, imp + FORCE, imp)
    imp = jnp.where(blk > tb, -FORCE, imp)
    sel = jnp.zeros((n_slc, TQ), F32)
    for _ in range(min(SLC_TOPK, n_slc)):
        mx = jnp.max(imp, axis=0, keepdims=True)
        first = jnp.min(jnp.where(imp == mx, blk_f, float(n_slc)), axis=0, keepdims=True)
        pick = blk_f == first
        sel = jnp.where(pick, 1.0, sel)
        imp = jnp.where(pick, LOWEST, imp)
    sel_bias = jnp.concatenate([(sel - 1.0) * (-NEG_BIG), jnp.zeros((LANES - n_slc, TQ), F32)], axis=0)
    sel_bias = sel_bias.T.astype(BF16)

    diff = lax.broadcasted_iota(jnp.int32, (TQ, TK), 1) - lax.broadcasted_iota(jnp.int32, (TQ, TK), 0)
    e_row = lax.broadcasted_iota(jnp.int32, (LANES, TK), 0)
    e_col = lax.broadcasted_iota(jnp.int32, (LANES, TK), 1) // SLC_BLOCK
    lo_k = lax.broadcasted_iota(jnp.int32, (TK, LANES), 1) < HEAD_DIM
    last = q0 // TK

    def step(k_ref, v_ref, kt, bias, carry):
        k = _tile(k_ref, kt, TK).astype(BF16)
        v_ones = jnp.where(lo_k, _tile(v_ref, kt, TK), 1.0).astype(BF16)
        s = (_scores(q4, k).reshape(R, TQ, TK) + bias[None]).reshape(rows, TK)
        return _flash_step(s, *carry, v_ones)

    def slc_tile(kt, carry, causal):
        expand = (e_row == kt * (TK // SLC_BLOCK) + e_col).astype(BF16)
        bias = jnp.dot(sel_bias, expand, preferred_element_type=F32)
        if causal:
            bias = bias + jnp.where(diff <= q0 - kt * TK, 0.0, NEG_BIG)
        return step(ks_ref, vs_ref, kt, bias, carry)

    carry = lax.fori_loop(0, last, lambda kt, c: slc_tile(kt, c, False), _flash_init(rows))
    o_slc = _flash_finish(slc_tile(last, carry, True)[1])

    def win_tile(i, carry):
        kt = last - i
        off = q0 - kt * TK
        bias = jnp.where((diff <= off) & (diff > off - WIN), 0.0, NEG_BIG)
        return step(kw_ref, vw_ref, kt, bias, carry)

    first_win = jnp.maximum(q0 - (WIN - 1), 0) // TK
    o_win = _flash_finish(lax.fori_loop(0, last - first_win + 1, win_tile, _flash_init(rows))[1])

    gates = jax.nn.sigmoid(gate_ref[0])
    lo = lax.broadcasted_iota(jnp.int32, (TQ, LANES), 1) < HEAD_DIM
    outs = []
    for r in range(R):
        rs = slice(r * TQ, (r + 1) * TQ)
        o = (gates[:, 3 * r:3 * r + 1] * o_cmp[r] + gates[:, 3 * r + 1:3 * r + 2] * o_slc[rs]
             + gates[:, 3 * r + 2:3 * r + 3] * o_win[rs])
        ms = jnp.sum(jnp.where(lo, o * o, 0.0), axis=-1, keepdims=True) * (1.0 / HEAD_DIM)
        outs.append(o * lax.rsqrt(ms + EPS))
    pairs = [jnp.where(lo, outs[2 * j], pltpu.roll(outs[2 * j + 1], HEAD_DIM, 1)) for j in range(R // 2)]
    o_ref[0] = jnp.concatenate(pairs, axis=1)


def _attention_b(proj, cmp_kv, ov):
    B, T, _ = proj.shape
    G = B_KV_HEADS
    n_c = cmp_kv.shape[3]
    qw = 2 * LANES
    TQ = TQ_B
    return pl.pallas_call(
        _nsa_kernel,
        grid=(B, G, T // TQ),
        in_specs=[pl.BlockSpec((1, TQ, qw), lambda b, g, i: (b, i, BLK_BQ // 2 + g)),
                  pl.BlockSpec((1, 1, 1, n_c, LANES), lambda b, g, i: (b, 0, g, 0, 0)),
                  pl.BlockSpec((1, 1, 1, n_c, LANES), lambda b, g, i: (b, 1, g, 0, 0)),
                  pl.BlockSpec((1, T, LANES), lambda b, g, i: (b, 0, BLK_KS + g)),
                  pl.BlockSpec((1, T, LANES), lambda b, g, i: (b, 0, BLK_VS + g)),
                  pl.BlockSpec((1, T, LANES), lambda b, g, i: (b, 0, BLK_KW + g)),
                  pl.BlockSpec((1, T, LANES), lambda b, g, i: (b, 0, BLK_VW + g)),
                  pl.BlockSpec((1, TQ, LANES), lambda b, g, i: (b, i, BLK_GATE + g)),
                  pl.BlockSpec(ov.shape, lambda b, g, i: (0, 0))],
        out_specs=pl.BlockSpec((1, TQ, qw), lambda b, g, i: (b, i, g)),
        out_shape=jax.ShapeDtypeStruct((B, T, B_HEADS * HEAD_DIM), F32),
        compiler_params=_cparams(3),
        name="attn_nsa",
    )(proj, cmp_kv, cmp_kv, proj, proj, proj, proj, proj, ov)


def _outproj_kernel(oa_ref, ob_ref, oc_ref, mn_ref, w_ref, x_ref, g_ref, o_ref, h_scr):
    wa = oa_ref.shape[1]
    wb = ob_ref.shape[1]

    @pl.when(pl.program_id(1) == 0)
    def _():
        h_scr[:, :wa] = (oa_ref[...] * mn_ref[:, :wa]).astype(BF16)
        h_scr[:, wa:wa + wb] = (ob_ref[...] * mn_ref[:, wa:wa + wb]).astype(BF16)
        h_scr[:, wa + wb:] = (oc_ref[...] * mn_ref[:, wa + wb:]).astype(BF16)

    y = jnp.dot(h_scr[...], w_ref[...], preferred_element_type=F32)
    o_ref[...] = x_ref[...] + g_ref[0] * y


def _out_projection(oa, ob, oc, mix_norm, w_out_bf, x2d, g1, T):
    N, D = x2d.shape
    nt = T // TM_PROJ
    wa, wb, wc = oa.shape[1], ob.shape[1], oc.shape[1]
    return pl.pallas_call(
        _outproj_kernel,
        grid=(N // TM_PROJ, D // TN_OUT),
        in_specs=[pl.BlockSpec((TM_PROJ, wa), lambda i, j: (i, 0)),
                  pl.BlockSpec((TM_PROJ, wb), lambda i, j: (i, 0)),
                  pl.BlockSpec((TM_PROJ, wc), lambda i, j: (i, 0)),
                  pl.BlockSpec((1, D), lambda i, j: (0, 0)),
                  pl.BlockSpec((D, TN_OUT), lambda i, j: (0, j)),
                  pl.BlockSpec((TM_PROJ, TN_OUT), lambda i, j: (i, j)),
                  pl.BlockSpec((1, 1, TN_OUT), lambda i, j: (i // nt, 0, j))],
        out_specs=pl.BlockSpec((TM_PROJ, TN_OUT), lambda i, j: (i, j)),
        out_shape=jax.ShapeDtypeStruct((N, D), F32),
        scratch_shapes=[pltpu.VMEM((TM_PROJ, D), BF16)],
        compiler_params=_cparams(2),
        name="out_proj",
    )(oa, ob, oc, mix_norm, w_out_bf, x2d, g1)


def _router_kernel(x_ref, sc_ref, sh_ref, g_ref, wr_ref, br_ref, h_ref, idx_ref, gate_ref):
    h = _norm_mod(x_ref[...], g_ref[...], sc_ref[0], sh_ref[0])
    h_ref[...] = h
    logits = jnp.dot(h, wr_ref[...], precision=HIGHEST, preferred_element_type=F32) + br_ref[...]
    lane = lax.broadcasted_iota(jnp.int32, logits.shape, 1)
    lane_f = lane.astype(F32)
    cur = jnp.where(lane < N_EXPERTS, logits, LOWEST)
    idx_out = jnp.zeros(logits.shape, F32)
    e_out = jnp.zeros(logits.shape, F32)
    top0 = None
    denom = None
    for k in range(TOP_K):
        mx = jnp.max(cur, axis=-1, keepdims=True)
        first = jnp.min(jnp.where(cur == mx, lane_f, float(LANES)), axis=-1, keepdims=True)
        cur = jnp.where(lane_f == first, LOWEST, cur)
        if k == 0:
            top0 = mx
        e = jnp.exp(mx - top0)
        denom = e if k == 0 else denom + e
        idx_out = jnp.where(lane == k, first, idx_out)
        e_out = jnp.where(lane == k, e, e_out)
    idx_ref[...] = idx_out.astype(jnp.int32)
    gate_ref[...] = e_out / denom


def _router(x2d, sc, sh, gain, wr_pad, br_pad, T):
    N, D = x2d.shape
    nt = T // TM_ROUTE
    return pl.pallas_call(
        _router_kernel,
        grid=(N // TM_ROUTE,),
        in_specs=[pl.BlockSpec((TM_ROUTE, D), lambda i: (i, 0)),
                  pl.BlockSpec((1, 1, D), lambda i: (i // nt, 0, 0)),
                  pl.BlockSpec((1, 1, D), lambda i: (i // nt, 0, 0)),
                  pl.BlockSpec((1, D), lambda i: (0, 0)),
                  pl.BlockSpec((D, LANES), lambda i: (0, 0)),
                  pl.BlockSpec((1, LANES), lambda i: (0, 0))],
        out_specs=[pl.BlockSpec((TM_ROUTE, D), lambda i: (i, 0)),
                   pl.BlockSpec((TM_ROUTE, LANES), lambda i: (i, 0)),
                   pl.BlockSpec((TM_ROUTE, LANES), lambda i: (i, 0))],
        out_shape=[jax.ShapeDtypeStruct((N, D), F32),
                   jax.ShapeDtypeStruct((N, LANES), jnp.int32),
                   jax.ShapeDtypeStruct((N, LANES), F32)],
        compiler_params=_cparams(1),
        name="moe_router",
    )(x2d, sc, sh, gain, wr_pad, br_pad)


DMA_UNROLL = 8


def _row_gather(src_hbm, idx_ref, buf, sem, slot, n_rows, dst_row, wait):
    def body(a8, c):
        for u in range(DMA_UNROLL):
            a = a8 * DMA_UNROLL + u
            src_row = 0 if wait else idx_ref[0, 0, a]
            cp = pltpu.make_async_copy(src_hbm.at[pl.ds(src_row, 1), :],
                                       buf.at[slot, pl.ds(dst_row(a), 1), :], sem.at[slot])
            if wait:
                cp.wait()
            else:
                cp.start()
        return c

    lax.fori_loop(0, n_rows // DMA_UNROLL, body, 0)


def _gather_kernel(nu_ref, tok_ref, nxt_ref, h_hbm, o_ref, buf, sem):
    i = pl.program_id(0)
    n_used = nu_ref[0]
    n = buf.shape[1]
    slot = i % 2
    ident = lambda a: a

    @pl.when(i == 0)
    def _():
        _row_gather(h_hbm, tok_ref, buf, sem, 0, n, ident, False)

    @pl.when(i + 1 < n_used)
    def _():
        _row_gather(h_hbm, nxt_ref, buf, sem, 1 - slot, n, ident, False)

    @pl.when(i < n_used)
    def _():
        _row_gather(h_hbm, tok_ref, buf, sem, slot, n, ident, True)
        o_ref[...] = buf[slot].astype(BF16)

    @pl.when(i >= n_used)
    def _():
        o_ref[...] = jnp.zeros(o_ref.shape, o_ref.dtype)


def _gather_rows(n_used, buf_tok, h):
    N, D = h.shape
    n_blk = buf_tok.shape[0]
    return pl.pallas_call(
        _gather_kernel,
        grid_spec=pltpu.PrefetchScalarGridSpec(
            num_scalar_prefetch=1,
            grid=(n_blk,),
            in_specs=[pl.BlockSpec((1, 1, MOE_ROWS), lambda i, nu: (i, 0, 0), memory_space=pltpu.SMEM),
                      pl.BlockSpec((1, 1, MOE_ROWS), lambda i, nu: (jnp.minimum(i + 1, n_blk - 1), 0, 0),
                                   memory_space=pltpu.SMEM),
                      pl.BlockSpec(memory_space=pl.ANY)],
            out_specs=pl.BlockSpec((MOE_ROWS, D), lambda i, nu: (i, 0)),
            scratch_shapes=[pltpu.VMEM((2, MOE_ROWS, D), F32), pltpu.SemaphoreType.DMA((2,))]),
        out_shape=jax.ShapeDtypeStruct((n_blk * MOE_ROWS, D), BF16),
        compiler_params=_cparams(1),
        name="moe_gather",
    )(n_used, buf_tok, buf_tok, h)


def _expert_changed(be_ref, i):
    prev = be_ref[jnp.maximum(i - 1, 0)]
    return (i == 0) | (be_ref[i] != prev)


def _gmm1_kernel(be_ref, nu_ref, x_ref, wg_ref, wl_ref, bg_ref, bl_ref, o_ref, wg_scr, wl_scr):
    i = pl.program_id(1)

    @pl.when(i < nu_ref[0])
    def _():
        @pl.when(_expert_changed(be_ref, i))
        def _():
            wg_scr[...] = wg_ref[0, 0].astype(BF16)
            wl_scr[...] = wl_ref[0, 0].astype(BF16)

        x = x_ref[...]
        glu = jnp.dot(x, wg_scr[...], preferred_element_type=F32) + bg_ref[0, 0]
        lin = jnp.dot(x, wl_scr[...], preferred_element_type=F32) + bl_ref[0, 0]
        glu = jnp.minimum(glu, SWIGLU_LIMIT)
        lin = jnp.clip(lin, -SWIGLU_LIMIT, SWIGLU_LIMIT)
        o_ref[...] = (glu * jax.nn.sigmoid(SWIGLU_ALPHA * glu) * (lin + 1.0)).astype(BF16)

    @pl.when(i >= nu_ref[0])
    def _():
        o_ref[...] = jnp.zeros(o_ref.shape, o_ref.dtype)


def _gmm2_kernel(be_ref, nu_ref, a_ref, w_ref, b_ref, o_ref, w_scr):
    i = pl.program_id(1)

    @pl.when(i < nu_ref[0])
    def _():
        @pl.when(_expert_changed(be_ref, i))
        def _():
            w_scr[...] = w_ref[0, 0].astype(BF16)

        o_ref[...] = jnp.dot(a_ref[...], w_scr[...], preferred_element_type=F32) + b_ref[0, 0]

    @pl.when(i >= nu_ref[0])
    def _():
        o_ref[...] = jnp.zeros(o_ref.shape, o_ref.dtype)


def _experts(layer, blk_expert, n_used, xs, w1, b1, w2, b2):
    P, D = xs.shape
    L, E, _, F2 = w1.shape
    F = F2 // 2
    n_blk = P // MOE_ROWS
    nf = F // TF
    b1r = b1.reshape(L, E, 1, F2)
    act = pl.pallas_call(
        _gmm1_kernel,
        grid_spec=pltpu.PrefetchScalarGridSpec(
            num_scalar_prefetch=2,
            grid=(nf, n_blk),
            in_specs=[pl.BlockSpec((MOE_ROWS, D), lambda j, i, be, nu: (i, 0)),
                      pl.BlockSpec((1, 1, D, TF), lambda j, i, be, nu: (layer, be[i], 0, j)),
                      pl.BlockSpec((1, 1, D, TF), lambda j, i, be, nu: (layer, be[i], 0, nf + j)),
                      pl.BlockSpec((1, 1, 1, TF), lambda j, i, be, nu: (layer, be[i], 0, j)),
                      pl.BlockSpec((1, 1, 1, TF), lambda j, i, be, nu: (layer, be[i], 0, nf + j))],
            out_specs=pl.BlockSpec((MOE_ROWS, TF), lambda j, i, be, nu: (i, j)),
            scratch_shapes=[pltpu.VMEM((D, TF), BF16), pltpu.VMEM((D, TF), BF16)]),
        out_shape=jax.ShapeDtypeStruct((P, F), BF16),
        compiler_params=_cparams(2),
        name="moe_up",
    )(blk_expert, n_used, xs, w1, w1, b1r, b1r)
    return pl.pallas_call(
        _gmm2_kernel,
        grid_spec=pltpu.PrefetchScalarGridSpec(
            num_scalar_prefetch=2,
            grid=(D // TN_MOE, n_blk),
            in_specs=[pl.BlockSpec((MOE_ROWS, F), lambda j, i, be, nu: (i, 0)),
                      pl.BlockSpec((1, 1, F, TN_MOE), lambda j, i, be, nu: (layer, be[i], 0, j)),
                      pl.BlockSpec((1, 1, 1, TN_MOE), lambda j, i, be, nu: (layer, be[i], 0, j))],
            out_specs=pl.BlockSpec((MOE_ROWS, TN_MOE), lambda j, i, be, nu: (i, j)),
            scratch_shapes=[pltpu.VMEM((F, TN_MOE), BF16)]),
        out_shape=jax.ShapeDtypeStruct((P, D), F32),
        compiler_params=_cparams(2),
        name="moe_down",
    )(blk_expert, n_used, act, w2, b2.reshape(L, E, 1, D))


def _combine_kernel(pos_ref, nxt_ref, y_hbm, x_ref, gate_ref, g2_ref, nf_ref, o_ref, buf, sem, *, final):
    i = pl.program_id(0)
    n_steps = pl.num_programs(0)
    n_tok = x_ref.shape[0]
    n_rows = TOP_K * n_tok
    slot = i % 2
    by_choice = lambda a: (a % TOP_K) * n_tok + a // TOP_K

    @pl.when(i == 0)
    def _():
        _row_gather(y_hbm, pos_ref, buf, sem, 0, n_rows, by_choice, False)

    @pl.when(i + 1 < n_steps)
    def _():
        _row_gather(y_hbm, nxt_ref, buf, sem, 1 - slot, n_rows, by_choice, False)

    _row_gather(y_hbm, pos_ref, buf, sem, slot, n_rows, by_choice, True)
    gate = gate_ref[...]
    moe = gate[:, 0:1] * buf[slot, 0:n_tok, :]
    for k in range(1, TOP_K):
        moe = moe + gate[:, k:k + 1] * buf[slot, k * n_tok:(k + 1) * n_tok, :]
    x = x_ref[...] + g2_ref[0] * moe
    if final:
        ms = jnp.mean(x * x, axis=-1, keepdims=True)
        x = x * lax.rsqrt(ms + EPS) * nf_ref[...]
    o_ref[...] = x


def _combine(pos, y, x2d, gate, g2, norm_final, T, final):
    N, D = x2d.shape
    nt = T // TC
    n_steps = N // TC
    return pl.pallas_call(
        functools.partial(_combine_kernel, final=final),
        grid=(n_steps,),
        in_specs=[pl.BlockSpec((1, 1, TOP_K * TC), lambda i: (i, 0, 0), memory_space=pltpu.SMEM),
                  pl.BlockSpec((1, 1, TOP_K * TC), lambda i: (jnp.minimum(i + 1, n_steps - 1), 0, 0),
                               memory_space=pltpu.SMEM),
                  pl.BlockSpec(memory_space=pl.ANY),
                  pl.BlockSpec((TC, D), lambda i: (i, 0)),
                  pl.BlockSpec((TC, LANES), lambda i: (i, 0)),
                  pl.BlockSpec((1, 1, D), lambda i: (i // nt, 0, 0)),
                  pl.BlockSpec((1, D), lambda i: (0, 0))],
        out_specs=pl.BlockSpec((TC, D), lambda i: (i, 0)),
        out_shape=jax.ShapeDtypeStruct((N, D), F32),
        scratch_shapes=[pltpu.VMEM((2, TOP_K * TC, D), F32), pltpu.SemaphoreType.DMA((2,))],
        compiler_params=_cparams(1),
        name="moe_combine_final" if final else "moe_combine",
    )(pos, pos, y, x2d, gate, g2, norm_final)


def _routing_tables(top_idx):
    N = top_idx.shape[0]
    NK = N * TOP_K
    e_flat = top_idx.reshape(NK)
    onehot = (e_flat[:, None] == jnp.arange(N_EXPERTS, dtype=jnp.int32)[None, :]).astype(jnp.int32)
    csum = jnp.cumsum(onehot, axis=0)
    counts = csum[-1]
    rank = jnp.sum(csum * onehot, axis=1) - 1
    padded = (counts + MOE_ROWS - 1) // MOE_ROWS * MOE_ROWS
    pad_end = jnp.cumsum(padded)
    pad_start = pad_end - padded
    dest = jnp.sum(onehot * pad_start[None, :], axis=1) + rank
    P = NK + N_EXPERTS * MOE_ROWS
    n_blk = P // MOE_ROWS
    tok = jnp.arange(NK, dtype=jnp.int32) // TOP_K
    buf_tok = jnp.zeros((P,), jnp.int32).at[dest].set(tok)
    blk_start = jnp.arange(n_blk, dtype=jnp.int32) * MOE_ROWS
    blk_expert = jnp.minimum(jnp.sum(blk_start[:, None] >= pad_end[None, :], axis=-1), N_EXPERTS - 1)
    n_used = (pad_end[-1] // MOE_ROWS).reshape(1)
    return (dest.astype(jnp.int32), buf_tok.reshape(n_blk, 1, MOE_ROWS), blk_expert.astype(jnp.int32),
            n_used.astype(jnp.int32))


def _rope_tables(T):
    inv = 1.0 / (ROPE_THETA ** (jnp.arange(0, HEAD_DIM, 2, dtype=F32) / HEAD_DIM))
    ang = jnp.arange(T, dtype=F32)[:, None] * inv[None, :]
    return jnp.cos(ang), jnp.sin(ang)


def kernel(x, c, w_mod, b_mod, norm_attn, norm_ffn, w_in, cmp_pe, cmp_w1, cmp_w2, mix_norm, w_out,
           w_router, b_router, w_exp1, b_exp1, w_exp2, b_exp2, norm_final):
    B, T, D = x.shape
    N = B * T
    G = B_KV_HEADS
    assert D == D_MODEL and T % TM_PROJ == 0 and T % TM_ROUTE == 0
    assert all(T % n == 0 for n in (TQ_A, TK_A, TQ_B, TK_B, TQ_C, TK_C)) and TK_B % TQ_B == 0 and TQ_C % TK_C == 0

    cos, sin = _rope_tables(T)
    cos128 = jnp.tile(cos, (1, LANES // (HEAD_DIM // 2)))
    sin128 = jnp.tile(jnp.concatenate([-sin, sin], axis=1), (1, LANES // HEAD_DIM))
    n_cmp = (T - CMP_BLOCK) // CMP_STRIDE + 1
    n_chunk = T // CMP_STRIDE
    assert n_chunk == n_cmp + 1
    cmp_end = np.arange(n_chunk) * CMP_STRIDE + CMP_BLOCK - 1
    cmp_end = np.minimum(cmp_end, T - 1)
    cosc = jnp.stack([jnp.tile(cos[cmp_end], (1, 4)), jnp.ones((n_chunk, LANES), F32)])
    sinc = jnp.stack([jnp.tile(sin[cmp_end], (1, 4)), jnp.zeros((n_chunk, LANES), F32)])
    tbl = jnp.asarray(_dilated_bias_table())
    ov = jnp.asarray(_overlap_table_t(T))

    mod = _modulation(c, w_mod, b_mod).reshape(DEPTH, B, 6, 1, D)
    w_in_re = _permute_cols(w_in).astype(BF16)
    w_out_bf = w_out.astype(BF16)
    wr_pad = jnp.zeros((DEPTH, D, LANES), F32).at[:, :, :N_EXPERTS].set(w_router)
    br_pad = jnp.zeros((DEPTH, 1, LANES), F32).at[:, 0, :N_EXPERTS].set(b_router)
    half = HEAD_DIM // 2
    w2d = jnp.concatenate([cmp_w2, cmp_w2], axis=-1)
    w2rot = jnp.concatenate([-cmp_w2[..., half:], cmp_w2[..., :half]], axis=-1)
    w2r = jnp.concatenate([w2rot, w2rot], axis=-1)
    pe = cmp_pe.reshape(DEPTH, 2, 2, CMP_STRIDE * HEAD_DIM)

    xf = x.reshape(N, D)
    for i in range(DEPTH):
        sh1, sc1, g1, sh2, sc2, g2 = [mod[i, :, k] for k in range(6)]
        proj = _in_projection(xf, sc1, sh1, norm_attn[i][None, :], w_in_re[i], cos128, sin128, T)
        proj3 = proj.reshape(B, T, N_COLS)
        o_a = _attention_a(proj3, tbl)
        kcvc = proj3[:, :, BLK_KCVC * LANES:(BLK_KCVC + 3) * LANES]
        chunks = kcvc.reshape(B, T, 2, G, HEAD_DIM).transpose(0, 2, 3, 1, 4).reshape(
            B, 2, G, n_chunk, CMP_STRIDE * HEAD_DIM)
        cmp_kv = _compress(chunks, pe[i], cmp_w1[i], w2d[i], w2r[i], cosc, sinc)
        o_b = _attention_b(proj3, cmp_kv, ov)
        o_c = _attention_c(proj3)
        xf = _out_projection(o_a.reshape(N, -1), o_b.reshape(N, -1), o_c.reshape(N, -1),
                             mix_norm[i][None, :], w_out_bf[i], xf, g1, T)
        h2, idx128, gate128 = _router(xf, sc2, sh2, norm_ffn[i][None, :], wr_pad[i], br_pad[i], T)
        dest, buf_tok, blk_expert, n_used = _routing_tables(idx128[:, :TOP_K])
        xs = _gather_rows(n_used, buf_tok, h2)
        y = _experts(i, blk_expert, n_used, xs, w_exp1, b_exp1, w_exp2, b_exp2)
        xf = _combine(dest.reshape(N // TC, 1, TOP_K * TC), y, xf, gate128, g2,
                      norm_final[None, :], T, final=(i == DEPTH - 1))
    return xf.reshape(B, T, D)
```

```python
import functools

import numpy as np
import jax
import jax.numpy as jnp
from jax import lax
from jax.experimental import pallas as pl
from jax.experimental.pallas import tpu as pltpu

F32 = jnp.float32
BF16 = jnp.bfloat16
HIGHEST = lax.Precision.HIGHEST

D_MODEL = 2048
DEPTH = 2
HEAD_DIM = 64
A_HEADS = 12
A_PATTERNS = ((128, 1), (512, 4), (2048, 16))
B_HEADS = 12
B_KV_HEADS = 3
B_BRANCHES = 3
CMP_BLOCK = 32
CMP_STRIDE = 16
CMP_HIDDEN = 128
SLC_BLOCK = 64
SLC_TOPK = 16
WIN = 512
C_HEADS = 8
A_QKV_W = 3 * A_HEADS * HEAD_DIM
B_Q_W = B_HEADS * HEAD_DIM
B_KV_W = 2 * B_BRANCHES * B_KV_HEADS * HEAD_DIM
B_GATE_W = B_BRANCHES * B_HEADS
C_QKV_W = 3 * C_HEADS * HEAD_DIM
N_EXPERTS = 32
TOP_K = 4
D_FF = 2048
SWIGLU_LIMIT = 7.0
SWIGLU_ALPHA = 1.702
ROPE_THETA = 10000.0
EPS = 1e-6
NEG_BIG = -1e30
TINY = 1e-30
FORCE = 1e4
LOWEST = -3.0e38

LANES = 128
VMEM_LIMIT = 56 * 1024 * 1024

TQ_A, TK_A = 512, 512
TQ_B, TK_B = 256, 512
TQ_C, TK_C = 512, 256
TM_PROJ = 512
TN_IN = 768
TN_OUT = 512
TM_ROUTE = 256
MOE_ROWS = 256
TF = 1024
TN_MOE = 2048
TC = 128

BLK_AQ, BLK_AK, BLK_AV = 0, 6, 12
BLK_BQ = 18
BLK_KCVC = 24
BLK_KS, BLK_VS, BLK_KW, BLK_VW = 27, 30, 33, 36
BLK_CQ, BLK_CK, BLK_CV = 39, 43, 47
BLK_GATE = 51
N_BLKS = 54
N_COLS = N_BLKS * LANES


def _cparams(n_axes):
    return pltpu.CompilerParams(dimension_semantics=("arbitrary",) * n_axes,
                                vmem_limit_bytes=VMEM_LIMIT)


def _layout():
    a0 = 0
    bq0 = A_QKV_W
    bkv0 = bq0 + B_Q_W
    bg0 = bkv0 + B_KV_W
    c0 = bg0 + B_GATE_W
    scale = HEAD_DIM ** -0.5
    runs, rope, colscale = [], [], []

    def add(start, stop, r, s):
        runs.append((start, stop))
        n = stop - start
        rope.extend([r] * n)
        colscale.extend([s] * n)

    def pad(n):
        runs.append((None, n))
        rope.extend([0.0] * n)
        colscale.extend([1.0] * n)

    hw = A_HEADS * HEAD_DIM
    add(a0, a0 + hw, 1.0, scale)
    add(a0 + hw, a0 + 2 * hw, 1.0, 1.0)
    add(a0 + 2 * hw, a0 + 3 * hw, 0.0, 1.0)
    add(bq0, bq0 + B_Q_W, 1.0, scale)
    gw = B_KV_HEADS * HEAD_DIM
    add(bkv0, bkv0 + 2 * gw, 0.0, 1.0)
    for arr in (2, 3, 4, 5):
        r = 1.0 if arr in (2, 4) else 0.0
        for g in range(B_KV_HEADS):
            s0 = bkv0 + arr * gw + g * HEAD_DIM
            add(s0, s0 + HEAD_DIM, r, 1.0)
            add(s0, s0 + HEAD_DIM, r, 1.0)
    cw = C_HEADS * HEAD_DIM
    add(c0, c0 + cw, 0.0, scale)
    add(c0 + cw, c0 + 3 * cw, 0.0, 1.0)
    per_g = B_GATE_W // B_KV_HEADS
    for g in range(B_KV_HEADS):
        add(bg0 + g * per_g, bg0 + (g + 1) * per_g, 0.0, 1.0)
        pad(LANES - per_g)
    assert len(rope) == N_COLS
    return runs, np.asarray(rope, np.float32)[None, :], np.asarray(colscale, np.float32)[None, :]


_RUNS, _ROPE_FLAG, _COL_SCALE = _layout()


def _permute_cols(w):
    parts = []
    for start, stop in _RUNS:
        if start is None:
            parts.append(jnp.zeros(w.shape[:-1] + (stop,), w.dtype))
        else:
            parts.append(w[..., start:stop])
    return jnp.concatenate(parts, axis=-1)


def _dilated_bias_table():
    assert TQ_A == TK_A
    n = -(-A_PATTERNS[-1][0] // TK_A) + 1
    d = (np.arange(n)[:, None, None] * TK_A + np.arange(TQ_A)[None, :, None] - np.arange(TK_A)[None, None, :])
    m = np.zeros(d.shape, np.float64)
    for window, dil in A_PATTERNS:
        m += ((d >= 0) & (d <= window) & (d % dil == 0))
    return np.where(m > 0, np.log(np.maximum(m, 1.0)), NEG_BIG).astype(np.float32)


def _overlap_table_t(T):
    n_cmp = (T - CMP_BLOCK) // CMP_STRIDE + 1
    n_slc = T // SLC_BLOCK
    cs = np.arange(n_cmp) * CMP_STRIDE
    ss = np.arange(n_slc) * SLC_BLOCK
    ov = (cs[None, :] < ss[:, None] + SLC_BLOCK) & (cs[None, :] + CMP_BLOCK > ss[:, None])
    out = np.zeros((n_slc, n_cmp + 1), np.float32)
    out[:n_slc, :n_cmp] = ov
    return out


def _mod_kernel(c_ref, w_ref, b_ref, o_ref):
    c = c_ref[...]
    ca = (c * jax.nn.sigmoid(c)).astype(BF16)
    y = jnp.dot(ca, w_ref[0].astype(BF16), preferred_element_type=F32)
    o_ref[0] = y + b_ref[0]


def _modulation(c, w_mod, b_mod):
    B, D = c.shape
    L, _, W = w_mod.shape
    tn = 1024
    rows = 8
    cp = jnp.zeros((rows, D), F32).at[:B].set(c)
    out = pl.pallas_call(
        _mod_kernel,
        grid=(L, W // tn),
        in_specs=[pl.BlockSpec((rows, D), lambda l, j: (0, 0)),
                  pl.BlockSpec((1, D, tn), lambda l, j: (l, 0, j)),
                  pl.BlockSpec((1, 1, tn), lambda l, j: (l, 0, j))],
        out_specs=pl.BlockSpec((1, rows, tn), lambda l, j: (l, 0, j)),
        out_shape=jax.ShapeDtypeStruct((L, rows, W), F32),
        compiler_params=_cparams(2),
        name="adaln_mod",
    )(cp, w_mod, b_mod.reshape(L, 1, W))
    return out[:, :B]


def _norm_mod(x, gain, sc, sh):
    ms = jnp.mean(x * x, axis=-1, keepdims=True)
    y = x * lax.rsqrt(ms + EPS) * gain
    return y * (1.0 + sc) + sh


def _rope128(y, cos, sin_signed):
    lane = lax.broadcasted_iota(jnp.int32, y.shape, 1)
    first_half = (lane % HEAD_DIM) < (HEAD_DIM // 2)
    rot = jnp.where(first_half, pltpu.roll(y, LANES - HEAD_DIM // 2, 1), pltpu.roll(y, HEAD_DIM // 2, 1))
    return y * cos + rot * sin_signed


def _inproj_kernel(x_ref, sc_ref, sh_ref, g_ref, w_ref, cs_ref, rf_ref, cos_ref, sin_ref, o_ref, h_scr):
    @pl.when(pl.program_id(1) == 0)
    def _():
        h_scr[...] = _norm_mod(x_ref[...], g_ref[...], sc_ref[0], sh_ref[0]).astype(BF16)

    y = jnp.dot(h_scr[...], w_ref[...], preferred_element_type=F32)
    cos = cos_ref[...]
    sin = sin_ref[...]
    for c in range(TN_IN // LANES):
        sl = slice(c * LANES, (c + 1) * LANES)
        yc = y[:, sl]
        yr = _rope128(yc, cos, sin)
        o_ref[:, sl] = jnp.where(rf_ref[:, sl] > 0.5, yr, yc) * cs_ref[:, sl]


def _in_projection(x2d, sc, sh, gain, w_re, cos128, sin128, T):
    N, D = x2d.shape
    nt = T // TM_PROJ
    return pl.pallas_call(
        _inproj_kernel,
        grid=(N // TM_PROJ, N_COLS // TN_IN),
        in_specs=[pl.BlockSpec((TM_PROJ, D), lambda i, j: (i, 0)),
                  pl.BlockSpec((1, 1, D), lambda i, j: (i // nt, 0, 0)),
                  pl.BlockSpec((1, 1, D), lambda i, j: (i // nt, 0, 0)),
                  pl.BlockSpec((1, D), lambda i, j: (0, 0)),
                  pl.BlockSpec((D, TN_IN), lambda i, j: (0, j)),
                  pl.BlockSpec((1, TN_IN), lambda i, j: (0, j)),
                  pl.BlockSpec((1, TN_IN), lambda i, j: (0, j)),
                  pl.BlockSpec((TM_PROJ, LANES), lambda i, j: (i % nt, 0)),
                  pl.BlockSpec((TM_PROJ, LANES), lambda i, j: (i % nt, 0))],
        out_specs=pl.BlockSpec((TM_PROJ, TN_IN), lambda i, j: (i, j)),
        out_shape=jax.ShapeDtypeStruct((N, N_COLS), F32),
        scratch_shapes=[pltpu.VMEM((TM_PROJ, D), BF16)],
        compiler_params=_cparams(2),
        name="in_proj",
    )(x2d, sc, sh, gain, w_re, jnp.asarray(_COL_SCALE), jnp.asarray(_ROPE_FLAG), cos128, sin128)


def _split_heads_q(q):
    lane = lax.broadcasted_iota(jnp.int32, q.shape, 1)
    lo = lane < HEAD_DIM
    return jnp.concatenate([jnp.where(lo, q, 0.0), jnp.where(lo, 0.0, q)], axis=0)


def _scores(q2, k):
    return lax.dot_general(q2, k, (((1,), (1,)), ((), ())), preferred_element_type=F32)


def _flash_step(s, m, acc, v_ones):
    m_new = jnp.maximum(m, jnp.max(s, axis=-1, keepdims=True))
    p = jnp.exp(s - m_new).astype(BF16)
    acc = jnp.exp(m - m_new) * acc + jnp.dot(p, v_ones, preferred_element_type=F32)
    return m_new, acc


def _flash_init(rows):
    return jnp.full((rows, 1), NEG_BIG, F32), jnp.zeros((rows, LANES), F32)


def _flash_finish(acc):
    return acc / jnp.maximum(pltpu.roll(acc, HEAD_DIM, 1), TINY)


def _high_half(x):
    bits = lax.bitcast_convert_type(x, jnp.int32)
    return lax.bitcast_convert_type(bits & jnp.int32(-65536), F32)


def _split_bf16(x):
    hi = x.astype(BF16)
    return hi, (x - hi.astype(F32)).astype(BF16)


def _tile(ref, kt, n):
    return ref[0, pl.ds(pl.multiple_of(kt * n, n), n), :]


def _two_head_norm(out):
    lane = lax.broadcasted_iota(jnp.int32, out.shape, 1)
    lo = lane < HEAD_DIM
    sq = out * out
    ms0 = jnp.sum(jnp.where(lo, sq, 0.0), axis=-1, keepdims=True) * (1.0 / HEAD_DIM)
    ms1 = jnp.sum(jnp.where(lo, 0.0, sq), axis=-1, keepdims=True) * (1.0 / HEAD_DIM)
    return out * jnp.where(lo, lax.rsqrt(ms0 + EPS), lax.rsqrt(ms1 + EPS))


def _attn_a_kernel(q_ref, k_ref, v_ref, tbl_ref, o_ref):
    qi = pl.program_id(2)
    q = q_ref[0]
    lane = lax.broadcasted_iota(jnp.int32, q.shape, 1)
    lo = lane < HEAD_DIM
    q0 = jnp.where(lo, q, 0.0).astype(BF16)
    q1 = jnp.where(lo, 0.0, q).astype(BF16)
    n_d = tbl_ref.shape[0]

    def body(i, carry):
        m0, acc0, m1, acc1 = carry
        k = _tile(k_ref, qi - i, TK_A).astype(BF16)
        v = _tile(v_ref, qi - i, TK_A)
        lo_k = lax.broadcasted_iota(jnp.int32, v.shape, 1) < HEAD_DIM
        bias = tbl_ref[i]
        m0, acc0 = _flash_step(_scores(q0, k) + bias, m0, acc0, jnp.where(lo_k, v, 1.0).astype(BF16))
        m1, acc1 = _flash_step(_scores(q1, k) + bias, m1, acc1, jnp.where(lo_k, 1.0, v).astype(BF16))
        return m0, acc0, m1, acc1

    init = _flash_init(TQ_A) + _flash_init(TQ_A)
    _, acc0, _, acc1 = lax.fori_loop(0, jnp.minimum(qi, n_d - 1) + 1, body, init)
    o_ref[0] = _two_head_norm(jnp.where(lo, _flash_finish(acc0), _flash_finish(acc1)))


def _attention_a(proj, tbl):
    B, T, _ = proj.shape
    n_pairs = A_HEADS // 2
    return pl.pallas_call(
        _attn_a_kernel,
        grid=(B, n_pairs, T // TQ_A),
        in_specs=[pl.BlockSpec((1, TQ_A, LANES), lambda b, p, i: (b, i, BLK_AQ + p)),
                  pl.BlockSpec((1, T, LANES), lambda b, p, i: (b, 0, BLK_AK + p)),
                  pl.BlockSpec((1, T, LANES), lambda b, p, i: (b, 0, BLK_AV + p)),
                  pl.BlockSpec(tbl.shape, lambda b, p, i: (0, 0, 0))],
        out_specs=pl.BlockSpec((1, TQ_A, LANES), lambda b, p, i: (b, i, p)),
        out_shape=jax.ShapeDtypeStruct((B, T, n_pairs * LANES), F32),
        compiler_params=_cparams(3),
        name="attn_dilated",
    )(proj, proj, proj, tbl)


def _attn_c_kernel(q_ref, k_ref, v_ref, o_ref):
    qi = pl.program_id(2)
    q = q_ref[0]
    lo = lax.broadcasted_iota(jnp.int32, q.shape, 1) < HEAD_DIM
    qs = (jnp.where(lo, q, 0.0).astype(BF16), jnp.where(lo, 0.0, q).astype(BF16))
    diff = (lax.broadcasted_iota(jnp.int32, (TQ_C, TK_C), 1) - lax.broadcasted_iota(jnp.int32, (TQ_C, TK_C), 0))
    uj = lax.broadcasted_iota(jnp.int32, (TK_C, TK_C), 0)
    us = lax.broadcasted_iota(jnp.int32, (TK_C, TK_C), 1)
    upper = (uj >= us).astype(BF16)
    tiles_per_q = TQ_C // TK_C

    def tile(kt, carry, masked):
        k = _tile(k_ref, kt, TK_C).astype(BF16)
        v = _tile(v_ref, kt, TK_C).astype(BF16)
        if masked:
            before = diff < qi * TQ_C - kt * TK_C
        out = []
        for qh, (run, acc) in zip(qs, carry):
            z = _scores(qh, k)
            sp = jnp.maximum(z, 0.0) + jnp.log(1.0 + jnp.exp(-jnp.abs(z)))
            if masked:
                sp = jnp.where(before, sp, 0.0)
            hi = _high_half(sp)
            suffix = (jnp.dot(hi.astype(BF16), upper, preferred_element_type=F32)
                      + jnp.dot((sp - hi).astype(BF16), upper, preferred_element_type=F32))
            a = jnp.exp(z - suffix - run)
            if masked:
                a = jnp.where(before, a, 0.0)
            out.append((run + suffix[:, 0:1], acc + jnp.dot(a.astype(BF16), v, preferred_element_type=F32)))
        return tuple(out)

    carry = ((jnp.zeros((TQ_C, 1), F32), jnp.zeros((TQ_C, LANES), F32)),) * 2
    last = (qi + 1) * tiles_per_q - 1
    for j in range(tiles_per_q):
        carry = tile(last - j, carry, True)
    carry = lax.fori_loop(tiles_per_q, last + 1, lambda i, c: tile(last - i, c, False), carry)
    o_ref[0] = _two_head_norm(jnp.where(lo, carry[0][1], carry[1][1]))


def _attention_c(proj):
    B, T, _ = proj.shape
    n_pairs = C_HEADS // 2
    return pl.pallas_call(
        _attn_c_kernel,
        grid=(B, n_pairs, T // TQ_C),
        in_specs=[pl.BlockSpec((1, TQ_C, LANES), lambda b, p, i: (b, i, BLK_CQ + p)),
                  pl.BlockSpec((1, T, LANES), lambda b, p, i: (b, 0, BLK_CK + p)),
                  pl.BlockSpec((1, T, LANES), lambda b, p, i: (b, 0, BLK_CV + p))],
        out_specs=pl.BlockSpec((1, TQ_C, LANES), lambda b, p, i: (b, i, p)),
        out_shape=jax.ShapeDtypeStruct((B, T, n_pairs * LANES), F32),
        compiler_params=_cparams(3),
        name="attn_stickbreak",
    )(proj, proj, proj)


def _compress_kernel(x_ref, pe_ref, w1_ref, w2_ref, w2r_ref, cos_ref, sin_ref, o_ref):
    x = x_ref[0, 0, 0]
    half = x.shape[1]
    ha = jnp.dot(x + pe_ref[0, 0:1, :], w1_ref[0, :half, :], precision=HIGHEST, preferred_element_type=F32)
    hb = jnp.dot(x + pe_ref[0, 1:2, :], w1_ref[0, half:, :], precision=HIGHEST, preferred_element_type=F32)
    n = x.shape[0]
    h = ha + pltpu.roll(hb, n - 1, 0)
    g = jax.nn.gelu(h)
    y = jnp.dot(g, w2_ref[0], precision=HIGHEST, preferred_element_type=F32)
    yr = jnp.dot(g, w2r_ref[0], precision=HIGHEST, preferred_element_type=F32)
    out = y * cos_ref[0] + yr * sin_ref[0]
    rowi = lax.broadcasted_iota(jnp.int32, out.shape, 0)
    o_ref[0, 0, 0] = jnp.where(rowi < n - 1, out, 0.0)


def _compress(chunks, pe, w1, w2d, w2r, cosc, sinc):
    B, _, G, n, cw = chunks.shape
    return pl.pallas_call(
        _compress_kernel,
        grid=(B, 2, G),
        in_specs=[pl.BlockSpec((1, 1, 1, n, cw), lambda b, w, g: (b, w, g, 0, 0)),
                  pl.BlockSpec((1, 2, cw), lambda b, w, g: (w, 0, 0)),
                  pl.BlockSpec((1, 2 * cw, CMP_HIDDEN), lambda b, w, g: (w, 0, 0)),
                  pl.BlockSpec((1, CMP_HIDDEN, LANES), lambda b, w, g: (w, 0, 0)),
                  pl.BlockSpec((1, CMP_HIDDEN, LANES), lambda b, w, g: (w, 0, 0)),
                  pl.BlockSpec((1, n, LANES), lambda b, w, g: (w, 0, 0)),
                  pl.BlockSpec((1, n, LANES), lambda b, w, g: (w, 0, 0))],
        out_specs=pl.BlockSpec((1, 1, 1, n, LANES), lambda b, w, g: (b, w, g, 0, 0)),
        out_shape=jax.ShapeDtypeStruct((B, 2, G, n, LANES), F32),
        compiler_params=_cparams(3),
        name="nsa_compress",
    )(chunks, pe, w1, w2d, w2r, cosc, sinc)


def _nsa_kernel(q_ref, cmp_k_ref, cmp_v_ref, ks_ref, vs_ref, kw_ref, vw_ref, gate_ref, ovt_ref, o_ref):
    qi = pl.program_id(2)
    R = B_HEADS // B_KV_HEADS
    TQ, TK = TQ_B, TK_B
    rows = R * TQ
    q0 = qi * TQ
    q = q_ref[0]
    qf = jnp.concatenate([_split_heads_q(q[:, :LANES]), _split_heads_q(q[:, LANES:])], axis=0)
    q4 = qf.astype(BF16)

    kc = cmp_k_ref[0, 0, 0]
    vc = cmp_v_ref[0, 0, 0].astype(BF16)
    n_c = kc.shape[0]
    last_end = (n_c - 1) * CMP_STRIDE + CMP_BLOCK - 1
    t_col = q0 + lax.broadcasted_iota(jnp.int32, (TQ, 1), 0)
    cmp_end = lax.broadcasted_iota(jnp.int32, (TQ, n_c), 1) * CMP_STRIDE + (CMP_BLOCK - 1)
    valid = (cmp_end <= t_col) & (cmp_end < last_end)
    t_lane = q0 + lax.broadcasted_iota(jnp.int32, (1, TQ), 1)
    nt_dims = (((1,), (1,)), ((), ()))
    p_sum = jnp.zeros((TQ, n_c), F32)
    o_cmp = []
    kc_hi, kc_lo = _split_bf16(kc)
    for r in range(R):
        q_hi, q_lo = _split_bf16(qf[r * TQ:(r + 1) * TQ])
        s = _scores(q_hi, kc_hi) + (_scores(q_hi, kc_lo) + _scores(q_lo, kc_hi))
        s = jnp.where(valid, s, NEG_BIG)
        p = jnp.exp(s - jnp.max(s, axis=-1, keepdims=True)) * valid.astype(F32)
        p = p / jnp.maximum(jnp.sum(p, axis=-1, keepdims=True), TINY)
        o_cmp.append(jnp.dot(p.astype(BF16), vc, preferred_element_type=F32))
        p_sum = p_sum + p
    imp = lax.dot_general(ovt_ref[...], p_sum, nt_dims, precision=HIGHEST, preferred_element_type=F32)
    n_slc = imp.shape[0]
    blk = lax.broadcasted_iota(jnp.int32, (n_slc, TQ), 0)
    blk_f = blk.astype(F32)
    tb = t_lane // SLC_BLOCK
    forced = (blk == 0) | (blk == tb) | (blk == tb - 1)
    imp = jnp.where(forced, imp + FORCE, imp)
    imp = jnp.where(blk > tb, -FORCE, imp)
    sel = jnp.zeros((n_slc, TQ), F32)
    for _ in range(min(SLC_TOPK, n_slc)):
        mx = jnp.max(imp, axis=0, keepdims=True)
        first = jnp.min(jnp.where(imp == mx, blk_f, float(n_slc)), axis=0, keepdims=True)
        pick = blk_f == first
        sel = jnp.where(pick, 1.0, sel)
        imp = jnp.where(pick, LOWEST, imp)
    sel_bias = jnp.concatenate([(sel - 1.0) * (-NEG_BIG), jnp.zeros((LANES - n_slc, TQ), F32)], axis=0)
    sel_bias = sel_bias.T.astype(BF16)

    diff = lax.broadcasted_iota(jnp.int32, (TQ, TK), 1) - lax.broadcasted_iota(jnp.int32, (TQ, TK), 0)
    e_row = lax.broadcasted_iota(jnp.int32, (LANES, TK), 0)
    e_col = lax.broadcasted_iota(jnp.int32, (LANES, TK), 1) // SLC_BLOCK
    lo_k = lax.broadcasted_iota(jnp.int32, (TK, LANES), 1) < HEAD_DIM
    last = q0 // TK

    def step(k_ref, v_ref, kt, bias, carry):
        k = _tile(k_ref, kt, TK).astype(BF16)
        v_ones = jnp.where(lo_k, _tile(v_ref, kt, TK), 1.0).astype(BF16)
        s = (_scores(q4, k).reshape(R, TQ, TK) + bias[None]).reshape(rows, TK)
        return _flash_step(s, *carry, v_ones)

    def slc_tile(kt, carry, causal):
        expand = (e_row == kt * (TK // SLC_BLOCK) + e_col).astype(BF16)
        bias = jnp.dot(sel_bias, expand, preferred_element_type=F32)
        if causal:
            bias = bias + jnp.where(diff <= q0 - kt * TK, 0.0, NEG_BIG)
        return step(ks_ref, vs_ref, kt, bias, carry)

    carry = lax.fori_loop(0, last, lambda kt, c: slc_tile(kt, c, False), _flash_init(rows))
    o_slc = _flash_finish(slc_tile(last, carry, True)[1])

    def win_tile(i, carry):
        kt = last - i
        off = q0 - kt * TK
        bias = jnp.where((diff <= off) & (diff > off - WIN), 0.0, NEG_BIG)
        return step(kw_ref, vw_ref, kt, bias, carry)

    first_win = jnp.maximum(q0 - (WIN - 1), 0) // TK
    o_win = _flash_finish(lax.fori_loop(0, last - first_win + 1, win_tile, _flash_init(rows))[1])

    gates = jax.nn.sigmoid(gate_ref[0])
    lo = lax.broadcasted_iota(jnp.int32, (TQ, LANES), 1) < HEAD_DIM
    outs = []
    for r in range(R):
        rs = slice(r * TQ, (r + 1) * TQ)
        o = (gates[:, 3 * r:3 * r + 1] * o_cmp[r] + gates[:, 3 * r + 1:3 * r + 2] * o_slc[rs]
             + gates[:, 3 * r + 2:3 * r + 3] * o_win[rs])
        ms = jnp.sum(jnp.where(lo, o * o, 0.0), axis=-1, keepdims=True) * (1.0 / HEAD_DIM)
        outs.append(o * lax.rsqrt(ms + EPS))
    pairs = [jnp.where(lo, outs[2 * j], pltpu.roll(outs[2 * j + 1], HEAD_DIM, 1)) for j in range(R // 2)]
    o_ref[0] = jnp.concatenate(pairs, axis=1)


def _attention_b(proj, cmp_kv, ov):
    B, T, _ = proj.shape
    G = B_KV_HEADS
    n_c = cmp_kv.shape[3]
    qw = 2 * LANES
    TQ = TQ_B
    return pl.pallas_call(
        _nsa_kernel,
        grid=(B, G, T // TQ),
        in_specs=[pl.BlockSpec((1, TQ, qw), lambda b, g, i: (b, i, BLK_BQ // 2 + g)),
                  pl.BlockSpec((1, 1, 1, n_c, LANES), lambda b, g, i: (b, 0, g, 0, 0)),
                  pl.BlockSpec((1, 1, 1, n_c, LANES), lambda b, g, i: (b, 1, g, 0, 0)),
                  pl.BlockSpec((1, T, LANES), lambda b, g, i: (b, 0, BLK_KS + g)),
                  pl.BlockSpec((1, T, LANES), lambda b, g, i: (b, 0, BLK_VS + g)),
                  pl.BlockSpec((1, T, LANES), lambda b, g, i: (b, 0, BLK_KW + g)),
                  pl.BlockSpec((1, T, LANES), lambda b, g, i: (b, 0, BLK_VW + g)),
                  pl.BlockSpec((1, TQ, LANES), lambda b, g, i: (b, i, BLK_GATE + g)),
                  pl.BlockSpec(ov.shape, lambda b, g, i: (0, 0))],
        out_specs=pl.BlockSpec((1, TQ, qw), lambda b, g, i: (b, i, g)),
        out_shape=jax.ShapeDtypeStruct((B, T, B_HEADS * HEAD_DIM), F32),
        compiler_params=_cparams(3),
        name="attn_nsa",
    )(proj, cmp_kv, cmp_kv, proj, proj, proj, proj, proj, ov)


def _outproj_kernel(oa_ref, ob_ref, oc_ref, mn_ref, w_ref, x_ref, g_ref, o_ref, h_scr):
    wa = oa_ref.shape[1]
    wb = ob_ref.shape[1]

    @pl.when(pl.program_id(1) == 0)
    def _():
        h_scr[:, :wa] = (oa_ref[...] * mn_ref[:, :wa]).astype(BF16)
        h_scr[:, wa:wa + wb] = (ob_ref[...] * mn_ref[:, wa:wa + wb]).astype(BF16)
        h_scr[:, wa + wb:] = (oc_ref[...] * mn_ref[:, wa + wb:]).astype(BF16)

    y = jnp.dot(h_scr[...], w_ref[...], preferred_element_type=F32)
    o_ref[...] = x_ref[...] + g_ref[0] * y


def _out_projection(oa, ob, oc, mix_norm, w_out_bf, x2d, g1, T):
    N, D = x2d.shape
    nt = T // TM_PROJ
    wa, wb, wc = oa.shape[1], ob.shape[1], oc.shape[1]
    return pl.pallas_call(
        _outproj_kernel,
        grid=(N // TM_PROJ, D // TN_OUT),
        in_specs=[pl.BlockSpec((TM_PROJ, wa), lambda i, j: (i, 0)),
                  pl.BlockSpec((TM_PROJ, wb), lambda i, j: (i, 0)),
                  pl.BlockSpec((TM_PROJ, wc), lambda i, j: (i, 0)),
                  pl.BlockSpec((1, D), lambda i, j: (0, 0)),
                  pl.BlockSpec((D, TN_OUT), lambda i, j: (0, j)),
                  pl.BlockSpec((TM_PROJ, TN_OUT), lambda i, j: (i, j)),
                  pl.BlockSpec((1, 1, TN_OUT), lambda i, j: (i // nt, 0, j))],
        out_specs=pl.BlockSpec((TM_PROJ, TN_OUT), lambda i, j: (i, j)),
        out_shape=jax.ShapeDtypeStruct((N, D), F32),
        scratch_shapes=[pltpu.VMEM((TM_PROJ, D), BF16)],
        compiler_params=_cparams(2),
        name="out_proj",
    )(oa, ob, oc, mix_norm, w_out_bf, x2d, g1)


def _router_kernel(x_ref, sc_ref, sh_ref, g_ref, wr_ref, br_ref, h_ref, idx_ref, gate_ref):
    h = _norm_mod(x_ref[...], g_ref[...], sc_ref[0], sh_ref[0])
    h_ref[...] = h
    logits = jnp.dot(h, wr_ref[...], precision=HIGHEST, preferred_element_type=F32) + br_ref[...]
    lane = lax.broadcasted_iota(jnp.int32, logits.shape, 1)
    lane_f = lane.astype(F32)
    cur = jnp.where(lane < N_EXPERTS, logits, LOWEST)
    idx_out = jnp.zeros(logits.shape, F32)
    e_out = jnp.zeros(logits.shape, F32)
    top0 = None
    denom = None
    for k in range(TOP_K):
        mx = jnp.max(cur, axis=-1, keepdims=True)
        first = jnp.min(jnp.where(cur == mx, lane_f, float(LANES)), axis=-1, keepdims=True)
        cur = jnp.where(lane_f == first, LOWEST, cur)
        if k == 0:
            top0 = mx
        e = jnp.exp(mx - top0)
        denom = e if k == 0 else denom + e
        idx_out = jnp.where(lane == k, first, idx_out)
        e_out = jnp.where(lane == k, e, e_out)
    idx_ref[...] = idx_out.astype(jnp.int32)
    gate_ref[...] = e_out / denom


def _router(x2d, sc, sh, gain, wr_pad, br_pad, T):
    N, D = x2d.shape
    nt = T // TM_ROUTE
    return pl.pallas_call(
        _router_kernel,
        grid=(N // TM_ROUTE,),
        in_specs=[pl.BlockSpec((TM_ROUTE, D), lambda i: (i, 0)),
                  pl.BlockSpec((1, 1, D), lambda i: (i // nt, 0, 0)),
                  pl.BlockSpec((1, 1, D), lambda i: (i // nt, 0, 0)),
                  pl.BlockSpec((1, D), lambda i: (0, 0)),
                  pl.BlockSpec((D, LANES), lambda i: (0, 0)),
                  pl.BlockSpec((1, LANES), lambda i: (0, 0))],
        out_specs=[pl.BlockSpec((TM_ROUTE, D), lambda i: (i, 0)),
                   pl.BlockSpec((TM_ROUTE, LANES), lambda i: (i, 0)),
                   pl.BlockSpec((TM_ROUTE, LANES), lambda i: (i, 0))],
        out_shape=[jax.ShapeDtypeStruct((N, D), F32),
                   jax.ShapeDtypeStruct((N, LANES), jnp.int32),
                   jax.ShapeDtypeStruct((N, LANES), F32)],
        compiler_params=_cparams(1),
        name="moe_router",
    )(x2d, sc, sh, gain, wr_pad, br_pad)


DMA_UNROLL = 8


def _row_gather(src_hbm, idx_ref, buf, sem, slot, n_rows, dst_row, wait):
    if wait:
        pltpu.make_async_copy(src_hbm.at[pl.ds(0, n_rows), :], buf.at[slot], sem.at[slot]).wait()
        return

    def body(a8, c):
        for u in range(DMA_UNROLL):
            a = a8 * DMA_UNROLL + u
            pltpu.make_async_copy(src_hbm.at[pl.ds(idx_ref[0, 0, a], 1), :],
                                  buf.at[slot, pl.ds(dst_row(a), 1), :], sem.at[slot]).start()
        return c

    lax.fori_loop(0, n_rows // DMA_UNROLL, body, 0)


def _gather_kernel(nu_ref, tok_ref, nxt_ref, h_hbm, o_ref, buf, sem):
    i = pl.program_id(0)
    n_used = nu_ref[0]
    n = buf.shape[1]
    slot = i % 2
    ident = lambda a: a

    @pl.when(i == 0)
    def _():
        _row_gather(h_hbm, tok_ref, buf, sem, 0, n, ident, False)

    @pl.when(i + 1 < n_used)
    def _():
        _row_gather(h_hbm, nxt_ref, buf, sem, 1 - slot, n, ident, False)

    @pl.when(i < n_used)
    def _():
        _row_gather(h_hbm, tok_ref, buf, sem, slot, n, ident, True)
        o_ref[...] = buf[slot].astype(BF16)

    @pl.when(i >= n_used)
    def _():
        o_ref[...] = jnp.zeros(o_ref.shape, o_ref.dtype)


def _gather_rows(n_used, buf_tok, h):
    N, D = h.shape
    n_blk = buf_tok.shape[0]
    return pl.pallas_call(
        _gather_kernel,
        grid_spec=pltpu.PrefetchScalarGridSpec(
            num_scalar_prefetch=1,
            grid=(n_blk,),
            in_specs=[pl.BlockSpec((1, 1, MOE_ROWS), lambda i, nu: (i, 0, 0), memory_space=pltpu.SMEM),
                      pl.BlockSpec((1, 1, MOE_ROWS), lambda i, nu: (jnp.minimum(i + 1, n_blk - 1), 0, 0),
                                   memory_space=pltpu.SMEM),
                      pl.BlockSpec(memory_space=pl.ANY)],
            out_specs=pl.BlockSpec((MOE_ROWS, D), lambda i, nu: (i, 0)),
            scratch_shapes=[pltpu.VMEM((2, MOE_ROWS, D), F32), pltpu.SemaphoreType.DMA((2,))]),
        out_shape=jax.ShapeDtypeStruct((n_blk * MOE_ROWS, D), BF16),
        compiler_params=_cparams(1),
        name="moe_gather",
    )(n_used, buf_tok, buf_tok, h)


def _expert_changed(be_ref, i):
    prev = be_ref[jnp.maximum(i - 1, 0)]
    return (i == 0) | (be_ref[i] != prev)


def _gmm1_kernel(be_ref, nu_ref, x_ref, wg_ref, wl_ref, bg_ref, bl_ref, o_ref, wg_scr, wl_scr):
    i = pl.program_id(1)

    @pl.when(i < nu_ref[0])
    def _():
        @pl.when(_expert_changed(be_ref, i))
        def _():
            wg_scr[...] = wg_ref[0, 0].astype(BF16)
            wl_scr[...] = wl_ref[0, 0].astype(BF16)

        x = x_ref[...]
        glu = jnp.dot(x, wg_scr[...], preferred_element_type=F32) + bg_ref[0, 0]
        lin = jnp.dot(x, wl_scr[...], preferred_element_type=F32) + bl_ref[0, 0]
        glu = jnp.minimum(glu, SWIGLU_LIMIT)
        lin = jnp.clip(lin, -SWIGLU_LIMIT, SWIGLU_LIMIT)
        o_ref[...] = (glu * jax.nn.sigmoid(SWIGLU_ALPHA * glu) * (lin + 1.0)).astype(BF16)

    @pl.when(i >= nu_ref[0])
    def _():
        o_ref[...] = jnp.zeros(o_ref.shape, o_ref.dtype)


def _gmm2_kernel(be_ref, nu_ref, a_ref, w_ref, b_ref, o_ref, w_scr):
    i = pl.program_id(1)

    @pl.when(i < nu_ref[0])
    def _():
        @pl.when(_expert_changed(be_ref, i))
        def _():
            w_scr[...] = w_ref[0, 0].astype(BF16)

        o_ref[...] = jnp.dot(a_ref[...], w_scr[...], preferred_element_type=F32) + b_ref[0, 0]

    @pl.when(i >= nu_ref[0])
    def _():
        o_ref[...] = jnp.zeros(o_ref.shape, o_ref.dtype)


def _experts(layer, blk_expert, n_used, xs, w1, b1, w2, b2):
    P, D = xs.shape
    L, E, _, F2 = w1.shape
    F = F2 // 2
    n_blk = P // MOE_ROWS
    nf = F // TF
    b1r = b1.reshape(L, E, 1, F2)
    act = pl.pallas_call(
        _gmm1_kernel,
        grid_spec=pltpu.PrefetchScalarGridSpec(
            num_scalar_prefetch=2,
            grid=(nf, n_blk),
            in_specs=[pl.BlockSpec((MOE_ROWS, D), lambda j, i, be, nu: (i, 0)),
                      pl.BlockSpec((1, 1, D, TF), lambda j, i, be, nu: (layer, be[i], 0, j)),
                      pl.BlockSpec((1, 1, D, TF), lambda j, i, be, nu: (layer, be[i], 0, nf + j)),
                      pl.BlockSpec((1, 1, 1, TF), lambda j, i, be, nu: (layer, be[i], 0, j)),
                      pl.BlockSpec((1, 1, 1, TF), lambda j, i, be, nu: (layer, be[i], 0, nf + j))],
            out_specs=pl.BlockSpec((MOE_ROWS, TF), lambda j, i, be, nu: (i, j)),
            scratch_shapes=[pltpu.VMEM((D, TF), BF16), pltpu.VMEM((D, TF), BF16)]),
        out_shape=jax.ShapeDtypeStruct((P, F), BF16),
        compiler_params=_cparams(2),
        name="moe_up",
    )(blk_expert, n_used, xs, w1, w1, b1r, b1r)
    return pl.pallas_call(
        _gmm2_kernel,
        grid_spec=pltpu.PrefetchScalarGridSpec(
            num_scalar_prefetch=2,
            grid=(D // TN_MOE, n_blk),
            in_specs=[pl.BlockSpec((MOE_ROWS, F), lambda j, i, be, nu: (i, 0)),
                      pl.BlockSpec((1, 1, F, TN_MOE), lambda j, i, be, nu: (layer, be[i], 0, j)),
                      pl.BlockSpec((1, 1, 1, TN_MOE), lambda j, i, be, nu: (layer, be[i], 0, j))],
            out_specs=pl.BlockSpec((MOE_ROWS, TN_MOE), lambda j, i, be, nu: (i, j)),
            scratch_shapes=[pltpu.VMEM((F, TN_MOE), BF16)]),
        out_shape=jax.ShapeDtypeStruct((P, D), F32),
        compiler_params=_cparams(2),
        name="moe_down",
    )(blk_expert, n_used, act, w2, b2.reshape(L, E, 1, D))


def _combine_kernel(pos_ref, nxt_ref, y_hbm, x_ref, gate_ref, g2_ref, nf_ref, o_ref, buf, sem, *, final):
    i = pl.program_id(0)
    n_steps = pl.num_programs(0)
    n_tok = x_ref.shape[0]
    n_rows = TOP_K * n_tok
    slot = i % 2
    by_choice = lambda a: (a % TOP_K) * n_tok + a // TOP_K

    @pl.when(i == 0)
    def _():
        _row_gather(y_hbm, pos_ref, buf, sem, 0, n_rows, by_choice, False)

    @pl.when(i + 1 < n_steps)
    def _():
        _row_gather(y_hbm, nxt_ref, buf, sem, 1 - slot, n_rows, by_choice, False)

    _row_gather(y_hbm, pos_ref, buf, sem, slot, n_rows, by_choice, True)
    gate = gate_ref[...]
    moe = gate[:, 0:1] * buf[slot, 0:n_tok, :]
    for k in range(1, TOP_K):
        moe = moe + gate[:, k:k + 1] * buf[slot, k * n_tok:(k + 1) * n_tok, :]
    x = x_ref[...] + g2_ref[0] * moe
    if final:
        ms = jnp.mean(x * x, axis=-1, keepdims=True)
        x = x * lax.rsqrt(ms + EPS) * nf_ref[...]
    o_ref[...] = x


def _combine(pos, y, x2d, gate, g2, norm_final, T, final):
    N, D = x2d.shape
    nt = T // TC
    n_steps = N // TC
    return pl.pallas_call(
        functools.partial(_combine_kernel, final=final),
        grid=(n_steps,),
        in_specs=[pl.BlockSpec((1, 1, TOP_K * TC), lambda i: (i, 0, 0), memory_space=pltpu.SMEM),
                  pl.BlockSpec((1, 1, TOP_K * TC), lambda i: (jnp.minimum(i + 1, n_steps - 1), 0, 0),
                               memory_space=pltpu.SMEM),
                  pl.BlockSpec(memory_space=pl.ANY),
                  pl.BlockSpec((TC, D), lambda i: (i, 0)),
                  pl.BlockSpec((TC, LANES), lambda i: (i, 0)),
                  pl.BlockSpec((1, 1, D), lambda i: (i // nt, 0, 0)),
                  pl.BlockSpec((1, D), lambda i: (0, 0))],
        out_specs=pl.BlockSpec((TC, D), lambda i: (i, 0)),
        out_shape=jax.ShapeDtypeStruct((N, D), F32),
        scratch_shapes=[pltpu.VMEM((2, TOP_K * TC, D), F32), pltpu.SemaphoreType.DMA((2,))],
        compiler_params=_cparams(1),
        name="moe_combine_final" if final else "moe_combine",
    )(pos, pos, y, x2d, gate, g2, norm_final)


def _routing_tables(top_idx):
    N = top_idx.shape[0]
    NK = N * TOP_K
    e_flat = top_idx.reshape(NK)
    onehot = (e_flat[:, None] == jnp.arange(N_EXPERTS, dtype=jnp.int32)[None, :]).astype(jnp.int32)
    csum = jnp.cumsum(onehot, axis=0)
    counts = csum[-1]
    rank = jnp.sum(csum * onehot, axis=1) - 1
    padded = (counts + MOE_ROWS - 1) // MOE_ROWS * MOE_ROWS
    pad_end = jnp.cumsum(padded)
    pad_start = pad_end - padded
    dest = jnp.sum(onehot * pad_start[None, :], axis=1) + rank
    P = NK + N_EXPERTS * MOE_ROWS
    n_blk = P // MOE_ROWS
    tok = jnp.arange(NK, dtype=jnp.int32) // TOP_K
    buf_tok = jnp.zeros((P,), jnp.int32).at[dest].set(tok)
    blk_start = jnp.arange(n_blk, dtype=jnp.int32) * MOE_ROWS
    blk_expert = jnp.minimum(jnp.sum(blk_start[:, None] >= pad_end[None, :], axis=-1), N_EXPERTS - 1)
    n_used = (pad_end[-1] // MOE_ROWS).reshape(1)
    return (dest.astype(jnp.int32), buf_tok.reshape(n_blk, 1, MOE_ROWS), blk_expert.astype(jnp.int32),
            n_used.astype(jnp.int32))


def _rope_tables(T):
    inv = 1.0 / (ROPE_THETA ** (jnp.arange(0, HEAD_DIM, 2, dtype=F32) / HEAD_DIM))
    ang = jnp.arange(T, dtype=F32)[:, None] * inv[None, :]
    return jnp.cos(ang), jnp.sin(ang)


def kernel(x, c, w_mod, b_mod, norm_attn, norm_ffn, w_in, cmp_pe, cmp_w1, cmp_w2, mix_norm, w_out,
           w_router, b_router, w_exp1, b_exp1, w_exp2, b_exp2, norm_final):
    B, T, D = x.shape
    N = B * T
    G = B_KV_HEADS
    assert D == D_MODEL and T % TM_PROJ == 0 and T % TM_ROUTE == 0
    assert all(T % n == 0 for n in (TQ_A, TK_A, TQ_B, TK_B, TQ_C, TK_C)) and TK_B % TQ_B == 0 and TQ_C % TK_C == 0

    cos, sin = _rope_tables(T)
    cos128 = jnp.tile(cos, (1, LANES // (HEAD_DIM // 2)))
    sin128 = jnp.tile(jnp.concatenate([-sin, sin], axis=1), (1, LANES // HEAD_DIM))
    n_cmp = (T - CMP_BLOCK) // CMP_STRIDE + 1
    n_chunk = T // CMP_STRIDE
    assert n_chunk == n_cmp + 1
    cmp_end = np.arange(n_chunk) * CMP_STRIDE + CMP_BLOCK - 1
    cmp_end = np.minimum(cmp_end, T - 1)
    cosc = jnp.stack([jnp.tile(cos[cmp_end], (1, 4)), jnp.ones((n_chunk, LANES), F32)])
    sinc = jnp.stack([jnp.tile(sin[cmp_end], (1, 4)), jnp.zeros((n_chunk, LANES), F32)])
    tbl = jnp.asarray(_dilated_bias_table())
    ov = jnp.asarray(_overlap_table_t(T))

    mod = _modulation(c, w_mod, b_mod).reshape(DEPTH, B, 6, 1, D)
    w_in_re = _permute_cols(w_in).astype(BF16)
    w_out_bf = w_out.astype(BF16)
    wr_pad = jnp.zeros((DEPTH, D, LANES), F32).at[:, :, :N_EXPERTS].set(w_router)
    br_pad = jnp.zeros((DEPTH, 1, LANES), F32).at[:, 0, :N_EXPERTS].set(b_router)
    half = HEAD_DIM // 2
    w2d = jnp.concatenate([cmp_w2, cmp_w2], axis=-1)
    w2rot = jnp.concatenate([-cmp_w2[..., half:], cmp_w2[..., :half]], axis=-1)
    w2r = jnp.concatenate([w2rot, w2rot], axis=-1)
    pe = cmp_pe.reshape(DEPTH, 2, 2, CMP_STRIDE * HEAD_DIM)

    xf = x.reshape(N, D)
    for i in range(DEPTH):
        sh1, sc1, g1, sh2, sc2, g2 = [mod[i, :, k] for k in range(6)]
        proj = _in_projection(xf, sc1, sh1, norm_attn[i][None, :], w_in_re[i], cos128, sin128, T)
        proj3 = proj.reshape(B, T, N_COLS)
        o_a = _attention_a(proj3, tbl)
        kcvc = proj3[:, :, BLK_KCVC * LANES:(BLK_KCVC + 3) * LANES]
        chunks = kcvc.reshape(B, T, 2, G, HEAD_DIM).transpose(0, 2, 3, 1, 4).reshape(
            B, 2, G, n_chunk, CMP_STRIDE * HEAD_DIM)
        cmp_kv = _compress(chunks, pe[i], cmp_w1[i], w2d[i], w2r[i], cosc, sinc)
        o_b = _attention_b(proj3, cmp_kv, ov)
        o_c = _attention_c(proj3)
        xf = _out_projection(o_a.reshape(N, -1), o_b.reshape(N, -1), o_c.reshape(N, -1),
                             mix_norm[i][None, :], w_out_bf[i], xf, g1, T)
        h2, idx128, gate128 = _router(xf, sc2, sh2, norm_ffn[i][None, :], wr_pad[i], br_pad[i], T)
        dest, buf_tok, blk_expert, n_used = _routing_tables(idx128[:, :TOP_K])
        xs = _gather_rows(n_used, buf_tok, h2)
        y = _experts(i, blk_expert, n_used, xs, w_exp1, b_exp1, w_exp2, b_exp2)
        xf = _combine(dest.reshape(N // TC, 1, TOP_K * TC), y, xf, gate128, g2,
                      norm_final[None, :], T, final=(i == DEPTH - 1))
    return xf.reshape(B, T, D)
```

```python
import functools

import numpy as np
import jax
import jax.numpy as jnp
from jax import lax
from jax.experimental import pallas as pl
from jax.experimental.pallas import tpu as pltpu

F32 = jnp.float32
BF16 = jnp.bfloat16
HIGHEST = lax.Precision.HIGHEST

D_MODEL = 2048
DEPTH = 2
HEAD_DIM = 64
A_HEADS = 12
A_PATTERNS = ((128, 1), (512, 4), (2048, 16))
B_HEADS = 12
B_KV_HEADS = 3
B_BRANCHES = 3
CMP_BLOCK = 32
CMP_STRIDE = 16
CMP_HIDDEN = 128
SLC_BLOCK = 64
SLC_TOPK = 16
WIN = 512
C_HEADS = 8
A_QKV_W = 3 * A_HEADS * HEAD_DIM
B_Q_W = B_HEADS * HEAD_DIM
B_KV_W = 2 * B_BRANCHES * B_KV_HEADS * HEAD_DIM
B_GATE_W = B_BRANCHES * B_HEADS
C_QKV_W = 3 * C_HEADS * HEAD_DIM
N_EXPERTS = 32
TOP_K = 4
D_FF = 2048
SWIGLU_LIMIT = 7.0
SWIGLU_ALPHA = 1.702
ROPE_THETA = 10000.0
EPS = 1e-6
NEG_BIG = -1e30
TINY = 1e-30
FORCE = 1e4
LOWEST = -3.0e38

LANES = 128
VMEM_LIMIT = 56 * 1024 * 1024

TQ_A, TK_A = 512, 512
TQ_B, TK_B = 256, 512
TQ_C, TK_C = 512, 256
TM_PROJ = 512
TN_IN = 768
TN_OUT = 512
TM_ROUTE = 256
MOE_ROWS = 256
TF = 1024
TN_MOE = 2048
TC = 128

BLK_AQ, BLK_AK, BLK_BQ, BLK_KS, BLK_KW = 0, 6, 12, 18, 21
N_ROPE_BLKS = 24
BLK_AV, BLK_KCVC, BLK_VS, BLK_VW = 24, 30, 33, 36
BLK_CQ, BLK_CK, BLK_CV = 39, 43, 47
BLK_GATE = 51
N_BLKS = 54
N_COLS = N_BLKS * LANES
assert N_ROPE_BLKS * LANES % TN_IN == 0 and N_COLS % TN_IN == 0


def _cparams(n_axes):
    return pltpu.CompilerParams(dimension_semantics=("arbitrary",) * n_axes,
                                vmem_limit_bytes=VMEM_LIMIT)


def _layout():
    a0 = 0
    bq0 = A_QKV_W
    bkv0 = bq0 + B_Q_W
    bg0 = bkv0 + B_KV_W
    c0 = bg0 + B_GATE_W
    scale = HEAD_DIM ** -0.5
    runs, colscale = [], []

    def add(start, stop, s, blk):
        assert len(colscale) == blk * LANES
        runs.append((start, stop))
        colscale.extend([s] * (stop - start))

    def add_dup(arr, blk):
        for g in range(B_KV_HEADS):
            s0 = bkv0 + arr * gw + g * HEAD_DIM
            add(s0, s0 + HEAD_DIM, 1.0, blk + g)
            runs.append((s0, s0 + HEAD_DIM))
            colscale.extend([1.0] * HEAD_DIM)

    hw = A_HEADS * HEAD_DIM
    gw = B_KV_HEADS * HEAD_DIM
    cw = C_HEADS * HEAD_DIM
    add(a0, a0 + hw, scale, BLK_AQ)
    add(a0 + hw, a0 + 2 * hw, 1.0, BLK_AK)
    add(bq0, bq0 + B_Q_W, scale, BLK_BQ)
    add_dup(2, BLK_KS)
    add_dup(4, BLK_KW)
    add(a0 + 2 * hw, a0 + 3 * hw, 1.0, BLK_AV)
    add(bkv0, bkv0 + 2 * gw, 1.0, BLK_KCVC)
    add_dup(3, BLK_VS)
    add_dup(5, BLK_VW)
    add(c0, c0 + cw, scale, BLK_CQ)
    add(c0 + cw, c0 + 3 * cw, 1.0, BLK_CK)
    per_g = B_GATE_W // B_KV_HEADS
    for g in range(B_KV_HEADS):
        add(bg0 + g * per_g, bg0 + (g + 1) * per_g, 1.0, BLK_GATE + g)
        runs.append((None, LANES - per_g))
        colscale.extend([1.0] * (LANES - per_g))
    assert len(colscale) == N_COLS
    return runs, np.asarray(colscale, np.float32)[None, :]


_RUNS, _COL_SCALE = _layout()


def _permute_cols(w):
    parts = []
    for start, stop in _RUNS:
        if start is None:
            parts.append(jnp.zeros(w.shape[:-1] + (stop,), w.dtype))
        else:
            parts.append(w[..., start:stop])
    return jnp.concatenate(parts, axis=-1)


def _dilated_bias_table():
    assert TQ_A == TK_A
    n = -(-A_PATTERNS[-1][0] // TK_A) + 1
    d = (np.arange(n)[:, None, None] * TK_A + np.arange(TQ_A)[None, :, None] - np.arange(TK_A)[None, None, :])
    m = np.zeros(d.shape, np.float64)
    for window, dil in A_PATTERNS:
        m += ((d >= 0) & (d <= window) & (d % dil == 0))
    return np.where(m > 0, np.log(np.maximum(m, 1.0)), NEG_BIG).astype(np.float32)


def _overlap_table_t(T):
    n_cmp = (T - CMP_BLOCK) // CMP_STRIDE + 1
    n_slc = T // SLC_BLOCK
    cs = np.arange(n_cmp) * CMP_STRIDE
    ss = np.arange(n_slc) * SLC_BLOCK
    ov = (cs[None, :] < ss[:, None] + SLC_BLOCK) & (cs[None, :] + CMP_BLOCK > ss[:, None])
    out = np.zeros((n_slc, n_cmp + 1), np.float32)
    out[:n_slc, :n_cmp] = ov
    return out


def _mod_kernel(c_ref, w_ref, b_ref, o_ref):
    c = c_ref[...]
    ca = (c * jax.nn.sigmoid(c)).astype(BF16)
    y = jnp.dot(ca, w_ref[0].astype(BF16), preferred_element_type=F32)
    o_ref[0] = y + b_ref[0]


def _modulation(c, w_mod, b_mod):
    B, D = c.shape
    L, _, W = w_mod.shape
    tn = 1024
    rows = 8
    cp = jnp.zeros((rows, D), F32).at[:B].set(c)
    out = pl.pallas_call(
        _mod_kernel,
        grid=(L, W // tn),
        in_specs=[pl.BlockSpec((rows, D), lambda l, j: (0, 0)),
                  pl.BlockSpec((1, D, tn), lambda l, j: (l, 0, j)),
                  pl.BlockSpec((1, 1, tn), lambda l, j: (l, 0, j))],
        out_specs=pl.BlockSpec((1, rows, tn), lambda l, j: (l, 0, j)),
        out_shape=jax.ShapeDtypeStruct((L, rows, W), F32),
        compiler_params=_cparams(2),
        name="adaln_mod",
    )(cp, w_mod, b_mod.reshape(L, 1, W))
    return out[:, :B]


def _norm_mod(x, gain, sc, sh):
    ms = jnp.mean(x * x, axis=-1, keepdims=True)
    y = x * lax.rsqrt(ms + EPS) * gain
    return y * (1.0 + sc) + sh


def _rope128(y, cos, sin_signed):
    lane = lax.broadcasted_iota(jnp.int32, y.shape, 1)
    first_half = (lane % HEAD_DIM) < (HEAD_DIM // 2)
    rot = jnp.where(first_half, pltpu.roll(y, LANES - HEAD_DIM // 2, 1), pltpu.roll(y, HEAD_DIM // 2, 1))
    return y * cos + rot * sin_signed


def _inproj_kernel(x_ref, sc_ref, sh_ref, g_ref, w_ref, cs_ref, cos_ref, sin_ref, o_ref, h_scr):
    @pl.when(pl.program_id(1) == 0)
    def _():
        h_scr[...] = _norm_mod(x_ref[...], g_ref[...], sc_ref[0], sh_ref[0]).astype(BF16)

    y = jnp.dot(h_scr[...], w_ref[...], preferred_element_type=F32)
    n_rope_tiles = N_ROPE_BLKS * LANES // TN_IN

    @pl.when(pl.program_id(1) < n_rope_tiles)
    def _():
        cos = cos_ref[...]
        sin = sin_ref[...]
        for c in range(TN_IN // LANES):
            sl = slice(c * LANES, (c + 1) * LANES)
            o_ref[:, sl] = _rope128(y[:, sl], cos, sin) * cs_ref[:, sl]

    @pl.when(pl.program_id(1) >= n_rope_tiles)
    def _():
        o_ref[...] = y * cs_ref[...]


def _in_projection(x2d, sc, sh, gain, w_re, cos128, sin128, T):
    N, D = x2d.shape
    nt = T // TM_PROJ
    return pl.pallas_call(
        _inproj_kernel,
        grid=(N // TM_PROJ, N_COLS // TN_IN),
        in_specs=[pl.BlockSpec((TM_PROJ, D), lambda i, j: (i, 0)),
                  pl.BlockSpec((1, 1, D), lambda i, j: (i // nt, 0, 0)),
                  pl.BlockSpec((1, 1, D), lambda i, j: (i // nt, 0, 0)),
                  pl.BlockSpec((1, D), lambda i, j: (0, 0)),
                  pl.BlockSpec((D, TN_IN), lambda i, j: (0, j)),
                  pl.BlockSpec((1, TN_IN), lambda i, j: (0, j)),
                  pl.BlockSpec((TM_PROJ, LANES), lambda i, j: (i % nt, 0)),
                  pl.BlockSpec((TM_PROJ, LANES), lambda i, j: (i % nt, 0))],
        out_specs=pl.BlockSpec((TM_PROJ, TN_IN), lambda i, j: (i, j)),
        out_shape=jax.ShapeDtypeStruct((N, N_COLS), F32),
        scratch_shapes=[pltpu.VMEM((TM_PROJ, D), BF16)],
        compiler_params=_cparams(2),
        name="in_proj",
    )(x2d, sc, sh, gain, w_re, jnp.asarray(_COL_SCALE), cos128, sin128)


def _split_heads_q(q):
    lane = lax.broadcasted_iota(jnp.int32, q.shape, 1)
    lo = lane < HEAD_DIM
    return jnp.concatenate([jnp.where(lo, q, 0.0), jnp.where(lo, 0.0, q)], axis=0)


def _scores(q2, k):
    return lax.dot_general(q2, k, (((1,), (1,)), ((), ())), preferred_element_type=F32)


def _flash_step(s, m, acc, v_ones):
    m_new = jnp.maximum(m, jnp.max(s, axis=-1, keepdims=True))
    p = jnp.exp(s - m_new).astype(BF16)
    acc = jnp.exp(m - m_new) * acc + jnp.dot(p, v_ones, preferred_element_type=F32)
    return m_new, acc


def _flash_init(rows):
    return jnp.full((rows, 1), NEG_BIG, F32), jnp.zeros((rows, LANES), F32)


def _flash_finish(acc):
    return acc / jnp.maximum(pltpu.roll(acc, HEAD_DIM, 1), TINY)


def _high_half(x):
    bits = lax.bitcast_convert_type(x, jnp.int32)
    return lax.bitcast_convert_type(bits & jnp.int32(-65536), F32)


def _split_bf16(x):
    hi = x.astype(BF16)
    return hi, (x - hi.astype(F32)).astype(BF16)


def _tile(ref, kt, n):
    return ref[0, pl.ds(pl.multiple_of(kt * n, n), n), :]


def _two_head_norm(out):
    lane = lax.broadcasted_iota(jnp.int32, out.shape, 1)
    lo = lane < HEAD_DIM
    sq = out * out
    ms0 = jnp.sum(jnp.where(lo, sq, 0.0), axis=-1, keepdims=True) * (1.0 / HEAD_DIM)
    ms1 = jnp.sum(jnp.where(lo, 0.0, sq), axis=-1, keepdims=True) * (1.0 / HEAD_DIM)
    return out * jnp.where(lo, lax.rsqrt(ms0 + EPS), lax.rsqrt(ms1 + EPS))


def _attn_a_kernel(q_ref, k_ref, v_ref, tbl_ref, o_ref):
    qi = pl.program_id(2)
    q = q_ref[0]
    lane = lax.broadcasted_iota(jnp.int32, q.shape, 1)
    lo = lane < HEAD_DIM
    q0 = jnp.where(lo, q, 0.0).astype(BF16)
    q1 = jnp.where(lo, 0.0, q).astype(BF16)
    n_d = tbl_ref.shape[0]

    def body(i, carry):
        m0, acc0, m1, acc1 = carry
        k = _tile(k_ref, qi - i, TK_A).astype(BF16)
        v = _tile(v_ref, qi - i, TK_A)
        lo_k = lax.broadcasted_iota(jnp.int32, v.shape, 1) < HEAD_DIM
        bias = tbl_ref[i]
        m0, acc0 = _flash_step(_scores(q0, k) + bias, m0, acc0, jnp.where(lo_k, v, 1.0).astype(BF16))
        m1, acc1 = _flash_step(_scores(q1, k) + bias, m1, acc1, jnp.where(lo_k, 1.0, v).astype(BF16))
        return m0, acc0, m1, acc1

    init = _flash_init(TQ_A) + _flash_init(TQ_A)
    _, acc0, _, acc1 = lax.fori_loop(0, jnp.minimum(qi, n_d - 1) + 1, body, init)
    o_ref[0] = _two_head_norm(jnp.where(lo, _flash_finish(acc0), _flash_finish(acc1)))


def _attention_a(proj, tbl):
    B, T, _ = proj.shape
    n_pairs = A_HEADS // 2
    return pl.pallas_call(
        _attn_a_kernel,
        grid=(B, n_pairs, T // TQ_A),
        in_specs=[pl.BlockSpec((1, TQ_A, LANES), lambda b, p, i: (b, i, BLK_AQ + p)),
                  pl.BlockSpec((1, T, LANES), lambda b, p, i: (b, 0, BLK_AK + p)),
                  pl.BlockSpec((1, T, LANES), lambda b, p, i: (b, 0, BLK_AV + p)),
                  pl.BlockSpec(tbl.shape, lambda b, p, i: (0, 0, 0))],
        out_specs=pl.BlockSpec((1, TQ_A, LANES), lambda b, p, i: (b, i, p)),
        out_shape=jax.ShapeDtypeStruct((B, T, n_pairs * LANES), F32),
        compiler_params=_cparams(3),
        name="attn_dilated",
    )(proj, proj, proj, tbl)


def _attn_c_kernel(q_ref, k_ref, v_ref, o_ref):
    qi = pl.program_id(2)
    q = q_ref[0]
    lo = lax.broadcasted_iota(jnp.int32, q.shape, 1) < HEAD_DIM
    qs = (jnp.where(lo, q, 0.0).astype(BF16), jnp.where(lo, 0.0, q).astype(BF16))
    diff = (lax.broadcasted_iota(jnp.int32, (TQ_C, TK_C), 1) - lax.broadcasted_iota(jnp.int32, (TQ_C, TK_C), 0))
    uj = lax.broadcasted_iota(jnp.int32, (TK_C, TK_C), 0)
    us = lax.broadcasted_iota(jnp.int32, (TK_C, TK_C), 1)
    upper = (uj >= us).astype(BF16)
    tiles_per_q = TQ_C // TK_C

    def tile(kt, carry, masked):
        k = _tile(k_ref, kt, TK_C).astype(BF16)
        v = _tile(v_ref, kt, TK_C).astype(BF16)
        if masked:
            before = diff < qi * TQ_C - kt * TK_C
        out = []
        for qh, (run, acc) in zip(qs, carry):
            z = _scores(qh, k)
            sp = jnp.maximum(z, 0.0) + jnp.log(1.0 + jnp.exp(-jnp.abs(z)))
            if masked:
                sp = jnp.where(before, sp, 0.0)
            hi = _high_half(sp)
            suffix = (jnp.dot(hi.astype(BF16), upper, preferred_element_type=F32)
                      + jnp.dot((sp - hi).astype(BF16), upper, preferred_element_type=F32))
            a = jnp.exp(z - suffix - run)
            if masked:
                a = jnp.where(before, a, 0.0)
            out.append((run + suffix[:, 0:1], acc + jnp.dot(a.astype(BF16), v, preferred_element_type=F32)))
        return tuple(out)

    carry = ((jnp.zeros((TQ_C, 1), F32), jnp.zeros((TQ_C, LANES), F32)),) * 2
    last = (qi + 1) * tiles_per_q - 1
    for j in range(tiles_per_q):
        carry = tile(last - j, carry, True)
    carry = lax.fori_loop(tiles_per_q, last + 1, lambda i, c: tile(last - i, c, False), carry)
    o_ref[0] = _two_head_norm(jnp.where(lo, carry[0][1], carry[1][1]))


def _attention_c(proj):
    B, T, _ = proj.shape
    n_pairs = C_HEADS // 2
    return pl.pallas_call(
        _attn_c_kernel,
        grid=(B, n_pairs, T // TQ_C),
        in_specs=[pl.BlockSpec((1, TQ_C, LANES), lambda b, p, i: (b, i, BLK_CQ + p)),
                  pl.BlockSpec((1, T, LANES), lambda b, p, i: (b, 0, BLK_CK + p)),
                  pl.BlockSpec((1, T, LANES), lambda b, p, i: (b, 0, BLK_CV + p))],
        out_specs=pl.BlockSpec((1, TQ_C, LANES), lambda b, p, i: (b, i, p)),
        out_shape=jax.ShapeDtypeStruct((B, T, n_pairs * LANES), F32),
        compiler_params=_cparams(3),
        name="attn_stickbreak",
    )(proj, proj, proj)


def _compress_kernel(x_ref, pe_ref, w1_ref, w2_ref, w2r_ref, cos_ref, sin_ref, o_ref):
    x = x_ref[0, 0, 0]
    half = x.shape[1]
    ha = jnp.dot(x + pe_ref[0, 0:1, :], w1_ref[0, :half, :], precision=HIGHEST, preferred_element_type=F32)
    hb = jnp.dot(x + pe_ref[0, 1:2, :], w1_ref[0, half:, :], precision=HIGHEST, preferred_element_type=F32)
    n = x.shape[0]
    h = ha + pltpu.roll(hb, n - 1, 0)
    g = jax.nn.gelu(h)
    y = jnp.dot(g, w2_ref[0], precision=HIGHEST, preferred_element_type=F32)
    yr = jnp.dot(g, w2r_ref[0], precision=HIGHEST, preferred_element_type=F32)
    out = y * cos_ref[0] + yr * sin_ref[0]
    rowi = lax.broadcasted_iota(jnp.int32, out.shape, 0)
    o_ref[0, 0, 0] = jnp.where(rowi < n - 1, out, 0.0)


def _compress(chunks, pe, w1, w2d, w2r, cosc, sinc):
    B, _, G, n, cw = chunks.shape
    return pl.pallas_call(
        _compress_kernel,
        grid=(B, 2, G),
        in_specs=[pl.BlockSpec((1, 1, 1, n, cw), lambda b, w, g: (b, w, g, 0, 0)),
                  pl.BlockSpec((1, 2, cw), lambda b, w, g: (w, 0, 0)),
                  pl.BlockSpec((1, 2 * cw, CMP_HIDDEN), lambda b, w, g: (w, 0, 0)),
                  pl.BlockSpec((1, CMP_HIDDEN, LANES), lambda b, w, g: (w, 0, 0)),
                  pl.BlockSpec((1, CMP_HIDDEN, LANES), lambda b, w, g: (w, 0, 0)),
                  pl.BlockSpec((1, n, LANES), lambda b, w, g: (w, 0, 0)),
                  pl.BlockSpec((1, n, LANES), lambda b, w, g: (w, 0, 0))],
        out_specs=pl.BlockSpec((1, 1, 1, n, LANES), lambda b, w, g: (b, w, g, 0, 0)),
        out_shape=jax.ShapeDtypeStruct((B, 2, G, n, LANES), F32),
        compiler_params=_cparams(3),
        name="nsa_compress",
    )(chunks, pe, w1, w2d, w2r, cosc, sinc)


def _nsa_kernel(q_ref, cmp_k_ref, cmp_v_ref, ks_ref, vs_ref, kw_ref, vw_ref, gate_ref, ovt_ref, o_ref):
    qi = pl.program_id(2)
    R = B_HEADS // B_KV_HEADS
    TQ, TK = TQ_B, TK_B
    rows = R * TQ
    q0 = qi * TQ
    q = q_ref[0]
    qf = jnp.concatenate([_split_heads_q(q[:, :LANES]), _split_heads_q(q[:, LANES:])], axis=0)
    q4 = qf.astype(BF16)

    kc = cmp_k_ref[0, 0, 0]
    vc = cmp_v_ref[0, 0, 0].astype(BF16)
    n_c = kc.shape[0]
    last_end = (n_c - 1) * CMP_STRIDE + CMP_BLOCK - 1
    t_col = q0 + lax.broadcasted_iota(jnp.int32, (TQ, 1), 0)
    cmp_end = lax.broadcasted_iota(jnp.int32, (TQ, n_c), 1) * CMP_STRIDE + (CMP_BLOCK - 1)
    valid = (cmp_end <= t_col) & (cmp_end < last_end)
    t_lane = q0 + lax.broadcasted_iota(jnp.int32, (1, TQ), 1)
    nt_dims = (((1,), (1,)), ((), ()))
    p_sum = jnp.zeros((TQ, n_c), F32)
    o_cmp = []
    kc_hi, kc_lo = _split_bf16(kc)
    for r in range(R):
        q_hi, q_lo = _split_bf16(qf[r * TQ:(r + 1) * TQ])
        s = _scores(q_hi, kc_hi) + (_scores(q_hi, kc_lo) + _scores(q_lo, kc_hi))
        s = jnp.where(valid, s, NEG_BIG)
        p = jnp.exp(s - jnp.max(s, axis=-1, keepdims=True)) * valid.astype(F32)
        p = p / jnp.maximum(jnp.sum(p, axis=-1, keepdims=True), TINY)
        o_cmp.append(jnp.dot(p.astype(BF16), vc, preferred_element_type=F32))
        p_sum = p_sum + p
    imp = lax.dot_general(ovt_ref[...], p_sum, nt_dims, precision=HIGHEST, preferred_element_type=F32)
    n_slc = imp.shape[0]
    blk = lax.broadcasted_iota(jnp.int32, (n_slc, TQ), 0)
    blk_f = blk.astype(F32)
    tb = t_lane // SLC_BLOCK
    forced = (blk == 0) | (blk == tb) | (blk == tb - 1)
    imp = jnp.where(forced, imp + FORCE, imp)
    imp = jnp.where(blk > tb, -FORCE, imp)
    sel = jnp.zeros((n_slc, TQ), F32)
    for _ in range(min(SLC_TOPK, n_slc)):
        mx = jnp.max(imp, axis=0, keepdims=True)
        first = jnp.min(jnp.where(imp == mx, blk_f, float(n_slc)), axis=0, keepdims=True)
        pick = blk_f == first
        sel = jnp.where(pick, 1.0, sel)
        imp = jnp.where(pick, LOWEST, imp)
    sel_bias = jnp.concatenate([(sel - 1.0) * (-NEG_BIG), jnp.zeros((LANES - n_slc, TQ), F32)], axis=0)
    sel_bias = sel_bias.T.astype(BF16)

    diff = lax.broadcasted_iota(jnp.int32, (TQ, TK), 1) - lax.broadcasted_iota(jnp.int32, (TQ, TK), 0)
    e_row = lax.broadcasted_iota(jnp.int32, (LANES, TK), 0)
    e_col = lax.broadcasted_iota(jnp.int32, (LANES, TK), 1) // SLC_BLOCK
    lo_k = lax.broadcasted_iota(jnp.int32, (TK, LANES), 1) < HEAD_DIM
    last = q0 // TK

    def step(k_ref, v_ref, kt, bias, carry):
        k = _tile(k_ref, kt, TK).astype(BF16)
        v_ones = jnp.where(lo_k, _tile(v_ref, kt, TK), 1.0).astype(BF16)
        s = (_scores(q4, k).reshape(R, TQ, TK) + bias[None]).reshape(rows, TK)
        return _flash_step(s, *carry, v_ones)

    def slc_tile(kt, carry, causal):
        expand = (e_row == kt * (TK // SLC_BLOCK) + e_col).astype(BF16)
        bias = jnp.dot(sel_bias, expand, preferred_element_type=F32)
        if causal:
            bias = bias + jnp.where(diff <= q0 - kt * TK, 0.0, NEG_BIG)
        return step(ks_ref, vs_ref, kt, bias, carry)

    carry = lax.fori_loop(0, last, lambda kt, c: slc_tile(kt, c, False), _flash_init(rows))
    o_slc = _flash_finish(slc_tile(last, carry, True)[1])

    def win_tile(i, carry):
        kt = last - i
        off = q0 - kt * TK
        bias = jnp.where((diff <= off) & (diff > off - WIN), 0.0, NEG_BIG)
        return step(kw_ref, vw_ref, kt, bias, carry)

    first_win = jnp.maximum(q0 - (WIN - 1), 0) // TK
    o_win = _flash_finish(lax.fori_loop(0, last - first_win + 1, win_tile, _flash_init(rows))[1])

    gates = jax.nn.sigmoid(gate_ref[0])
    lo = lax.broadcasted_iota(jnp.int32, (TQ, LANES), 1) < HEAD_DIM
    outs = []
    for r in range(R):
        rs = slice(r * TQ, (r + 1) * TQ)
        o = (gates[:, 3 * r:3 * r + 1] * o_cmp[r] + gates[:, 3 * r + 1:3 * r + 2] * o_slc[rs]
             + gates[:, 3 * r + 2:3 * r + 3] * o_win[rs])
        ms = jnp.sum(jnp.where(lo, o * o, 0.0), axis=-1, keepdims=True) * (1.0 / HEAD_DIM)
        outs.append(o * lax.rsqrt(ms + EPS))
    pairs = [jnp.where(lo, outs[2 * j], pltpu.roll(outs[2 * j + 1], HEAD_DIM, 1)) for j in range(R // 2)]
    o_ref[0] = jnp.concatenate(pairs, axis=1)


def _attention_b(proj, cmp_kv, ov):
    B, T, _ = proj.shape
    G = B_KV_HEADS
    n_c = cmp_kv.shape[3]
    qw = 2 * LANES
    TQ = TQ_B
    return pl.pallas_call(
        _nsa_kernel,
        grid=(B, G, T // TQ),
        in_specs=[pl.BlockSpec((1, TQ, qw), lambda b, g, i: (b, i, BLK_BQ // 2 + g)),
                  pl.BlockSpec((1, 1, 1, n_c, LANES), lambda b, g, i: (b, 0, g, 0, 0)),
                  pl.BlockSpec((1, 1, 1, n_c, LANES), lambda b, g, i: (b, 1, g, 0, 0)),
                  pl.BlockSpec((1, T, LANES), lambda b, g, i: (b, 0, BLK_KS + g)),
                  pl.BlockSpec((1, T, LANES), lambda b, g, i: (b, 0, BLK_VS + g)),
                  pl.BlockSpec((1, T, LANES), lambda b, g, i: (b, 0, BLK_KW + g)),
                  pl.BlockSpec((1, T, LANES), lambda b, g, i: (b, 0, BLK_VW + g)),
                  pl.BlockSpec((1, TQ, LANES), lambda b, g, i: (b, i, BLK_GATE + g)),
                  pl.BlockSpec(ov.shape, lambda b, g, i: (0, 0))],
        out_specs=pl.BlockSpec((1, TQ, qw), lambda b, g, i: (b, i, g)),
        out_shape=jax.ShapeDtypeStruct((B, T, B_HEADS * HEAD_DIM), F32),
        compiler_params=_cparams(3),
        name="attn_nsa",
    )(proj, cmp_kv, cmp_kv, proj, proj, proj, proj, proj, ov)


def _outproj_kernel(oa_ref, ob_ref, oc_ref, mn_ref, w_ref, x_ref, g_ref, o_ref, h_scr):
    wa = oa_ref.shape[1]
    wb = ob_ref.shape[1]

    @pl.when(pl.program_id(1) == 0)
    def _():
        h_scr[:, :wa] = (oa_ref[...] * mn_ref[:, :wa]).astype(BF16)
        h_scr[:, wa:wa + wb] = (ob_ref[...] * mn_ref[:, wa:wa + wb]).astype(BF16)
        h_scr[:, wa + wb:] = (oc_ref[...] * mn_ref[:, wa + wb:]).astype(BF16)

    y = jnp.dot(h_scr[...], w_ref[...], preferred_element_type=F32)
    o_ref[...] = x_ref[...] + g_ref[0] * y


def _out_projection(oa, ob, oc, mix_norm, w_out_bf, x2d, g1, T):
    N, D = x2d.shape
    nt = T // TM_PROJ
    wa, wb, wc = oa.shape[1], ob.shape[1], oc.shape[1]
    return pl.pallas_call(
        _outproj_kernel,
        grid=(N // TM_PROJ, D // TN_OUT),
        in_specs=[pl.BlockSpec((TM_PROJ, wa), lambda i, j: (i, 0)),
                  pl.BlockSpec((TM_PROJ, wb), lambda i, j: (i, 0)),
                  pl.BlockSpec((TM_PROJ, wc), lambda i, j: (i, 0)),
                  pl.BlockSpec((1, D), lambda i, j: (0, 0)),
                  pl.BlockSpec((D, TN_OUT), lambda i, j: (0, j)),
                  pl.BlockSpec((TM_PROJ, TN_OUT), lambda i, j: (i, j)),
                  pl.BlockSpec((1, 1, TN_OUT), lambda i, j: (i // nt, 0, j))],
        out_specs=pl.BlockSpec((TM_PROJ, TN_OUT), lambda i, j: (i, j)),
        out_shape=jax.ShapeDtypeStruct((N, D), F32),
        scratch_shapes=[pltpu.VMEM((TM_PROJ, D), BF16)],
        compiler_params=_cparams(2),
        name="out_proj",
    )(oa, ob, oc, mix_norm, w_out_bf, x2d, g1)


def _router_kernel(x_ref, sc_ref, sh_ref, g_ref, wr_ref, br_ref, h_ref, idx_ref, gate_ref):
    h = _norm_mod(x_ref[...], g_ref[...], sc_ref[0], sh_ref[0])
    h_ref[...] = h
    logits = jnp.dot(h, wr_ref[...], precision=HIGHEST, preferred_element_type=F32) + br_ref[...]
    lane = lax.broadcasted_iota(jnp.int32, logits.shape, 1)
    lane_f = lane.astype(F32)
    cur = jnp.where(lane < N_EXPERTS, logits, LOWEST)
    idx_out = jnp.zeros(logits.shape, F32)
    e_out = jnp.zeros(logits.shape, F32)
    top0 = None
    denom = None
    for k in range(TOP_K):
        mx = jnp.max(cur, axis=-1, keepdims=True)
        first = jnp.min(jnp.where(cur == mx, lane_f, float(LANES)), axis=-1, keepdims=True)
        cur = jnp.where(lane_f == first, LOWEST, cur)
        if k == 0:
            top0 = mx
        e = jnp.exp(mx - top0)
        denom = e if k == 0 else denom + e
        idx_out = jnp.where(lane == k, first, idx_out)
        e_out = jnp.where(lane == k, e, e_out)
    idx_ref[...] = idx_out.astype(jnp.int32)
    gate_ref[...] = e_out / denom


def _router(x2d, sc, sh, gain, wr_pad, br_pad, T):
    N, D = x2d.shape
    nt = T // TM_ROUTE
    return pl.pallas_call(
        _router_kernel,
        grid=(N // TM_ROUTE,),
        in_specs=[pl.BlockSpec((TM_ROUTE, D), lambda i: (i, 0)),
                  pl.BlockSpec((1, 1, D), lambda i: (i // nt, 0, 0)),
                  pl.BlockSpec((1, 1, D), lambda i: (i // nt, 0, 0)),
                  pl.BlockSpec((1, D), lambda i: (0, 0)),
                  pl.BlockSpec((D, LANES), lambda i: (0, 0)),
                  pl.BlockSpec((1, LANES), lambda i: (0, 0))],
        out_specs=[pl.BlockSpec((TM_ROUTE, D), lambda i: (i, 0)),
                   pl.BlockSpec((TM_ROUTE, LANES), lambda i: (i, 0)),
                   pl.BlockSpec((TM_ROUTE, LANES), lambda i: (i, 0))],
        out_shape=[jax.ShapeDtypeStruct((N, D), F32),
                   jax.ShapeDtypeStruct((N, LANES), jnp.int32),
                   jax.ShapeDtypeStruct((N, LANES), F32)],
        compiler_params=_cparams(1),
        name="moe_router",
    )(x2d, sc, sh, gain, wr_pad, br_pad)


DMA_UNROLL = 8


def _row_gather(src_hbm, idx_ref, buf, sem, slot, n_rows, dst_row, wait):
    if wait:
        pltpu.make_async_copy(src_hbm.at[pl.ds(0, n_rows), :], buf.at[slot], sem.at[slot]).wait()
        return

    def body(a8, c):
        for u in range(DMA_UNROLL):
            a = a8 * DMA_UNROLL + u
            pltpu.make_async_copy(src_hbm.at[pl.ds(idx_ref[0, 0, a], 1), :],
                                  buf.at[slot, pl.ds(dst_row(a), 1), :], sem.at[slot]).start(priority=u % 2)
        return c

    lax.fori_loop(0, n_rows // DMA_UNROLL, body, 0)


def _gather_kernel(nu_ref, tok_ref, nxt_ref, h_hbm, o_ref, buf, sem):
    i = pl.program_id(0)
    n_used = nu_ref[0]
    n = buf.shape[1]
    slot = i % 2
    ident = lambda a: a

    @pl.when(i == 0)
    def _():
        _row_gather(h_hbm, tok_ref, buf, sem, 0, n, ident, False)

    @pl.when(i + 1 < n_used)
    def _():
        _row_gather(h_hbm, nxt_ref, buf, sem, 1 - slot, n, ident, False)

    @pl.when(i < n_used)
    def _():
        _row_gather(h_hbm, tok_ref, buf, sem, slot, n, ident, True)
        o_ref[...] = buf[slot].astype(BF16)

    @pl.when(i >= n_used)
    def _():
        o_ref[...] = jnp.zeros(o_ref.shape, o_ref.dtype)


def _gather_rows(n_used, buf_tok, h):
    N, D = h.shape
    n_blk = buf_tok.shape[0]
    return pl.pallas_call(
        _gather_kernel,
        grid_spec=pltpu.PrefetchScalarGridSpec(
            num_scalar_prefetch=1,
            grid=(n_blk,),
            in_specs=[pl.BlockSpec((1, 1, MOE_ROWS), lambda i, nu: (i, 0, 0), memory_space=pltpu.SMEM),
                      pl.BlockSpec((1, 1, MOE_ROWS), lambda i, nu: (jnp.minimum(i + 1, n_blk - 1), 0, 0),
                                   memory_space=pltpu.SMEM),
                      pl.BlockSpec(memory_space=pl.ANY)],
            out_specs=pl.BlockSpec((MOE_ROWS, D), lambda i, nu: (i, 0)),
            scratch_shapes=[pltpu.VMEM((2, MOE_ROWS, D), F32), pltpu.SemaphoreType.DMA((2,))]),
        out_shape=jax.ShapeDtypeStruct((n_blk * MOE_ROWS, D), BF16),
        compiler_params=_cparams(1),
        name="moe_gather",
    )(n_used, buf_tok, buf_tok, h)


def _expert_changed(be_ref, i):
    prev = be_ref[jnp.maximum(i - 1, 0)]
    return (i == 0) | (be_ref[i] != prev)


def _stage_expert_weights(be_ref, meta_ref, nxt_ref, w_hbm, stage, w_scr, sem, layer, windows, n_pass):
    j = pl.program_id(0)
    i = pl.program_id(1)
    e = be_ref[i]
    width = stage.shape[-1]

    def copies(ee, jj):
        return [pltpu.make_async_copy(w_hbm.at[layer, ee, :, pl.ds(pl.multiple_of(c0, LANES), width)],
                                      stage.at[p], sem.at[p])
                for p, c0 in enumerate(windows(jj))]

    @pl.when(_expert_changed(be_ref, i))
    def _():
        @pl.when((j == 0) & (i == 0))
        def _():
            for cp in copies(e, j):
                cp.start()

        for cp in copies(e, j):
            cp.wait()
        for p in range(stage.shape[0]):
            w_scr[p] = stage[p].astype(BF16)
        nxt = nxt_ref[e]
        nj = jnp.where(nxt >= 0, j, j + 1)
        ne = jnp.where(nxt >= 0, nxt, meta_ref[1])

        @pl.when(nj < n_pass)
        def _():
            for cp in copies(ne, nj):
                cp.start()


def _gmm1_kernel(be_ref, meta_ref, nxt_ref, x_ref, w_hbm, bg_ref, bl_ref, o_ref, stage, w_scr, sem, *,
                 layer, n_pass, d_ff):
    i = pl.program_id(1)

    @pl.when(i < meta_ref[0])
    def _():
        _stage_expert_weights(be_ref, meta_ref, nxt_ref, w_hbm, stage, w_scr, sem, layer,
                              lambda jj: (jj * TF, d_ff + jj * TF), n_pass)
        x = x_ref[...]
        glu = jnp.dot(x, w_scr[0], preferred_element_type=F32) + bg_ref[0, 0]
        lin = jnp.dot(x, w_scr[1], preferred_element_type=F32) + bl_ref[0, 0]
        glu = jnp.minimum(glu, SWIGLU_LIMIT)
        lin = jnp.clip(lin, -SWIGLU_LIMIT, SWIGLU_LIMIT)
        o_ref[...] = (glu * jax.nn.sigmoid(SWIGLU_ALPHA * glu) * (lin + 1.0)).astype(BF16)

    @pl.when(i >= meta_ref[0])
    def _():
        o_ref[...] = jnp.zeros(o_ref.shape, o_ref.dtype)


def _gmm2_kernel(be_ref, meta_ref, nxt_ref, a_ref, w_hbm, b_ref, o_ref, stage, w_scr, sem, *, layer, n_pass):
    i = pl.program_id(1)

    @pl.when(i < meta_ref[0])
    def _():
        _stage_expert_weights(be_ref, meta_ref, nxt_ref, w_hbm, stage, w_scr, sem, layer,
                              lambda jj: (jj * TN_MOE,), n_pass)
        o_ref[...] = jnp.dot(a_ref[...], w_scr[0], preferred_element_type=F32) + b_ref[0, 0]

    @pl.when(i >= meta_ref[0])
    def _():
        o_ref[...] = jnp.zeros(o_ref.shape, o_ref.dtype)


def _experts(layer, blk_expert, meta, nxt, xs, w1, b1, w2, b2):
    P, D = xs.shape
    L, E, _, F2 = w1.shape
    F = F2 // 2
    n_blk = P // MOE_ROWS
    nf = F // TF
    nd = D // TN_MOE
    b1r = b1.reshape(L, E, 1, F2)
    act = pl.pallas_call(
        functools.partial(_gmm1_kernel, layer=layer, n_pass=nf, d_ff=F),
        grid_spec=pltpu.PrefetchScalarGridSpec(
            num_scalar_prefetch=3,
            grid=(nf, n_blk),
            in_specs=[pl.BlockSpec((MOE_ROWS, D), lambda j, i, be, mt, nx: (i, 0)),
                      pl.BlockSpec(memory_space=pl.ANY),
                      pl.BlockSpec((1, 1, 1, TF), lambda j, i, be, mt, nx: (layer, be[i], 0, j)),
                      pl.BlockSpec((1, 1, 1, TF), lambda j, i, be, mt, nx: (layer, be[i], 0, nf + j))],
            out_specs=pl.BlockSpec((MOE_ROWS, TF), lambda j, i, be, mt, nx: (i, j)),
            scratch_shapes=[pltpu.VMEM((2, D, TF), F32), pltpu.VMEM((2, D, TF), BF16),
                            pltpu.SemaphoreType.DMA((2,))]),
        out_shape=jax.ShapeDtypeStruct((P, F), BF16),
        compiler_params=_cparams(2),
        name="moe_up",
    )(blk_expert, meta, nxt, xs, w1, b1r, b1r)
    return pl.pallas_call(
        functools.partial(_gmm2_kernel, layer=layer, n_pass=nd),
        grid_spec=pltpu.PrefetchScalarGridSpec(
            num_scalar_prefetch=3,
            grid=(nd, n_blk),
            in_specs=[pl.BlockSpec((MOE_ROWS, F), lambda j, i, be, mt, nx: (i, 0)),
                      pl.BlockSpec(memory_space=pl.ANY),
                      pl.BlockSpec((1, 1, 1, TN_MOE), lambda j, i, be, mt, nx: (layer, be[i], 0, j))],
            out_specs=pl.BlockSpec((MOE_ROWS, TN_MOE), lambda j, i, be, mt, nx: (i, j)),
            scratch_shapes=[pltpu.VMEM((1, F, TN_MOE), F32), pltpu.VMEM((1, F, TN_MOE), BF16),
                            pltpu.SemaphoreType.DMA((1,))]),
        out_shape=jax.ShapeDtypeStruct((P, D), F32),
        compiler_params=_cparams(2),
        name="moe_down",
    )(blk_expert, meta, nxt, act, w2, b2.reshape(L, E, 1, D))


def _combine_kernel(pos_ref, nxt_ref, y_hbm, x_ref, gate_ref, g2_ref, nf_ref, o_ref, buf, sem, *, final):
    i = pl.program_id(0)
    n_steps = pl.num_programs(0)
    n_tok = x_ref.shape[0]
    n_rows = TOP_K * n_tok
    slot = i % 2
    by_choice = lambda a: (a % TOP_K) * n_tok + a // TOP_K

    @pl.when(i == 0)
    def _():
        _row_gather(y_hbm, pos_ref, buf, sem, 0, n_rows, by_choice, False)

    @pl.when(i + 1 < n_steps)
    def _():
        _row_gather(y_hbm, nxt_ref, buf, sem, 1 - slot, n_rows, by_choice, False)

    _row_gather(y_hbm, pos_ref, buf, sem, slot, n_rows, by_choice, True)
    gate = gate_ref[...]
    moe = gate[:, 0:1] * buf[slot, 0:n_tok, :]
    for k in range(1, TOP_K):
        moe = moe + gate[:, k:k + 1] * buf[slot, k * n_tok:(k + 1) * n_tok, :]
    x = x_ref[...] + g2_ref[0] * moe
    if final:
        ms = jnp.mean(x * x, axis=-1, keepdims=True)
        x = x * lax.rsqrt(ms + EPS) * nf_ref[...]
    o_ref[...] = x


def _combine(pos, y, x2d, gate, g2, norm_final, T, final):
    N, D = x2d.shape
    nt = T // TC
    n_steps = N // TC
    return pl.pallas_call(
        functools.partial(_combine_kernel, final=final),
        grid=(n_steps,),
        in_specs=[pl.BlockSpec((1, 1, TOP_K * TC), lambda i: (i, 0, 0), memory_space=pltpu.SMEM),
                  pl.BlockSpec((1, 1, TOP_K * TC), lambda i: (jnp.minimum(i + 1, n_steps - 1), 0, 0),
                               memory_space=pltpu.SMEM),
                  pl.BlockSpec(memory_space=pl.ANY),
                  pl.BlockSpec((TC, D), lambda i: (i, 0)),
                  pl.BlockSpec((TC, LANES), lambda i: (i, 0)),
                  pl.BlockSpec((1, 1, D), lambda i: (i // nt, 0, 0)),
                  pl.BlockSpec((1, D), lambda i: (0, 0))],
        out_specs=pl.BlockSpec((TC, D), lambda i: (i, 0)),
        out_shape=jax.ShapeDtypeStruct((N, D), F32),
        scratch_shapes=[pltpu.VMEM((2, TOP_K * TC, D), F32), pltpu.SemaphoreType.DMA((2,))],
        compiler_params=_cparams(1),
        name="moe_combine_final" if final else "moe_combine",
    )(pos, pos, y, x2d, gate, g2, norm_final)


def _routing_tables(top_idx):
    N = top_idx.shape[0]
    NK = N * TOP_K
    e_flat = top_idx.reshape(NK)
    onehot = (e_flat[:, None] == jnp.arange(N_EXPERTS, dtype=jnp.int32)[None, :]).astype(jnp.int32)
    csum = jnp.cumsum(onehot, axis=0)
    counts = csum[-1]
    rank = jnp.sum(csum * onehot, axis=1) - 1
    padded = (counts + MOE_ROWS - 1) // MOE_ROWS * MOE_ROWS
    pad_end = jnp.cumsum(padded)
    pad_start = pad_end - padded
    dest = jnp.sum(onehot * pad_start[None, :], axis=1) + rank
    P = NK + N_EXPERTS * MOE_ROWS
    n_blk = P // MOE_ROWS
    tok = jnp.arange(NK, dtype=jnp.int32) // TOP_K
    buf_tok = jnp.zeros((P,), jnp.int32).at[dest].set(tok)
    blk_start = jnp.arange(n_blk, dtype=jnp.int32) * MOE_ROWS
    blk_expert = jnp.minimum(jnp.sum(blk_start[:, None] >= pad_end[None, :], axis=-1), N_EXPERTS - 1)
    n_used = pad_end[-1] // MOE_ROWS
    ids = jnp.arange(N_EXPERTS, dtype=jnp.int32)
    at_or_after = lax.cummin(jnp.where(counts > 0, ids, N_EXPERTS), reverse=True)
    nxt = jnp.concatenate([at_or_after[1:], jnp.full((1,), N_EXPERTS, jnp.int32)])
    nxt = jnp.where(nxt < N_EXPERTS, nxt, -1)
    meta = jnp.stack([n_used, at_or_after[0]])
    return (dest.astype(jnp.int32), buf_tok.reshape(n_blk, 1, MOE_ROWS), blk_expert.astype(jnp.int32),
            meta.astype(jnp.int32), nxt.astype(jnp.int32))


def _rope_tables(T):
    inv = 1.0 / (ROPE_THETA ** (jnp.arange(0, HEAD_DIM, 2, dtype=F32) / HEAD_DIM))
    ang = jnp.arange(T, dtype=F32)[:, None] * inv[None, :]
    return jnp.cos(ang), jnp.sin(ang)


def kernel(x, c, w_mod, b_mod, norm_attn, norm_ffn, w_in, cmp_pe, cmp_w1, cmp_w2, mix_norm, w_out,
           w_router, b_router, w_exp1, b_exp1, w_exp2, b_exp2, norm_final):
    B, T, D = x.shape
    N = B * T
    G = B_KV_HEADS
    assert D == D_MODEL and T % TM_PROJ == 0 and T % TM_ROUTE == 0
    assert all(T % n == 0 for n in (TQ_A, TK_A, TQ_B, TK_B, TQ_C, TK_C)) and TK_B % TQ_B == 0 and TQ_C % TK_C == 0

    cos, sin = _rope_tables(T)
    cos128 = jnp.tile(cos, (1, LANES // (HEAD_DIM // 2)))
    sin128 = jnp.tile(jnp.concatenate([-sin, sin], axis=1), (1, LANES // HEAD_DIM))
    n_cmp = (T - CMP_BLOCK) // CMP_STRIDE + 1
    n_chunk = T // CMP_STRIDE
    assert n_chunk == n_cmp + 1
    cmp_end = np.arange(n_chunk) * CMP_STRIDE + CMP_BLOCK - 1
    cmp_end = np.minimum(cmp_end, T - 1)
    cosc = jnp.stack([jnp.tile(cos[cmp_end], (1, 4)), jnp.ones((n_chunk, LANES), F32)])
    sinc = jnp.stack([jnp.tile(sin[cmp_end], (1, 4)), jnp.zeros((n_chunk, LANES), F32)])
    tbl = jnp.asarray(_dilated_bias_table())
    ov = jnp.asarray(_overlap_table_t(T))

    mod = _modulation(c, w_mod, b_mod).reshape(DEPTH, B, 6, 1, D)
    w_in_re = _permute_cols(w_in).astype(BF16)
    w_out_bf = w_out.astype(BF16)
    wr_pad = jnp.zeros((DEPTH, D, LANES), F32).at[:, :, :N_EXPERTS].set(w_router)
    br_pad = jnp.zeros((DEPTH, 1, LANES), F32).at[:, 0, :N_EXPERTS].set(b_router)
    half = HEAD_DIM // 2
    w2d = jnp.concatenate([cmp_w2, cmp_w2], axis=-1)
    w2rot = jnp.concatenate([-cmp_w2[..., half:], cmp_w2[..., :half]], axis=-1)
    w2r = jnp.concatenate([w2rot, w2rot], axis=-1)
    pe = cmp_pe.reshape(DEPTH, 2, 2, CMP_STRIDE * HEAD_DIM)

    xf = x.reshape(N, D)
    for i in range(DEPTH):
        sh1, sc1, g1, sh2, sc2, g2 = [mod[i, :, k] for k in range(6)]
        proj = _in_projection(xf, sc1, sh1, norm_attn[i][None, :], w_in_re[i], cos128, sin128, T)
        proj3 = proj.reshape(B, T, N_COLS)
        o_a = _attention_a(proj3, tbl)
        kcvc = proj3[:, :, BLK_KCVC * LANES:(BLK_KCVC + 3) * LANES]
        chunks = kcvc.reshape(B, T, 2, G, HEAD_DIM).transpose(0, 2, 3, 1, 4).reshape(
            B, 2, G, n_chunk, CMP_STRIDE * HEAD_DIM)
        cmp_kv = _compress(chunks, pe[i], cmp_w1[i], w2d[i], w2r[i], cosc, sinc)
        o_b = _attention_b(proj3, cmp_kv, ov)
        o_c = _attention_c(proj3)
        xf = _out_projection(o_a.reshape(N, -1), o_b.reshape(N, -1), o_c.reshape(N, -1),
                             mix_norm[i][None, :], w_out_bf[i], xf, g1, T)
        h2, idx128, gate128 = _router(xf, sc2, sh2, norm_ffn[i][None, :], wr_pad[i], br_pad[i], T)
        dest, buf_tok, blk_expert, meta, nxt = _routing_tables(idx128[:, :TOP_K])
        xs = _gather_rows(meta, buf_tok, h2)
        y = _experts(i, blk_expert, meta, nxt, xs, w_exp1, b_exp1, w_exp2, b_exp2)
        xf = _combine(dest.reshape(N // TC, 1, TOP_K * TC), y, xf, gate128, g2,
                      norm_final[None, :], T, final=(i == DEPTH - 1))
    return xf.reshape(B, T, D)
```

```python
import functools

import numpy as np
import jax
import jax.numpy as jnp
from jax import lax
from jax.experimental import pallas as pl
from jax.experimental.pallas import tpu as pltpu

F32 = jnp.float32
BF16 = jnp.bfloat16
HIGHEST = lax.Precision.HIGHEST

D_MODEL = 2048
DEPTH = 2
HEAD_DIM = 64
A_HEADS = 12
A_PATTERNS = ((128, 1), (512, 4), (2048, 16))
B_HEADS = 12
B_KV_HEADS = 3
B_BRANCHES = 3
CMP_BLOCK = 32
CMP_STRIDE = 16
CMP_HIDDEN = 128
SLC_BLOCK = 64
SLC_TOPK = 16
WIN = 512
C_HEADS = 8
A_QKV_W = 3 * A_HEADS * HEAD_DIM
B_Q_W = B_HEADS * HEAD_DIM
B_KV_W = 2 * B_BRANCHES * B_KV_HEADS * HEAD_DIM
B_GATE_W = B_BRANCHES * B_HEADS
C_QKV_W = 3 * C_HEADS * HEAD_DIM
N_EXPERTS = 32
TOP_K = 4
D_FF = 2048
SWIGLU_LIMIT = 7.0
SWIGLU_ALPHA = 1.702
ROPE_THETA = 10000.0
EPS = 1e-6
NEG_BIG = -1e30
TINY = 1e-30
FORCE = 1e4
LOWEST = -3.0e38
RUN_DEAD = 110.0

LANES = 128
VMEM_LIMIT = 56 * 1024 * 1024

TQ_A, TK_A = 512, 512
TQ_B, TK_B = 256, 512
TQ_C, TK_C = 512, 256
TM_PROJ = 512
TN_IN = 768
TN_OUT = 512
TM_ROUTE = 256
MOE_ROWS = 256
TF = 1024
TN_MOE = 2048
TC = 128

BLK_AQ, BLK_AK, BLK_BQ, BLK_KS, BLK_KW = 0, 6, 12, 18, 21
N_ROPE_BLKS = 24
BLK_AV, BLK_KCVC, BLK_VS, BLK_VW = 24, 30, 33, 36
BLK_CQ, BLK_CK, BLK_CV = 39, 43, 47
BLK_GATE = 51
N_BLKS = 54
N_COLS = N_BLKS * LANES
assert N_ROPE_BLKS * LANES % TN_IN == 0 and N_COLS % TN_IN == 0


def _cparams(n_axes):
    return pltpu.CompilerParams(dimension_semantics=("arbitrary",) * n_axes,
                                vmem_limit_bytes=VMEM_LIMIT)


def _layout():
    a0 = 0
    bq0 = A_QKV_W
    bkv0 = bq0 + B_Q_W
    bg0 = bkv0 + B_KV_W
    c0 = bg0 + B_GATE_W
    scale = HEAD_DIM ** -0.5
    runs, colscale = [], []

    def add(start, stop, s, blk):
        assert len(colscale) == blk * LANES
        runs.append((start, stop))
        colscale.extend([s] * (stop - start))

    def add_dup(arr, blk):
        for g in range(B_KV_HEADS):
            s0 = bkv0 + arr * gw + g * HEAD_DIM
            add(s0, s0 + HEAD_DIM, 1.0, blk + g)
            runs.append((s0, s0 + HEAD_DIM))
            colscale.extend([1.0] * HEAD_DIM)

    hw = A_HEADS * HEAD_DIM
    gw = B_KV_HEADS * HEAD_DIM
    cw = C_HEADS * HEAD_DIM
    add(a0, a0 + hw, scale, BLK_AQ)
    add(a0 + hw, a0 + 2 * hw, 1.0, BLK_AK)
    add(bq0, bq0 + B_Q_W, scale, BLK_BQ)
    add_dup(2, BLK_KS)
    add_dup(4, BLK_KW)
    add(a0 + 2 * hw, a0 + 3 * hw, 1.0, BLK_AV)
    add(bkv0, bkv0 + 2 * gw, 1.0, BLK_KCVC)
    add_dup(3, BLK_VS)
    add_dup(5, BLK_VW)
    add(c0, c0 + cw, scale, BLK_CQ)
    add(c0 + cw, c0 + 3 * cw, 1.0, BLK_CK)
    per_g = B_GATE_W // B_KV_HEADS
    for g in range(B_KV_HEADS):
        add(bg0 + g * per_g, bg0 + (g + 1) * per_g, 1.0, BLK_GATE + g)
        runs.append((None, LANES - per_g))
        colscale.extend([1.0] * (LANES - per_g))
    assert len(colscale) == N_COLS
    return runs, np.asarray(colscale, np.float32)[None, :]


_RUNS, _COL_SCALE = _layout()


def _permute_cols(w):
    parts = []
    for start, stop in _RUNS:
        if start is None:
            parts.append(jnp.zeros(w.shape[:-1] + (stop,), w.dtype))
        else:
            parts.append(w[..., start:stop])
    return jnp.concatenate(parts, axis=-1)


def _dilated_bias_table():
    assert TQ_A == TK_A
    n = -(-A_PATTERNS[-1][0] // TK_A) + 1
    d = (np.arange(n)[:, None, None] * TK_A + np.arange(TQ_A)[None, :, None] - np.arange(TK_A)[None, None, :])
    m = np.zeros(d.shape, np.float64)
    for window, dil in A_PATTERNS:
        m += ((d >= 0) & (d <= window) & (d % dil == 0))
    return np.where(m > 0, np.log(np.maximum(m, 1.0)), NEG_BIG).astype(np.float32)


def _overlap_table_t(T):
    n_cmp = (T - CMP_BLOCK) // CMP_STRIDE + 1
    n_slc = T // SLC_BLOCK
    cs = np.arange(n_cmp) * CMP_STRIDE
    ss = np.arange(n_slc) * SLC_BLOCK
    ov = (cs[None, :] < ss[:, None] + SLC_BLOCK) & (cs[None, :] + CMP_BLOCK > ss[:, None])
    out = np.zeros((n_slc, n_cmp + 1), np.float32)
    out[:n_slc, :n_cmp] = ov
    return out


def _mod_kernel(c_ref, w_ref, b_ref, o_ref):
    c = c_ref[...]
    ca = (c * jax.nn.sigmoid(c)).astype(BF16)
    y = jnp.dot(ca, w_ref[0].astype(BF16), preferred_element_type=F32)
    o_ref[0] = y + b_ref[0]


def _modulation(c, w_mod, b_mod):
    B, D = c.shape
    L, _, W = w_mod.shape
    tn = 1024
    rows = 8
    cp = jnp.zeros((rows, D), F32).at[:B].set(c)
    out = pl.pallas_call(
        _mod_kernel,
        grid=(L, W // tn),
        in_specs=[pl.BlockSpec((rows, D), lambda l, j: (0, 0)),
                  pl.BlockSpec((1, D, tn), lambda l, j: (l, 0, j)),
                  pl.BlockSpec((1, 1, tn), lambda l, j: (l, 0, j))],
        out_specs=pl.BlockSpec((1, rows, tn), lambda l, j: (l, 0, j)),
        out_shape=jax.ShapeDtypeStruct((L, rows, W), F32),
        compiler_params=_cparams(2),
        name="adaln_mod",
    )(cp, w_mod, b_mod.reshape(L, 1, W))
    return out[:, :B]


def _norm_mod(x, gain, sc, sh):
    ms = jnp.mean(x * x, axis=-1, keepdims=True)
    y = x * lax.rsqrt(ms + EPS) * gain
    return y * (1.0 + sc) + sh


def _rope128(y, cos, sin_signed):
    lane = lax.broadcasted_iota(jnp.int32, y.shape, 1)
    first_half = (lane % HEAD_DIM) < (HEAD_DIM // 2)
    rot = jnp.where(first_half, pltpu.roll(y, LANES - HEAD_DIM // 2, 1), pltpu.roll(y, HEAD_DIM // 2, 1))
    return y * cos + rot * sin_signed


def _inproj_kernel(x_ref, sc_ref, sh_ref, g_ref, w_ref, cs_ref, cos_ref, sin_ref, o_ref, h_scr):
    @pl.when(pl.program_id(1) == 0)
    def _():
        h_scr[...] = _norm_mod(x_ref[...], g_ref[...], sc_ref[0], sh_ref[0]).astype(BF16)

    y = jnp.dot(h_scr[...], w_ref[pl.program_id(1)], preferred_element_type=F32)
    n_rope_tiles = N_ROPE_BLKS * LANES // TN_IN

    @pl.when(pl.program_id(1) < n_rope_tiles)
    def _():
        cos = cos_ref[...]
        sin = sin_ref[...]
        for c in range(TN_IN // LANES):
            sl = slice(c * LANES, (c + 1) * LANES)
            o_ref[:, sl] = _rope128(y[:, sl], cos, sin) * cs_ref[:, sl]

    @pl.when(pl.program_id(1) >= n_rope_tiles)
    def _():
        o_ref[...] = y * cs_ref[...]


def _in_projection(x2d, sc, sh, gain, w_re, cos128, sin128, T):
    N, D = x2d.shape
    nt = T // TM_PROJ
    return pl.pallas_call(
        _inproj_kernel,
        grid=(N // TM_PROJ, N_COLS // TN_IN),
        in_specs=[pl.BlockSpec((TM_PROJ, D), lambda i, j: (i, 0)),
                  pl.BlockSpec((1, 1, D), lambda i, j: (i // nt, 0, 0)),
                  pl.BlockSpec((1, 1, D), lambda i, j: (i // nt, 0, 0)),
                  pl.BlockSpec((1, D), lambda i, j: (0, 0)),
                  pl.BlockSpec(w_re.shape, lambda i, j: (0, 0, 0), pipeline_mode=pl.Buffered(1)),
                  pl.BlockSpec((1, TN_IN), lambda i, j: (0, j)),
                  pl.BlockSpec((TM_PROJ, LANES), lambda i, j: (i % nt, 0)),
                  pl.BlockSpec((TM_PROJ, LANES), lambda i, j: (i % nt, 0))],
        out_specs=pl.BlockSpec((TM_PROJ, TN_IN), lambda i, j: (i, j)),
        out_shape=jax.ShapeDtypeStruct((N, N_COLS), F32),
        scratch_shapes=[pltpu.VMEM((TM_PROJ, D), BF16)],
        compiler_params=_cparams(2),
        name="in_proj",
    )(x2d, sc, sh, gain, w_re, jnp.asarray(_COL_SCALE), cos128, sin128)


def _split_heads_q(q):
    lane = lax.broadcasted_iota(jnp.int32, q.shape, 1)
    lo = lane < HEAD_DIM
    return jnp.concatenate([jnp.where(lo, q, 0.0), jnp.where(lo, 0.0, q)], axis=0)


def _scores(q2, k):
    return lax.dot_general(q2, k, (((1,), (1,)), ((), ())), preferred_element_type=F32)


def _flash_step(s, m, acc, v_ones):
    m_new = jnp.maximum(m, jnp.max(s, axis=-1, keepdims=True))
    p = jnp.exp(s - m_new).astype(BF16)
    acc = jnp.exp(m - m_new) * acc + jnp.dot(p, v_ones, preferred_element_type=F32)
    return m_new, acc


def _flash_init(rows):
    return jnp.full((rows, 1), NEG_BIG, F32), jnp.zeros((rows, LANES), F32)


def _flash_finish(acc):
    return acc / jnp.maximum(pltpu.roll(acc, HEAD_DIM, 1), TINY)


def _high_half(x):
    bits = lax.bitcast_convert_type(x, jnp.int32)
    return lax.bitcast_convert_type(bits & jnp.int32(-65536), F32)


def _split_bf16(x):
    hi = x.astype(BF16)
    return hi, (x - hi.astype(F32)).astype(BF16)


def _tile(ref, kt, n):
    return ref[0, pl.ds(pl.multiple_of(kt * n, n), n), :]


def _two_head_norm(out):
    lane = lax.broadcasted_iota(jnp.int32, out.shape, 1)
    lo = lane < HEAD_DIM
    sq = out * out
    ms0 = jnp.sum(jnp.where(lo, sq, 0.0), axis=-1, keepdims=True) * (1.0 / HEAD_DIM)
    ms1 = jnp.sum(jnp.where(lo, 0.0, sq), axis=-1, keepdims=True) * (1.0 / HEAD_DIM)
    return out * jnp.where(lo, lax.rsqrt(ms0 + EPS), lax.rsqrt(ms1 + EPS))


def _attn_a_kernel(q_ref, k_ref, v_ref, tbl_ref, o_ref):
    qi = pl.program_id(2)
    q = q_ref[0]
    lane = lax.broadcasted_iota(jnp.int32, q.shape, 1)
    lo = lane < HEAD_DIM
    q0 = jnp.where(lo, q, 0.0).astype(BF16)
    q1 = jnp.where(lo, 0.0, q).astype(BF16)
    n_d = tbl_ref.shape[0]

    def body(i, carry):
        m0, acc0, m1, acc1 = carry
        k = _tile(k_ref, qi - i, TK_A).astype(BF16)
        v = _tile(v_ref, qi - i, TK_A)
        lo_k = lax.broadcasted_iota(jnp.int32, v.shape, 1) < HEAD_DIM
        bias = tbl_ref[i]
        m0, acc0 = _flash_step(_scores(q0, k) + bias, m0, acc0, jnp.where(lo_k, v, 1.0).astype(BF16))
        m1, acc1 = _flash_step(_scores(q1, k) + bias, m1, acc1, jnp.where(lo_k, 1.0, v).astype(BF16))
        return m0, acc0, m1, acc1

    init = _flash_init(TQ_A) + _flash_init(TQ_A)
    _, acc0, _, acc1 = lax.fori_loop(0, jnp.minimum(qi, n_d - 1) + 1, body, init)
    o_ref[0] = _two_head_norm(jnp.where(lo, _flash_finish(acc0), _flash_finish(acc1)))


def _attention_a(proj, tbl):
    B, T, _ = proj.shape
    n_pairs = A_HEADS // 2
    return pl.pallas_call(
        _attn_a_kernel,
        grid=(B, n_pairs, T // TQ_A),
        in_specs=[pl.BlockSpec((1, TQ_A, LANES), lambda b, p, i: (b, i, BLK_AQ + p)),
                  pl.BlockSpec((1, T, LANES), lambda b, p, i: (b, 0, BLK_AK + p)),
                  pl.BlockSpec((1, T, LANES), lambda b, p, i: (b, 0, BLK_AV + p)),
                  pl.BlockSpec(tbl.shape, lambda b, p, i: (0, 0, 0))],
        out_specs=pl.BlockSpec((1, TQ_A, LANES), lambda b, p, i: (b, i, p)),
        out_shape=jax.ShapeDtypeStruct((B, T, n_pairs * LANES), F32),
        compiler_params=_cparams(3),
        name="attn_dilated",
    )(proj, proj, proj, tbl)


def _attn_c_kernel(q_ref, k_ref, v_ref, o_ref):
    qi = pl.program_id(2)
    q = q_ref[0]
    lo = lax.broadcasted_iota(jnp.int32, q.shape, 1) < HEAD_DIM
    qs = (jnp.where(lo, q, 0.0).astype(BF16), jnp.where(lo, 0.0, q).astype(BF16))
    diff = (lax.broadcasted_iota(jnp.int32, (TQ_C, TK_C), 1) - lax.broadcasted_iota(jnp.int32, (TQ_C, TK_C), 0))
    uj = lax.broadcasted_iota(jnp.int32, (TK_C, TK_C), 0)
    us = lax.broadcasted_iota(jnp.int32, (TK_C, TK_C), 1)
    upper = (uj >= us).astype(BF16)
    tiles_per_q = TQ_C // TK_C

    def tile(kt, carry, masked):
        k = _tile(k_ref, kt, TK_C).astype(BF16)
        v = _tile(v_ref, kt, TK_C).astype(BF16)
        if masked:
            before = diff < qi * TQ_C - kt * TK_C
        out = []
        for qh, (run, acc) in zip(qs, carry):
            z = _scores(qh, k)
            sp = jnp.maximum(z, 0.0) + jnp.log(1.0 + jnp.exp(-jnp.abs(z)))
            if masked:
                sp = jnp.where(before, sp, 0.0)
            hi = _high_half(sp)
            suffix = (jnp.dot(hi.astype(BF16), upper, preferred_element_type=F32)
                      + jnp.dot((sp - hi).astype(BF16), upper, preferred_element_type=F32))
            a = jnp.exp(z - suffix - run)
            if masked:
                a = jnp.where(before, a, 0.0)
            out.append((run + suffix[:, 0:1], acc + jnp.dot(a.astype(BF16), v, preferred_element_type=F32)))
        return tuple(out)

    carry = ((jnp.zeros((TQ_C, 1), F32), jnp.zeros((TQ_C, LANES), F32)),) * 2
    last = (qi + 1) * tiles_per_q - 1
    for j in range(tiles_per_q):
        carry = tile(last - j, carry, True)
    def rest_is_zero(c):
        return (jnp.minimum(jnp.min(c[0][0]), jnp.min(c[1][0])) >= RUN_DEAD).astype(jnp.int32)

    def more(state):
        i, dead, _ = state
        return (i <= last) & (dead == 0)

    def step(state):
        i, _, c = state
        c = tile(last - i, c, False)
        return i + 1, rest_is_zero(c), c

    _, _, carry = lax.while_loop(more, step, (jnp.int32(tiles_per_q), rest_is_zero(carry), carry))
    o_ref[0] = _two_head_norm(jnp.where(lo, carry[0][1], carry[1][1]))


def _attention_c(proj):
    B, T, _ = proj.shape
    n_pairs = C_HEADS // 2
    return pl.pallas_call(
        _attn_c_kernel,
        grid=(B, n_pairs, T // TQ_C),
        in_specs=[pl.BlockSpec((1, TQ_C, LANES), lambda b, p, i: (b, i, BLK_CQ + p)),
                  pl.BlockSpec((1, T, LANES), lambda b, p, i: (b, 0, BLK_CK + p)),
                  pl.BlockSpec((1, T, LANES), lambda b, p, i: (b, 0, BLK_CV + p))],
        out_specs=pl.BlockSpec((1, TQ_C, LANES), lambda b, p, i: (b, i, p)),
        out_shape=jax.ShapeDtypeStruct((B, T, n_pairs * LANES), F32),
        compiler_params=_cparams(3),
        name="attn_stickbreak",
    )(proj, proj, proj)


def _compress_kernel(x_ref, pe_ref, w1_ref, w2_ref, w2r_ref, cos_ref, sin_ref, o_ref):
    x = x_ref[0, 0, 0]
    half = x.shape[1]
    ha = jnp.dot(x + pe_ref[0, 0:1, :], w1_ref[0, :half, :], precision=HIGHEST, preferred_element_type=F32)
    hb = jnp.dot(x + pe_ref[0, 1:2, :], w1_ref[0, half:, :], precision=HIGHEST, preferred_element_type=F32)
    n = x.shape[0]
    h = ha + pltpu.roll(hb, n - 1, 0)
    g = jax.nn.gelu(h)
    y = jnp.dot(g, w2_ref[0], precision=HIGHEST, preferred_element_type=F32)
    yr = jnp.dot(g, w2r_ref[0], precision=HIGHEST, preferred_element_type=F32)
    out = y * cos_ref[0] + yr * sin_ref[0]
    rowi = lax.broadcasted_iota(jnp.int32, out.shape, 0)
    o_ref[0, 0, 0] = jnp.where(rowi < n - 1, out, 0.0)


def _compress(chunks, pe, w1, w2d, w2r, cosc, sinc):
    B, _, G, n, cw = chunks.shape
    return pl.pallas_call(
        _compress_kernel,
        grid=(B, 2, G),
        in_specs=[pl.BlockSpec((1, 1, 1, n, cw), lambda b, w, g: (b, w, g, 0, 0)),
                  pl.BlockSpec((1, 2, cw), lambda b, w, g: (w, 0, 0)),
                  pl.BlockSpec((1, 2 * cw, CMP_HIDDEN), lambda b, w, g: (w, 0, 0)),
                  pl.BlockSpec((1, CMP_HIDDEN, LANES), lambda b, w, g: (w, 0, 0)),
                  pl.BlockSpec((1, CMP_HIDDEN, LANES), lambda b, w, g: (w, 0, 0)),
                  pl.BlockSpec((1, n, LANES), lambda b, w, g: (w, 0, 0)),
                  pl.BlockSpec((1, n, LANES), lambda b, w, g: (w, 0, 0))],
        out_specs=pl.BlockSpec((1, 1, 1, n, LANES), lambda b, w, g: (b, w, g, 0, 0)),
        out_shape=jax.ShapeDtypeStruct((B, 2, G, n, LANES), F32),
        compiler_params=_cparams(3),
        name="nsa_compress",
    )(chunks, pe, w1, w2d, w2r, cosc, sinc)


def _nsa_kernel(q_ref, cmp_k_ref, cmp_v_ref, ks_ref, vs_ref, kw_ref, vw_ref, gate_ref, ovt_ref, o_ref):
    qi = pl.program_id(2)
    R = B_HEADS // B_KV_HEADS
    TQ, TK = TQ_B, TK_B
    rows = R * TQ
    q0 = qi * TQ
    q = q_ref[0]
    qf = jnp.concatenate([_split_heads_q(q[:, :LANES]), _split_heads_q(q[:, LANES:])], axis=0)
    q4 = qf.astype(BF16)

    kc = cmp_k_ref[0, 0, 0]
    vc = cmp_v_ref[0, 0, 0].astype(BF16)
    n_c = kc.shape[0]
    last_end = (n_c - 1) * CMP_STRIDE + CMP_BLOCK - 1
    t_col = q0 + lax.broadcasted_iota(jnp.int32, (TQ, 1), 0)
    cmp_end = lax.broadcasted_iota(jnp.int32, (TQ, n_c), 1) * CMP_STRIDE + (CMP_BLOCK - 1)
    valid = (cmp_end <= t_col) & (cmp_end < last_end)
    t_lane = q0 + lax.broadcasted_iota(jnp.int32, (1, TQ), 1)
    nt_dims = (((1,), (1,)), ((), ()))
    p_sum = jnp.zeros((TQ, n_c), F32)
    o_cmp = []
    kc_hi, kc_lo = _split_bf16(kc)
    for r in range(R):
        q_hi, q_lo = _split_bf16(qf[r * TQ:(r + 1) * TQ])
        s = _scores(q_hi, kc_hi) + (_scores(q_hi, kc_lo) + _scores(q_lo, kc_hi))
        s = jnp.where(valid, s, NEG_BIG)
        p = jnp.exp(s - jnp.max(s, axis=-1, keepdims=True)) * valid.astype(F32)
        p = p / jnp.maximum(jnp.sum(p, axis=-1, keepdims=True), TINY)
        o_cmp.append(jnp.dot(p.astype(BF16), vc, preferred_element_type=F32))
        p_sum = p_sum + p
    imp = lax.dot_general(ovt_ref[...], p_sum, nt_dims, precision=HIGHEST, preferred_element_type=F32)
    n_slc = imp.shape[0]
    blk = lax.broadcasted_iota(jnp.int32, (n_slc, TQ), 0)
    blk_f = blk.astype(F32)
    tb = t_lane // SLC_BLOCK
    forced = (blk == 0) | (blk == tb) | (blk == tb - 1)
    imp = jnp.where(forced, imp + FORCE, imp)
    imp = jnp.where(blk > tb, -FORCE, imp)
    sel = jnp.zeros((n_slc, TQ), F32)
    for _ in range(min(SLC_TOPK, n_slc)):
        mx = jnp.max(imp, axis=0, keepdims=True)
        first = jnp.min(jnp.where(imp == mx, blk_f, float(n_slc)), axis=0, keepdims=True)
        pick = blk_f == first
        sel = jnp.where(pick, 1.0, sel)
        imp = jnp.where(pick, LOWEST, imp)
    sel_bias = jnp.concatenate([(sel - 1.0) * (-NEG_BIG), jnp.zeros((LANES - n_slc, TQ), F32)], axis=0)
    sel_bias = sel_bias.T.astype(BF16)

    diff = lax.broadcasted_iota(jnp.int32, (TQ, TK), 1) - lax.broadcasted_iota(jnp.int32, (TQ, TK), 0)
    e_row = lax.broadcasted_iota(jnp.int32, (LANES, TK), 0)
    e_col = lax.broadcasted_iota(jnp.int32, (LANES, TK), 1) // SLC_BLOCK
    lo_k = lax.broadcasted_iota(jnp.int32, (TK, LANES), 1) < HEAD_DIM
    last = q0 // TK

    def step(k_ref, v_ref, kt, bias, carry):
        k = _tile(k_ref, kt, TK).astype(BF16)
        v_ones = jnp.where(lo_k, _tile(v_ref, kt, TK), 1.0).astype(BF16)
        s = (_scores(q4, k).reshape(R, TQ, TK) + bias[None]).reshape(rows, TK)
        return _flash_step(s, *carry, v_ones)

    def slc_tile(kt, carry, causal):
        expand = (e_row == kt * (TK // SLC_BLOCK) + e_col).astype(BF16)
        bias = jnp.dot(sel_bias, expand, preferred_element_type=F32)
        if causal:
            bias = bias + jnp.where(diff <= q0 - kt * TK, 0.0, NEG_BIG)
        return step(ks_ref, vs_ref, kt, bias, carry)

    carry = lax.fori_loop(0, last, lambda kt, c: slc_tile(kt, c, False), _flash_init(rows))
    o_slc = _flash_finish(slc_tile(last, carry, True)[1])

    def win_tile(i, carry):
        kt = last - i
        off = q0 - kt * TK
        bias = jnp.where((diff <= off) & (diff > off - WIN), 0.0, NEG_BIG)
        return step(kw_ref, vw_ref, kt, bias, carry)

    first_win = jnp.maximum(q0 - (WIN - 1), 0) // TK
    o_win = _flash_finish(lax.fori_loop(0, last - first_win + 1, win_tile, _flash_init(rows))[1])

    gates = jax.nn.sigmoid(gate_ref[0])
    lo = lax.broadcasted_iota(jnp.int32, (TQ, LANES), 1) < HEAD_DIM
    outs = []
    for r in range(R):
        rs = slice(r * TQ, (r + 1) * TQ)
        o = (gates[:, 3 * r:3 * r + 1] * o_cmp[r] + gates[:, 3 * r + 1:3 * r + 2] * o_slc[rs]
             + gates[:, 3 * r + 2:3 * r + 3] * o_win[rs])
        ms = jnp.sum(jnp.where(lo, o * o, 0.0), axis=-1, keepdims=True) * (1.0 / HEAD_DIM)
        outs.append(o * lax.rsqrt(ms + EPS))
    pairs = [jnp.where(lo, outs[2 * j], pltpu.roll(outs[2 * j + 1], HEAD_DIM, 1)) for j in range(R // 2)]
    o_ref[0] = jnp.concatenate(pairs, axis=1)


def _attention_b(proj, cmp_kv, ov):
    B, T, _ = proj.shape
    G = B_KV_HEADS
    n_c = cmp_kv.shape[3]
    qw = 2 * LANES
    TQ = TQ_B
    return pl.pallas_call(
        _nsa_kernel,
        grid=(B, G, T // TQ),
        in_specs=[pl.BlockSpec((1, TQ, qw), lambda b, g, i: (b, i, BLK_BQ // 2 + g)),
                  pl.BlockSpec((1, 1, 1, n_c, LANES), lambda b, g, i: (b, 0, g, 0, 0)),
                  pl.BlockSpec((1, 1, 1, n_c, LANES), lambda b, g, i: (b, 1, g, 0, 0)),
                  pl.BlockSpec((1, T, LANES), lambda b, g, i: (b, 0, BLK_KS + g)),
                  pl.BlockSpec((1, T, LANES), lambda b, g, i: (b, 0, BLK_VS + g)),
                  pl.BlockSpec((1, T, LANES), lambda b, g, i: (b, 0, BLK_KW + g)),
                  pl.BlockSpec((1, T, LANES), lambda b, g, i: (b, 0, BLK_VW + g)),
                  pl.BlockSpec((1, TQ, LANES), lambda b, g, i: (b, i, BLK_GATE + g)),
                  pl.BlockSpec(ov.shape, lambda b, g, i: (0, 0))],
        out_specs=pl.BlockSpec((1, TQ, qw), lambda b, g, i: (b, i, g)),
        out_shape=jax.ShapeDtypeStruct((B, T, B_HEADS * HEAD_DIM), F32),
        compiler_params=_cparams(3),
        name="attn_nsa",
    )(proj, cmp_kv, cmp_kv, proj, proj, proj, proj, proj, ov)


def _outproj_kernel(oa_ref, ob_ref, oc_ref, mn_ref, w_ref, x_ref, g_ref, o_ref, h_scr):
    wa = oa_ref.shape[1]
    wb = ob_ref.shape[1]

    @pl.when(pl.program_id(1) == 0)
    def _():
        h_scr[:, :wa] = (oa_ref[...] * mn_ref[:, :wa]).astype(BF16)
        h_scr[:, wa:wa + wb] = (ob_ref[...] * mn_ref[:, wa:wa + wb]).astype(BF16)
        h_scr[:, wa + wb:] = (oc_ref[...] * mn_ref[:, wa + wb:]).astype(BF16)

    y = jnp.dot(h_scr[...], w_ref[...], preferred_element_type=F32)
    o_ref[...] = x_ref[...] + g_ref[0] * y


def _out_projection(oa, ob, oc, mix_norm, w_out_bf, x2d, g1, T):
    N, D = x2d.shape
    nt = T // TM_PROJ
    wa, wb, wc = oa.shape[1], ob.shape[1], oc.shape[1]
    return pl.pallas_call(
        _outproj_kernel,
        grid=(N // TM_PROJ, D // TN_OUT),
        in_specs=[pl.BlockSpec((TM_PROJ, wa), lambda i, j: (i, 0)),
                  pl.BlockSpec((TM_PROJ, wb), lambda i, j: (i, 0)),
                  pl.BlockSpec((TM_PROJ, wc), lambda i, j: (i, 0)),
                  pl.BlockSpec((1, D), lambda i, j: (0, 0)),
                  pl.BlockSpec((D, TN_OUT), lambda i, j: (0, j)),
                  pl.BlockSpec((TM_PROJ, TN_OUT), lambda i, j: (i, j)),
                  pl.BlockSpec((1, 1, TN_OUT), lambda i, j: (i // nt, 0, j))],
        out_specs=pl.BlockSpec((TM_PROJ, TN_OUT), lambda i, j: (i, j)),
        out_shape=jax.ShapeDtypeStruct((N, D), F32),
        scratch_shapes=[pltpu.VMEM((TM_PROJ, D), BF16)],
        compiler_params=_cparams(2),
        name="out_proj",
    )(oa, ob, oc, mix_norm, w_out_bf, x2d, g1)


def _router_kernel(x_ref, sc_ref, sh_ref, g_ref, wr_ref, br_ref, h_ref, idx_ref, gate_ref):
    h = _norm_mod(x_ref[...], g_ref[...], sc_ref[0], sh_ref[0])
    h_ref[...] = h
    logits = jnp.dot(h, wr_ref[...], precision=HIGHEST, preferred_element_type=F32) + br_ref[...]
    lane = lax.broadcasted_iota(jnp.int32, logits.shape, 1)
    lane_f = lane.astype(F32)
    cur = jnp.where(lane < N_EXPERTS, logits, LOWEST)
    idx_out = jnp.zeros(logits.shape, F32)
    e_out = jnp.zeros(logits.shape, F32)
    top0 = None
    denom = None
    for k in range(TOP_K):
        mx = jnp.max(cur, axis=-1, keepdims=True)
        first = jnp.min(jnp.where(cur == mx, lane_f, float(LANES)), axis=-1, keepdims=True)
        cur = jnp.where(lane_f == first, LOWEST, cur)
        if k == 0:
            top0 = mx
        e = jnp.exp(mx - top0)
        denom = e if k == 0 else denom + e
        idx_out = jnp.where(lane == k, first, idx_out)
        e_out = jnp.where(lane == k, e, e_out)
    idx_ref[...] = idx_out.astype(jnp.int32)
    gate_ref[...] = e_out / denom


def _router(x2d, sc, sh, gain, wr_pad, br_pad, T):
    N, D = x2d.shape
    nt = T // TM_ROUTE
    return pl.pallas_call(
        _router_kernel,
        grid=(N // TM_ROUTE,),
        in_specs=[pl.BlockSpec((TM_ROUTE, D), lambda i: (i, 0)),
                  pl.BlockSpec((1, 1, D), lambda i: (i // nt, 0, 0)),
                  pl.BlockSpec((1, 1, D), lambda i: (i // nt, 0, 0)),
                  pl.BlockSpec((1, D), lambda i: (0, 0)),
                  pl.BlockSpec((D, LANES), lambda i: (0, 0)),
                  pl.BlockSpec((1, LANES), lambda i: (0, 0))],
        out_specs=[pl.BlockSpec((TM_ROUTE, D), lambda i: (i, 0)),
                   pl.BlockSpec((TM_ROUTE, LANES), lambda i: (i, 0)),
                   pl.BlockSpec((TM_ROUTE, LANES), lambda i: (i, 0))],
        out_shape=[jax.ShapeDtypeStruct((N, D), F32),
                   jax.ShapeDtypeStruct((N, LANES), jnp.int32),
                   jax.ShapeDtypeStruct((N, LANES), F32)],
        compiler_params=_cparams(1),
        name="moe_router",
    )(x2d, sc, sh, gain, wr_pad, br_pad)


DMA_UNROLL = 8


def _row_gather(src_hbm, idx_ref, buf, sem, slot, n_rows, dst_row, wait):
    if wait:
        pltpu.make_async_copy(src_hbm.at[pl.ds(0, n_rows), :], buf.at[slot], sem.at[slot]).wait()
        return

    def body(a8, c):
        for u in range(DMA_UNROLL):
            a = a8 * DMA_UNROLL + u
            pltpu.make_async_copy(src_hbm.at[pl.ds(idx_ref[0, 0, a], 1), :],
                                  buf.at[slot, pl.ds(dst_row(a), 1), :], sem.at[slot]).start(priority=u % 2)
        return c

    lax.fori_loop(0, n_rows // DMA_UNROLL, body, 0)


def _gather_kernel(nu_ref, tok_ref, nxt_ref, h_hbm, o_ref, buf, sem):
    i = pl.program_id(0)
    n_used = nu_ref[0]
    n = buf.shape[1]
    slot = i % 2
    ident = lambda a: a

    @pl.when(i == 0)
    def _():
        _row_gather(h_hbm, tok_ref, buf, sem, 0, n, ident, False)

    @pl.when(i + 1 < n_used)
    def _():
        _row_gather(h_hbm, nxt_ref, buf, sem, 1 - slot, n, ident, False)

    @pl.when(i < n_used)
    def _():
        _row_gather(h_hbm, tok_ref, buf, sem, slot, n, ident, True)
        o_ref[...] = buf[slot].astype(BF16)

    @pl.when(i >= n_used)
    def _():
        o_ref[...] = jnp.zeros(o_ref.shape, o_ref.dtype)


def _gather_rows(n_used, buf_tok, h):
    N, D = h.shape
    n_blk = buf_tok.shape[0]
    return pl.pallas_call(
        _gather_kernel,
        grid_spec=pltpu.PrefetchScalarGridSpec(
            num_scalar_prefetch=1,
            grid=(n_blk,),
            in_specs=[pl.BlockSpec((1, 1, MOE_ROWS), lambda i, nu: (i, 0, 0), memory_space=pltpu.SMEM),
                      pl.BlockSpec((1, 1, MOE_ROWS), lambda i, nu: (jnp.minimum(i + 1, n_blk - 1), 0, 0),
                                   memory_space=pltpu.SMEM),
                      pl.BlockSpec(memory_space=pl.ANY)],
            out_specs=pl.BlockSpec((MOE_ROWS, D), lambda i, nu: (i, 0)),
            scratch_shapes=[pltpu.VMEM((2, MOE_ROWS, D), F32), pltpu.SemaphoreType.DMA((2,))]),
        out_shape=jax.ShapeDtypeStruct((n_blk * MOE_ROWS, D), BF16),
        compiler_params=_cparams(1),
        name="moe_gather",
    )(n_used, buf_tok, buf_tok, h)


def _expert_changed(be_ref, i):
    prev = be_ref[jnp.maximum(i - 1, 0)]
    return (i == 0) | (be_ref[i] != prev)


def _stage_expert_weights(be_ref, meta_ref, nxt_ref, w_hbm, stage, w_scr, sem, layer, windows, n_pass):
    j = pl.program_id(0)
    i = pl.program_id(1)
    e = be_ref[i]
    width = stage.shape[-1]

    def copies(ee, jj):
        return [pltpu.make_async_copy(w_hbm.at[layer, ee, :, pl.ds(pl.multiple_of(c0, LANES), width)],
                                      stage.at[p], sem.at[p])
                for p, c0 in enumerate(windows(jj))]

    @pl.when(_expert_changed(be_ref, i))
    def _():
        @pl.when((j == 0) & (i == 0))
        def _():
            for cp in copies(e, j):
                cp.start()

        for cp in copies(e, j):
            cp.wait()
        for p in range(stage.shape[0]):
            w_scr[p] = stage[p].astype(BF16)
        nxt = nxt_ref[e]
        nj = jnp.where(nxt >= 0, j, j + 1)
        ne = jnp.where(nxt >= 0, nxt, meta_ref[1])

        @pl.when(nj < n_pass)
        def _():
            for cp in copies(ne, nj):
                cp.start()


def _gmm1_kernel(be_ref, meta_ref, nxt_ref, x_ref, w_hbm, bg_ref, bl_ref, o_ref, stage, w_scr, sem, *,
                 layer, n_pass, d_ff):
    i = pl.program_id(1)

    @pl.when(i < meta_ref[0])
    def _():
        _stage_expert_weights(be_ref, meta_ref, nxt_ref, w_hbm, stage, w_scr, sem, layer,
                              lambda jj: (jj * TF, d_ff + jj * TF), n_pass)
        x = x_ref[...]
        glu = jnp.dot(x, w_scr[0], preferred_element_type=F32) + bg_ref[0, 0]
        lin = jnp.dot(x, w_scr[1], preferred_element_type=F32) + bl_ref[0, 0]
        glu = jnp.minimum(glu, SWIGLU_LIMIT)
        lin = jnp.clip(lin, -SWIGLU_LIMIT, SWIGLU_LIMIT)
        o_ref[...] = (glu * jax.nn.sigmoid(SWIGLU_ALPHA * glu) * (lin + 1.0)).astype(BF16)

    @pl.when(i >= meta_ref[0])
    def _():
        o_ref[...] = jnp.zeros(o_ref.shape, o_ref.dtype)


def _gmm2_kernel(be_ref, meta_ref, nxt_ref, a_ref, w_hbm, b_ref, o_ref, stage, w_scr, sem, *, layer, n_pass):
    i = pl.program_id(1)

    @pl.when(i < meta_ref[0])
    def _():
        _stage_expert_weights(be_ref, meta_ref, nxt_ref, w_hbm, stage, w_scr, sem, layer,
                              lambda jj: (jj * TN_MOE,), n_pass)
        o_ref[...] = jnp.dot(a_ref[...], w_scr[0], preferred_element_type=F32) + b_ref[0, 0]

    @pl.when(i >= meta_ref[0])
    def _():
        o_ref[...] = jnp.zeros(o_ref.shape, o_ref.dtype)


def _experts(layer, blk_expert, meta, nxt, xs, w1, b1, w2, b2):
    P, D = xs.shape
    L, E, _, F2 = w1.shape
    F = F2 // 2
    n_blk = P // MOE_ROWS
    nf = F // TF
    nd = D // TN_MOE
    b1r = b1.reshape(L, E, 1, F2)
    act = pl.pallas_call(
        functools.partial(_gmm1_kernel, layer=layer, n_pass=nf, d_ff=F),
        grid_spec=pltpu.PrefetchScalarGridSpec(
            num_scalar_prefetch=3,
            grid=(nf, n_blk),
            in_specs=[pl.BlockSpec((MOE_ROWS, D), lambda j, i, be, mt, nx: (i, 0)),
                      pl.BlockSpec(memory_space=pl.ANY),
                      pl.BlockSpec((1, 1, 1, TF), lambda j, i, be, mt, nx: (layer, be[i], 0, j)),
                      pl.BlockSpec((1, 1, 1, TF), lambda j, i, be, mt, nx: (layer, be[i], 0, nf + j))],
            out_specs=pl.BlockSpec((MOE_ROWS, TF), lambda j, i, be, mt, nx: (i, j)),
            scratch_shapes=[pltpu.VMEM((2, D, TF), F32), pltpu.VMEM((2, D, TF), BF16),
                            pltpu.SemaphoreType.DMA((2,))]),
        out_shape=jax.ShapeDtypeStruct((P, F), BF16),
        compiler_params=_cparams(2),
        name="moe_up",
    )(blk_expert, meta, nxt, xs, w1, b1r, b1r)
    return pl.pallas_call(
        functools.partial(_gmm2_kernel, layer=layer, n_pass=nd),
        grid_spec=pltpu.PrefetchScalarGridSpec(
            num_scalar_prefetch=3,
            grid=(nd, n_blk),
            in_specs=[pl.BlockSpec((MOE_ROWS, F), lambda j, i, be, mt, nx: (i, 0)),
                      pl.BlockSpec(memory_space=pl.ANY),
                      pl.BlockSpec((1, 1, 1, TN_MOE), lambda j, i, be, mt, nx: (layer, be[i], 0, j))],
            out_specs=pl.BlockSpec((MOE_ROWS, TN_MOE), lambda j, i, be, mt, nx: (i, j)),
            scratch_shapes=[pltpu.VMEM((1, F, TN_MOE), F32), pltpu.VMEM((1, F, TN_MOE), BF16),
                            pltpu.SemaphoreType.DMA((1,))]),
        out_shape=jax.ShapeDtypeStruct((P, D), F32),
        compiler_params=_cparams(2),
        name="moe_down",
    )(blk_expert, meta, nxt, act, w2, b2.reshape(L, E, 1, D))


def _combine_kernel(pos_ref, nxt_ref, y_hbm, x_ref, gate_ref, g2_ref, nf_ref, o_ref, buf, sem, *, final):
    i = pl.program_id(0)
    n_steps = pl.num_programs(0)
    n_tok = x_ref.shape[0]
    n_rows = TOP_K * n_tok
    slot = i % 2
    by_choice = lambda a: (a % TOP_K) * n_tok + a // TOP_K

    @pl.when(i == 0)
    def _():
        _row_gather(y_hbm, pos_ref, buf, sem, 0, n_rows, by_choice, False)

    @pl.when(i + 1 < n_steps)
    def _():
        _row_gather(y_hbm, nxt_ref, buf, sem, 1 - slot, n_rows, by_choice, False)

    _row_gather(y_hbm, pos_ref, buf, sem, slot, n_rows, by_choice, True)
    gate = gate_ref[...]
    moe = gate[:, 0:1] * buf[slot, 0:n_tok, :]
    for k in range(1, TOP_K):
        moe = moe + gate[:, k:k + 1] * buf[slot, k * n_tok:(k + 1) * n_tok, :]
    x = x_ref[...] + g2_ref[0] * moe
    if final:
        ms = jnp.mean(x * x, axis=-1, keepdims=True)
        x = x * lax.rsqrt(ms + EPS) * nf_ref[...]
    o_ref[...] = x


def _combine(pos, y, x2d, gate, g2, norm_final, T, final):
    N, D = x2d.shape
    nt = T // TC
    n_steps = N // TC
    return pl.pallas_call(
        functools.partial(_combine_kernel, final=final),
        grid=(n_steps,),
        in_specs=[pl.BlockSpec((1, 1, TOP_K * TC), lambda i: (i, 0, 0), memory_space=pltpu.SMEM),
                  pl.BlockSpec((1, 1, TOP_K * TC), lambda i: (jnp.minimum(i + 1, n_steps - 1), 0, 0),
                               memory_space=pltpu.SMEM),
                  pl.BlockSpec(memory_space=pl.ANY),
                  pl.BlockSpec((TC, D), lambda i: (i, 0)),
                  pl.BlockSpec((TC, LANES), lambda i: (i, 0)),
                  pl.BlockSpec((1, 1, D), lambda i: (i // nt, 0, 0)),
                  pl.BlockSpec((1, D), lambda i: (0, 0))],
        out_specs=pl.BlockSpec((TC, D), lambda i: (i, 0)),
        out_shape=jax.ShapeDtypeStruct((N, D), F32),
        scratch_shapes=[pltpu.VMEM((2, TOP_K * TC, D), F32), pltpu.SemaphoreType.DMA((2,))],
        compiler_params=_cparams(1),
        name="moe_combine_final" if final else "moe_combine",
    )(pos, pos, y, x2d, gate, g2, norm_final)


def _routing_tables(top_idx):
    N = top_idx.shape[0]
    NK = N * TOP_K
    e_flat = top_idx.reshape(NK)
    onehot = (e_flat[:, None] == jnp.arange(N_EXPERTS, dtype=jnp.int32)[None, :]).astype(jnp.int32)
    csum = jnp.cumsum(onehot, axis=0)
    counts = csum[-1]
    rank = jnp.sum(csum * onehot, axis=1) - 1
    padded = (counts + MOE_ROWS - 1) // MOE_ROWS * MOE_ROWS
    pad_end = jnp.cumsum(padded)
    pad_start = pad_end - padded
    dest = jnp.sum(onehot * pad_start[None, :], axis=1) + rank
    P = NK + N_EXPERTS * MOE_ROWS
    n_blk = P // MOE_ROWS
    tok = jnp.arange(NK, dtype=jnp.int32) // TOP_K
    buf_tok = jnp.zeros((P,), jnp.int32).at[dest].set(tok)
    blk_start = jnp.arange(n_blk, dtype=jnp.int32) * MOE_ROWS
    blk_expert = jnp.minimum(jnp.sum(blk_start[:, None] >= pad_end[None, :], axis=-1), N_EXPERTS - 1)
    n_used = pad_end[-1] // MOE_ROWS
    ids = jnp.arange(N_EXPERTS, dtype=jnp.int32)
    at_or_after = lax.cummin(jnp.where(counts > 0, ids, N_EXPERTS), reverse=True)
    nxt = jnp.concatenate([at_or_after[1:], jnp.full((1,), N_EXPERTS, jnp.int32)])
    nxt = jnp.where(nxt < N_EXPERTS, nxt, -1)
    meta = jnp.stack([n_used, at_or_after[0]])
    return (dest.astype(jnp.int32), buf_tok.reshape(n_blk, 1, MOE_ROWS), blk_expert.astype(jnp.int32),
            meta.astype(jnp.int32), nxt.astype(jnp.int32))


def _rope_tables(T):
    inv = 1.0 / (ROPE_THETA ** (jnp.arange(0, HEAD_DIM, 2, dtype=F32) / HEAD_DIM))
    ang = jnp.arange(T, dtype=F32)[:, None] * inv[None, :]
    return jnp.cos(ang), jnp.sin(ang)


def kernel(x, c, w_mod, b_mod, norm_attn, norm_ffn, w_in, cmp_pe, cmp_w1, cmp_w2, mix_norm, w_out,
           w_router, b_router, w_exp1, b_exp1, w_exp2, b_exp2, norm_final):
    B, T, D = x.shape
    N = B * T
    G = B_KV_HEADS
    assert D == D_MODEL and T % TM_PROJ == 0 and T % TM_ROUTE == 0
    assert all(T % n == 0 for n in (TQ_A, TK_A, TQ_B, TK_B, TQ_C, TK_C)) and TK_B % TQ_B == 0 and TQ_C % TK_C == 0

    cos, sin = _rope_tables(T)
    cos128 = jnp.tile(cos, (1, LANES // (HEAD_DIM // 2)))
    sin128 = jnp.tile(jnp.concatenate([-sin, sin], axis=1), (1, LANES // HEAD_DIM))
    n_cmp = (T - CMP_BLOCK) // CMP_STRIDE + 1
    n_chunk = T // CMP_STRIDE
    assert n_chunk == n_cmp + 1
    cmp_end = np.arange(n_chunk) * CMP_STRIDE + CMP_BLOCK - 1
    cmp_end = np.minimum(cmp_end, T - 1)
    cosc = jnp.stack([jnp.tile(cos[cmp_end], (1, 4)), jnp.ones((n_chunk, LANES), F32)])
    sinc = jnp.stack([jnp.tile(sin[cmp_end], (1, 4)), jnp.zeros((n_chunk, LANES), F32)])
    tbl = jnp.asarray(_dilated_bias_table())
    ov = jnp.asarray(_overlap_table_t(T))

    mod = _modulation(c, w_mod, b_mod).reshape(DEPTH, B, 6, 1, D)
    w_in_re = _permute_cols(w_in).astype(BF16).reshape(DEPTH, D, N_COLS // TN_IN, TN_IN).transpose(0, 2, 1, 3)
    w_out_bf = w_out.astype(BF16)
    wr_pad = jnp.zeros((DEPTH, D, LANES), F32).at[:, :, :N_EXPERTS].set(w_router)
    br_pad = jnp.zeros((DEPTH, 1, LANES), F32).at[:, 0, :N_EXPERTS].set(b_router)
    half = HEAD_DIM // 2
    w2d = jnp.concatenate([cmp_w2, cmp_w2], axis=-1)
    w2rot = jnp.concatenate([-cmp_w2[..., half:], cmp_w2[..., :half]], axis=-1)
    w2r = jnp.concatenate([w2rot, w2rot], axis=-1)
    pe = cmp_pe.reshape(DEPTH, 2, 2, CMP_STRIDE * HEAD_DIM)

    xf = x.reshape(N, D)
    for i in range(DEPTH):
        sh1, sc1, g1, sh2, sc2, g2 = [mod[i, :, k] for k in range(6)]
        proj = _in_projection(xf, sc1, sh1, norm_attn[i][None, :], w_in_re[i], cos128, sin128, T)
        proj3 = proj.reshape(B, T, N_COLS)
        o_a = _attention_a(proj3, tbl)
        kcvc = proj3[:, :, BLK_KCVC * LANES:(BLK_KCVC + 3) * LANES]
        chunks = kcvc.reshape(B, T, 2, G, HEAD_DIM).transpose(0, 2, 3, 1, 4).reshape(
            B, 2, G, n_chunk, CMP_STRIDE * HEAD_DIM)
        cmp_kv = _compress(chunks, pe[i], cmp_w1[i], w2d[i], w2r[i], cosc, sinc)
        o_b = _attention_b(proj3, cmp_kv, ov)
        o_c = _attention_c(proj3)
        xf = _out_projection(o_a.reshape(N, -1), o_b.reshape(N, -1), o_c.reshape(N, -1),
                             mix_norm[i][None, :], w_out_bf[i], xf, g1, T)
        h2, idx128, gate128 = _router(xf, sc2, sh2, norm_ffn[i][None, :], wr_pad[i], br_pad[i], T)
        dest, buf_tok, blk_expert, meta, nxt = _routing_tables(idx128[:, :TOP_K])
        xs = _gather_rows(meta, buf_tok, h2)
        y = _experts(i, blk_expert, meta, nxt, xs, w_exp1, b_exp1, w_exp2, b_exp2)
        xf = _combine(dest.reshape(N // TC, 1, TOP_K * TC), y, xf, gate128, g2,
                      norm_final[None, :], T, final=(i == DEPTH - 1))
    return xf.reshape(B, T, D)
```

```python
import functools

import numpy as np
import jax
import jax.numpy as jnp
from jax import lax
from jax.experimental import pallas as pl
from jax.experimental.pallas import tpu as pltpu

F32 = jnp.float32
BF16 = jnp.bfloat16
HIGHEST = lax.Precision.HIGHEST

D_MODEL = 2048
DEPTH = 2
HEAD_DIM = 64
A_HEADS = 12
A_PATTERNS = ((128, 1), (512, 4), (2048, 16))
B_HEADS = 12
B_KV_HEADS = 3
B_BRANCHES = 3
CMP_BLOCK = 32
CMP_STRIDE = 16
CMP_HIDDEN = 128
SLC_BLOCK = 64
SLC_TOPK = 16
WIN = 512
C_HEADS = 8
A_QKV_W = 3 * A_HEADS * HEAD_DIM
B_Q_W = B_HEADS * HEAD_DIM
B_KV_W = 2 * B_BRANCHES * B_KV_HEADS * HEAD_DIM
B_GATE_W = B_BRANCHES * B_HEADS
C_QKV_W = 3 * C_HEADS * HEAD_DIM
N_EXPERTS = 32
TOP_K = 4
D_FF = 2048
SWIGLU_LIMIT = 7.0
SWIGLU_ALPHA = 1.702
ROPE_THETA = 10000.0
EPS = 1e-6
NEG_BIG = -1e30
TINY = 1e-30
FORCE = 1e4
LOWEST = -3.0e38
RUN_DEAD = 110.0

LANES = 128
VMEM_LIMIT = 56 * 1024 * 1024

TQ_A, TK_A = 512, 512
TQ_B, TK_B = 256, 512
TQ_C, TK_C = 512, 256
TM_PROJ = 512
TN_IN = 768
TN_OUT = 512
TM_ROUTE = 256
MOE_ROWS = 256
TF = 1024
TN_MOE = 2048
TC = 128

BLK_AQ, BLK_AK, BLK_BQ, BLK_KS, BLK_KW = 0, 6, 12, 18, 21
N_ROPE_BLKS = 24
BLK_AV, BLK_KCVC, BLK_VS, BLK_VW = 24, 30, 33, 36
BLK_CQ, BLK_CK, BLK_CV = 39, 43, 47
BLK_GATE = 51
N_BLKS = 54
N_COLS = N_BLKS * LANES
assert N_ROPE_BLKS * LANES % TN_IN == 0 and N_COLS % TN_IN == 0


def _cparams(n_axes):
    return pltpu.CompilerParams(dimension_semantics=("arbitrary",) * n_axes,
                                vmem_limit_bytes=VMEM_LIMIT)


def _layout():
    a0 = 0
    bq0 = A_QKV_W
    bkv0 = bq0 + B_Q_W
    bg0 = bkv0 + B_KV_W
    c0 = bg0 + B_GATE_W
    scale = HEAD_DIM ** -0.5
    runs, colscale = [], []

    def add(start, stop, s, blk):
        assert len(colscale) == blk * LANES
        runs.append((start, stop))
        colscale.extend([s] * (stop - start))

    def add_dup(arr, blk):
        for g in range(B_KV_HEADS):
            s0 = bkv0 + arr * gw + g * HEAD_DIM
            add(s0, s0 + HEAD_DIM, 1.0, blk + g)
            runs.append((s0, s0 + HEAD_DIM))
            colscale.extend([1.0] * HEAD_DIM)

    hw = A_HEADS * HEAD_DIM
    gw = B_KV_HEADS * HEAD_DIM
    cw = C_HEADS * HEAD_DIM
    add(a0, a0 + hw, scale, BLK_AQ)
    add(a0 + hw, a0 + 2 * hw, 1.0, BLK_AK)
    add(bq0, bq0 + B_Q_W, scale, BLK_BQ)
    add_dup(2, BLK_KS)
    add_dup(4, BLK_KW)
    add(a0 + 2 * hw, a0 + 3 * hw, 1.0, BLK_AV)
    add(bkv0, bkv0 + 2 * gw, 1.0, BLK_KCVC)
    add_dup(3, BLK_VS)
    add_dup(5, BLK_VW)
    add(c0, c0 + cw, scale, BLK_CQ)
    add(c0 + cw, c0 + 3 * cw, 1.0, BLK_CK)
    per_g = B_GATE_W // B_KV_HEADS
    for g in range(B_KV_HEADS):
        add(bg0 + g * per_g, bg0 + (g + 1) * per_g, 1.0, BLK_GATE + g)
        runs.append((None, LANES - per_g))
        colscale.extend([1.0] * (LANES - per_g))
    assert len(colscale) == N_COLS
    return runs, np.asarray(colscale, np.float32)[None, :]


_RUNS, _COL_SCALE = _layout()


def _permute_cols(w):
    parts = []
    for start, stop in _RUNS:
        if start is None:
            parts.append(jnp.zeros(w.shape[:-1] + (stop,), w.dtype))
        else:
            parts.append(w[..., start:stop])
    return jnp.concatenate(parts, axis=-1)


def _dilated_bias_table():
    assert TQ_A == TK_A
    n = -(-A_PATTERNS[-1][0] // TK_A) + 1
    d = (np.arange(n)[:, None, None] * TK_A + np.arange(TQ_A)[None, :, None] - np.arange(TK_A)[None, None, :])
    m = np.zeros(d.shape, np.float64)
    for window, dil in A_PATTERNS:
        m += ((d >= 0) & (d <= window) & (d % dil == 0))
    return np.where(m > 0, np.log(np.maximum(m, 1.0)), NEG_BIG).astype(np.float32)


def _overlap_table_t(T):
    n_cmp = (T - CMP_BLOCK) // CMP_STRIDE + 1
    n_slc = T // SLC_BLOCK
    cs = np.arange(n_cmp) * CMP_STRIDE
    ss = np.arange(n_slc) * SLC_BLOCK
    ov = (cs[None, :] < ss[:, None] + SLC_BLOCK) & (cs[None, :] + CMP_BLOCK > ss[:, None])
    out = np.zeros((n_slc, n_cmp + 1), np.float32)
    out[:n_slc, :n_cmp] = ov
    return out


def _mod_kernel(c_ref, w_ref, b_ref, o_ref):
    c = c_ref[...]
    ca = (c * jax.nn.sigmoid(c)).astype(BF16)
    y = jnp.dot(ca, w_ref[0].astype(BF16), preferred_element_type=F32)
    o_ref[0] = y + b_ref[0]


def _modulation(c, w_mod, b_mod):
    B, D = c.shape
    L, _, W = w_mod.shape
    tn = 1024
    rows = 8
    cp = jnp.zeros((rows, D), F32).at[:B].set(c)
    out = pl.pallas_call(
        _mod_kernel,
        grid=(L, W // tn),
        in_specs=[pl.BlockSpec((rows, D), lambda l, j: (0, 0)),
                  pl.BlockSpec((1, D, tn), lambda l, j: (l, 0, j)),
                  pl.BlockSpec((1, 1, tn), lambda l, j: (l, 0, j))],
        out_specs=pl.BlockSpec((1, rows, tn), lambda l, j: (l, 0, j)),
        out_shape=jax.ShapeDtypeStruct((L, rows, W), F32),
        compiler_params=_cparams(2),
        name="adaln_mod",
    )(cp, w_mod, b_mod.reshape(L, 1, W))
    return out[:, :B]


def _norm_mod(x, gain, sc, sh):
    ms = jnp.mean(x * x, axis=-1, keepdims=True)
    y = x * lax.rsqrt(ms + EPS) * gain
    return y * (1.0 + sc) + sh


def _rope128(y, cos, sin_signed):
    lane = lax.broadcasted_iota(jnp.int32, y.shape, 1)
    first_half = (lane % HEAD_DIM) < (HEAD_DIM // 2)
    rot = jnp.where(first_half, pltpu.roll(y, LANES - HEAD_DIM // 2, 1), pltpu.roll(y, HEAD_DIM // 2, 1))
    return y * cos + rot * sin_signed


def _inproj_kernel(x_ref, sc_ref, sh_ref, g_ref, w_ref, cs_ref, cos_ref, sin_ref, o_ref, h_scr):
    @pl.when(pl.program_id(1) == 0)
    def _():
        h_scr[...] = _norm_mod(x_ref[...], g_ref[...], sc_ref[0], sh_ref[0]).astype(BF16)

    y = jnp.dot(h_scr[...], w_ref[...], preferred_element_type=F32)
    n_rope_tiles = N_ROPE_BLKS * LANES // TN_IN

    @pl.when(pl.program_id(1) < n_rope_tiles)
    def _():
        cos = cos_ref[...]
        sin = sin_ref[...]
        for c in range(TN_IN // LANES):
            sl = slice(c * LANES, (c + 1) * LANES)
            o_ref[:, sl] = _rope128(y[:, sl], cos, sin) * cs_ref[:, sl]

    @pl.when(pl.program_id(1) >= n_rope_tiles)
    def _():
        o_ref[...] = y * cs_ref[...]


def _in_projection(x2d, sc, sh, gain, w_re, cos128, sin128, T):
    N, D = x2d.shape
    nt = T // TM_PROJ
    return pl.pallas_call(
        _inproj_kernel,
        grid=(N // TM_PROJ, N_COLS // TN_IN),
        in_specs=[pl.BlockSpec((TM_PROJ, D), lambda i, j: (i, 0)),
                  pl.BlockSpec((1, 1, D), lambda i, j: (i // nt, 0, 0)),
                  pl.BlockSpec((1, 1, D), lambda i, j: (i // nt, 0, 0)),
                  pl.BlockSpec((1, D), lambda i, j: (0, 0)),
                  pl.BlockSpec((D, TN_IN), lambda i, j: (0, j)),
                  pl.BlockSpec((1, TN_IN), lambda i, j: (0, j)),
                  pl.BlockSpec((TM_PROJ, LANES), lambda i, j: (i % nt, 0)),
                  pl.BlockSpec((TM_PROJ, LANES), lambda i, j: (i % nt, 0))],
        out_specs=pl.BlockSpec((TM_PROJ, TN_IN), lambda i, j: (i, j)),
        out_shape=jax.ShapeDtypeStruct((N, N_COLS), F32),
        scratch_shapes=[pltpu.VMEM((TM_PROJ, D), BF16)],
        compiler_params=_cparams(2),
        name="in_proj",
    )(x2d, sc, sh, gain, w_re, jnp.asarray(_COL_SCALE), cos128, sin128)


def _split_heads_q(q):
    lane = lax.broadcasted_iota(jnp.int32, q.shape, 1)
    lo = lane < HEAD_DIM
    return jnp.concatenate([jnp.where(lo, q, 0.0), jnp.where(lo, 0.0, q)], axis=0)


def _scores(q2, k):
    return lax.dot_general(q2, k, (((1,), (1,)), ((), ())), preferred_element_type=F32)


def _flash_step(s, m, acc, v_ones):
    m_new = jnp.maximum(m, jnp.max(s, axis=-1, keepdims=True))
    p = jnp.exp(s - m_new).astype(BF16)
    acc = jnp.exp(m - m_new) * acc + jnp.dot(p, v_ones, preferred_element_type=F32)
    return m_new, acc


def _flash_init(rows):
    return jnp.full((rows, 1), NEG_BIG, F32), jnp.zeros((rows, LANES), F32)


def _flash_finish(acc):
    return acc / jnp.maximum(pltpu.roll(acc, HEAD_DIM, 1), TINY)


def _high_half(x):
    bits = lax.bitcast_convert_type(x, jnp.int32)
    return lax.bitcast_convert_type(bits & jnp.int32(-65536), F32)


def _split_bf16(x):
    hi = x.astype(BF16)
    return hi, (x - hi.astype(F32)).astype(BF16)


def _tile(ref, kt, n):
    return ref[0, pl.ds(pl.multiple_of(kt * n, n), n), :]


def _two_head_norm(out):
    lane = lax.broadcasted_iota(jnp.int32, out.shape, 1)
    lo = lane < HEAD_DIM
    sq = out * out
    ms0 = jnp.sum(jnp.where(lo, sq, 0.0), axis=-1, keepdims=True) * (1.0 / HEAD_DIM)
    ms1 = jnp.sum(jnp.where(lo, 0.0, sq), axis=-1, keepdims=True) * (1.0 / HEAD_DIM)
    return out * jnp.where(lo, lax.rsqrt(ms0 + EPS), lax.rsqrt(ms1 + EPS))


def _attn_a_kernel(q_ref, k_ref, v_ref, tbl_ref, o_ref):
    qi = pl.program_id(2)
    q = q_ref[0]
    lane = lax.broadcasted_iota(jnp.int32, q.shape, 1)
    lo = lane < HEAD_DIM
    q0 = jnp.where(lo, q, 0.0).astype(BF16)
    q1 = jnp.where(lo, 0.0, q).astype(BF16)
    n_d = tbl_ref.shape[0]

    def body(i, carry):
        m0, acc0, m1, acc1 = carry
        k = _tile(k_ref, qi - i, TK_A).astype(BF16)
        v = _tile(v_ref, qi - i, TK_A)
        lo_k = lax.broadcasted_iota(jnp.int32, v.shape, 1) < HEAD_DIM
        bias = tbl_ref[i]
        m0, acc0 = _flash_step(_scores(q0, k) + bias, m0, acc0, jnp.where(lo_k, v, 1.0).astype(BF16))
        m1, acc1 = _flash_step(_scores(q1, k) + bias, m1, acc1, jnp.where(lo_k, 1.0, v).astype(BF16))
        return m0, acc0, m1, acc1

    init = _flash_init(TQ_A) + _flash_init(TQ_A)
    _, acc0, _, acc1 = lax.fori_loop(0, jnp.minimum(qi, n_d - 1) + 1, body, init)
    o_ref[0] = _two_head_norm(jnp.where(lo, _flash_finish(acc0), _flash_finish(acc1)))


def _attention_a(proj, tbl):
    B, T, _ = proj.shape
    n_pairs = A_HEADS // 2
    return pl.pallas_call(
        _attn_a_kernel,
        grid=(B, n_pairs, T // TQ_A),
        in_specs=[pl.BlockSpec((1, TQ_A, LANES), lambda b, p, i: (b, i, BLK_AQ + p)),
                  pl.BlockSpec((1, T, LANES), lambda b, p, i: (b, 0, BLK_AK + p)),
                  pl.BlockSpec((1, T, LANES), lambda b, p, i: (b, 0, BLK_AV + p)),
                  pl.BlockSpec(tbl.shape, lambda b, p, i: (0, 0, 0))],
        out_specs=pl.BlockSpec((1, TQ_A, LANES), lambda b, p, i: (b, i, p)),
        out_shape=jax.ShapeDtypeStruct((B, T, n_pairs * LANES), F32),
        compiler_params=_cparams(3),
        name="attn_dilated",
    )(proj, proj, proj, tbl)


def _attn_c_kernel(q_ref, k_ref, v_ref, o_ref):
    qi = pl.program_id(2)
    q = q_ref[0]
    lo = lax.broadcasted_iota(jnp.int32, q.shape, 1) < HEAD_DIM
    qs = (jnp.where(lo, q, 0.0).astype(BF16), jnp.where(lo, 0.0, q).astype(BF16))
    diff = (lax.broadcasted_iota(jnp.int32, (TQ_C, TK_C), 1) - lax.broadcasted_iota(jnp.int32, (TQ_C, TK_C), 0))
    uj = lax.broadcasted_iota(jnp.int32, (TK_C, TK_C), 0)
    us = lax.broadcasted_iota(jnp.int32, (TK_C, TK_C), 1)
    upper = (uj >= us).astype(BF16)
    tiles_per_q = TQ_C // TK_C

    def tile(kt, carry, masked):
        k = _tile(k_ref, kt, TK_C).astype(BF16)
        v = _tile(v_ref, kt, TK_C).astype(BF16)
        if masked:
            before = diff < qi * TQ_C - kt * TK_C
        out = []
        for qh, (run, acc) in zip(qs, carry):
            z = _scores(qh, k)
            sp = jnp.maximum(z, 0.0) + jnp.log(1.0 + jnp.exp(-jnp.abs(z)))
            if masked:
                sp = jnp.where(before, sp, 0.0)
            hi = _high_half(sp)
            suffix = (jnp.dot(hi.astype(BF16), upper, preferred_element_type=F32)
                      + jnp.dot((sp - hi).astype(BF16), upper, preferred_element_type=F32))
            a = jnp.exp(z - suffix - run)
            if masked:
                a = jnp.where(before, a, 0.0)
            out.append((run + suffix[:, 0:1], acc + jnp.dot(a.astype(BF16), v, preferred_element_type=F32)))
        return tuple(out)

    carry = ((jnp.zeros((TQ_C, 1), F32), jnp.zeros((TQ_C, LANES), F32)),) * 2
    last = (qi + 1) * tiles_per_q - 1
    for j in range(tiles_per_q):
        carry = tile(last - j, carry, True)
    def rest_is_zero(c):
        return (jnp.minimum(jnp.min(c[0][0]), jnp.min(c[1][0])) >= RUN_DEAD).astype(jnp.int32)

    def more(state):
        i, dead, _ = state
        return (i <= last) & (dead == 0)

    def step(state):
        i, _, c = state
        c = tile(last - i, c, False)
        return i + 1, rest_is_zero(c), c

    _, _, carry = lax.while_loop(more, step, (jnp.int32(tiles_per_q), rest_is_zero(carry), carry))
    o_ref[0] = _two_head_norm(jnp.where(lo, carry[0][1], carry[1][1]))


def _attention_c(proj):
    B, T, _ = proj.shape
    n_pairs = C_HEADS // 2
    return pl.pallas_call(
        _attn_c_kernel,
        grid=(B, n_pairs, T // TQ_C),
        in_specs=[pl.BlockSpec((1, TQ_C, LANES), lambda b, p, i: (b, i, BLK_CQ + p)),
                  pl.BlockSpec((1, T, LANES), lambda b, p, i: (b, 0, BLK_CK + p)),
                  pl.BlockSpec((1, T, LANES), lambda b, p, i: (b, 0, BLK_CV + p))],
        out_specs=pl.BlockSpec((1, TQ_C, LANES), lambda b, p, i: (b, i, p)),
        out_shape=jax.ShapeDtypeStruct((B, T, n_pairs * LANES), F32),
        compiler_params=_cparams(3),
        name="attn_stickbreak",
    )(proj, proj, proj)


def _compress_kernel(x_ref, pe_ref, w1_ref, w2_ref, w2r_ref, cos_ref, sin_ref, o_ref):
    x = x_ref[0, 0, 0]
    half = x.shape[1]
    ha = jnp.dot(x + pe_ref[0, 0:1, :], w1_ref[0, :half, :], precision=HIGHEST, preferred_element_type=F32)
    hb = jnp.dot(x + pe_ref[0, 1:2, :], w1_ref[0, half:, :], precision=HIGHEST, preferred_element_type=F32)
    n = x.shape[0]
    h = ha + pltpu.roll(hb, n - 1, 0)
    g = jax.nn.gelu(h)
    y = jnp.dot(g, w2_ref[0], precision=HIGHEST, preferred_element_type=F32)
    yr = jnp.dot(g, w2r_ref[0], precision=HIGHEST, preferred_element_type=F32)
    out = y * cos_ref[0] + yr * sin_ref[0]
    rowi = lax.broadcasted_iota(jnp.int32, out.shape, 0)
    o_ref[0, 0, 0] = jnp.where(rowi < n - 1, out, 0.0)


def _compress(chunks, pe, w1, w2d, w2r, cosc, sinc):
    B, _, G, n, cw = chunks.shape
    return pl.pallas_call(
        _compress_kernel,
        grid=(B, 2, G),
        in_specs=[pl.BlockSpec((1, 1, 1, n, cw), lambda b, w, g: (b, w, g, 0, 0)),
                  pl.BlockSpec((1, 2, cw), lambda b, w, g: (w, 0, 0)),
                  pl.BlockSpec((1, 2 * cw, CMP_HIDDEN), lambda b, w, g: (w, 0, 0)),
                  pl.BlockSpec((1, CMP_HIDDEN, LANES), lambda b, w, g: (w, 0, 0)),
                  pl.BlockSpec((1, CMP_HIDDEN, LANES), lambda b, w, g: (w, 0, 0)),
                  pl.BlockSpec((1, n, LANES), lambda b, w, g: (w, 0, 0)),
                  pl.BlockSpec((1, n, LANES), lambda b, w, g: (w, 0, 0))],
        out_specs=pl.BlockSpec((1, 1, 1, n, LANES), lambda b, w, g: (b, w, g, 0, 0)),
        out_shape=jax.ShapeDtypeStruct((B, 2, G, n, LANES), F32),
        compiler_params=_cparams(3),
        name="nsa_compress",
    )(chunks, pe, w1, w2d, w2r, cosc, sinc)


def _nsa_kernel(q_ref, cmp_k_ref, cmp_v_ref, ks_ref, vs_ref, kw_ref, vw_ref, gate_ref, ovt_ref, o_ref):
    qi = pl.program_id(2)
    R = B_HEADS // B_KV_HEADS
    TQ, TK = TQ_B, TK_B
    rows = R * TQ
    q0 = qi * TQ
    q = q_ref[0]
    qf = jnp.concatenate([_split_heads_q(q[:, :LANES]), _split_heads_q(q[:, LANES:])], axis=0)
    q4 = qf.astype(BF16)

    kc = cmp_k_ref[0, 0, 0]
    vc = cmp_v_ref[0, 0, 0].astype(BF16)
    n_c = kc.shape[0]
    last_end = (n_c - 1) * CMP_STRIDE + CMP_BLOCK - 1
    t_col = q0 + lax.broadcasted_iota(jnp.int32, (TQ, 1), 0)
    cmp_end = lax.broadcasted_iota(jnp.int32, (TQ, n_c), 1) * CMP_STRIDE + (CMP_BLOCK - 1)
    valid = (cmp_end <= t_col) & (cmp_end < last_end)
    t_lane = q0 + lax.broadcasted_iota(jnp.int32, (1, TQ), 1)
    nt_dims = (((1,), (1,)), ((), ()))
    p_sum = jnp.zeros((TQ, n_c), F32)
    o_cmp = []
    kc_hi, kc_lo = _split_bf16(kc)
    for r in range(R):
        q_hi, q_lo = _split_bf16(qf[r * TQ:(r + 1) * TQ])
        s = _scores(q_hi, kc_hi) + (_scores(q_hi, kc_lo) + _scores(q_lo, kc_hi))
        s = jnp.where(valid, s, NEG_BIG)
        p = jnp.exp(s - jnp.max(s, axis=-1, keepdims=True)) * valid.astype(F32)
        p = p / jnp.maximum(jnp.sum(p, axis=-1, keepdims=True), TINY)
        o_cmp.append(jnp.dot(p.astype(BF16), vc, preferred_element_type=F32))
        p_sum = p_sum + p
    imp = lax.dot_general(ovt_ref[...], p_sum, nt_dims, precision=HIGHEST, preferred_element_type=F32)
    n_slc = imp.shape[0]
    blk = lax.broadcasted_iota(jnp.int32, (n_slc, TQ), 0)
    blk_f = blk.astype(F32)
    tb = t_lane // SLC_BLOCK
    forced = (blk == 0) | (blk == tb) | (blk == tb - 1)
    imp = jnp.where(forced, imp + FORCE, imp)
    imp = jnp.where(blk > tb, -FORCE, imp)
    sel = jnp.zeros((n_slc, TQ), F32)
    for _ in range(min(SLC_TOPK, n_slc)):
        mx = jnp.max(imp, axis=0, keepdims=True)
        first = jnp.min(jnp.where(imp == mx, blk_f, float(n_slc)), axis=0, keepdims=True)
        pick = blk_f == first
        sel = jnp.where(pick, 1.0, sel)
        imp = jnp.where(pick, LOWEST, imp)
    sel_bias = jnp.concatenate([(sel - 1.0) * (-NEG_BIG), jnp.zeros((LANES - n_slc, TQ), F32)], axis=0)
    sel_bias = sel_bias.T.astype(BF16)

    diff = lax.broadcasted_iota(jnp.int32, (TQ, TK), 1) - lax.broadcasted_iota(jnp.int32, (TQ, TK), 0)
    e_row = lax.broadcasted_iota(jnp.int32, (LANES, TK), 0)
    e_col = lax.broadcasted_iota(jnp.int32, (LANES, TK), 1) // SLC_BLOCK
    lo_k = lax.broadcasted_iota(jnp.int32, (TK, LANES), 1) < HEAD_DIM
    last = q0 // TK

    def step(k_ref, v_ref, kt, bias, carry):
        k = _tile(k_ref, kt, TK).astype(BF16)
        v_ones = jnp.where(lo_k, _tile(v_ref, kt, TK), 1.0).astype(BF16)
        s = (_scores(q4, k).reshape(R, TQ, TK) + bias[None]).reshape(rows, TK)
        return _flash_step(s, *carry, v_ones)

    def slc_tile(kt, carry, causal):
        expand = (e_row == kt * (TK // SLC_BLOCK) + e_col).astype(BF16)
        bias = jnp.dot(sel_bias, expand, preferred_element_type=F32)
        if causal:
            bias = bias + jnp.where(diff <= q0 - kt * TK, 0.0, NEG_BIG)
        return step(ks_ref, vs_ref, kt, bias, carry)

    carry = lax.fori_loop(0, last, lambda kt, c: slc_tile(kt, c, False), _flash_init(rows))
    o_slc = _flash_finish(slc_tile(last, carry, True)[1])

    def win_tile(i, carry):
        kt = last - i
        off = q0 - kt * TK
        bias = jnp.where((diff <= off) & (diff > off - WIN), 0.0, NEG_BIG)
        return step(kw_ref, vw_ref, kt, bias, carry)

    first_win = jnp.maximum(q0 - (WIN - 1), 0) // TK
    o_win = _flash_finish(lax.fori_loop(0, last - first_win + 1, win_tile, _flash_init(rows))[1])

    gates = jax.nn.sigmoid(gate_ref[0])
    lo = lax.broadcasted_iota(jnp.int32, (TQ, LANES), 1) < HEAD_DIM
    outs = []
    for r in range(R):
        rs = slice(r * TQ, (r + 1) * TQ)
        o = (gates[:, 3 * r:3 * r + 1] * o_cmp[r] + gates[:, 3 * r + 1:3 * r + 2] * o_slc[rs]
             + gates[:, 3 * r + 2:3 * r + 3] * o_win[rs])
        ms = jnp.sum(jnp.where(lo, o * o, 0.0), axis=-1, keepdims=True) * (1.0 / HEAD_DIM)
        outs.append(o * lax.rsqrt(ms + EPS))
    pairs = [jnp.where(lo, outs[2 * j], pltpu.roll(outs[2 * j + 1], HEAD_DIM, 1)) for j in range(R // 2)]
    o_ref[0] = jnp.concatenate(pairs, axis=1)


def _attention_b(proj, cmp_kv, ov):
    B, T, _ = proj.shape
    G = B_KV_HEADS
    n_c = cmp_kv.shape[3]
    qw = 2 * LANES
    TQ = TQ_B
    return pl.pallas_call(
        _nsa_kernel,
        grid=(B, G, T // TQ),
        in_specs=[pl.BlockSpec((1, TQ, qw), lambda b, g, i: (b, i, BLK_BQ // 2 + g)),
                  pl.BlockSpec((1, 1, 1, n_c, LANES), lambda b, g, i: (b, 0, g, 0, 0)),
                  pl.BlockSpec((1, 1, 1, n_c, LANES), lambda b, g, i: (b, 1, g, 0, 0)),
                  pl.BlockSpec((1, T, LANES), lambda b, g, i: (b, 0, BLK_KS + g)),
                  pl.BlockSpec((1, T, LANES), lambda b, g, i: (b, 0, BLK_VS + g)),
                  pl.BlockSpec((1, T, LANES), lambda b, g, i: (b, 0, BLK_KW + g)),
                  pl.BlockSpec((1, T, LANES), lambda b, g, i: (b, 0, BLK_VW + g)),
                  pl.BlockSpec((1, TQ, LANES), lambda b, g, i: (b, i, BLK_GATE + g)),
                  pl.BlockSpec(ov.shape, lambda b, g, i: (0, 0))],
        out_specs=pl.BlockSpec((1, TQ, qw), lambda b, g, i: (b, i, g)),
        out_shape=jax.ShapeDtypeStruct((B, T, B_HEADS * HEAD_DIM), F32),
        compiler_params=_cparams(3),
        name="attn_nsa",
    )(proj, cmp_kv, cmp_kv, proj, proj, proj, proj, proj, ov)


def _outproj_kernel(oa_ref, ob_ref, oc_ref, mn_ref, w_ref, x_ref, g_ref, o_ref, h_scr):
    wa = oa_ref.shape[1]
    wb = ob_ref.shape[1]

    @pl.when(pl.program_id(1) == 0)
    def _():
        h_scr[:, :wa] = (oa_ref[...] * mn_ref[:, :wa]).astype(BF16)
        h_scr[:, wa:wa + wb] = (ob_ref[...] * mn_ref[:, wa:wa + wb]).astype(BF16)
        h_scr[:, wa + wb:] = (oc_ref[...] * mn_ref[:, wa + wb:]).astype(BF16)

    y = jnp.dot(h_scr[...], w_ref[...], preferred_element_type=F32)
    o_ref[...] = x_ref[...] + g_ref[0] * y


def _out_projection(oa, ob, oc, mix_norm, w_out_bf, x2d, g1, T):
    N, D = x2d.shape
    nt = T // TM_PROJ
    wa, wb, wc = oa.shape[1], ob.shape[1], oc.shape[1]
    return pl.pallas_call(
        _outproj_kernel,
        grid=(N // TM_PROJ, D // TN_OUT),
        in_specs=[pl.BlockSpec((TM_PROJ, wa), lambda i, j: (i, 0)),
                  pl.BlockSpec((TM_PROJ, wb), lambda i, j: (i, 0)),
                  pl.BlockSpec((TM_PROJ, wc), lambda i, j: (i, 0)),
                  pl.BlockSpec((1, D), lambda i, j: (0, 0)),
                  pl.BlockSpec((D, TN_OUT), lambda i, j: (0, j)),
                  pl.BlockSpec((TM_PROJ, TN_OUT), lambda i, j: (i, j)),
                  pl.BlockSpec((1, 1, TN_OUT), lambda i, j: (i // nt, 0, j))],
        out_specs=pl.BlockSpec((TM_PROJ, TN_OUT), lambda i, j: (i, j)),
        out_shape=jax.ShapeDtypeStruct((N, D), F32),
        scratch_shapes=[pltpu.VMEM((TM_PROJ, D), BF16)],
        compiler_params=_cparams(2),
        name="out_proj",
    )(oa, ob, oc, mix_norm, w_out_bf, x2d, g1)


def _router_kernel(x_ref, sc_ref, sh_ref, g_ref, wr_ref, br_ref, h_ref, idx_ref, gate_ref):
    h = _norm_mod(x_ref[...], g_ref[...], sc_ref[0], sh_ref[0])
    h_ref[...] = h
    logits = jnp.dot(h, wr_ref[...], precision=HIGHEST, preferred_element_type=F32) + br_ref[...]
    lane = lax.broadcasted_iota(jnp.int32, logits.shape, 1)
    lane_f = lane.astype(F32)
    cur = jnp.where(lane < N_EXPERTS, logits, LOWEST)
    idx_out = jnp.zeros(logits.shape, F32)
    e_out = jnp.zeros(logits.shape, F32)
    top0 = None
    denom = None
    for k in range(TOP_K):
        mx = jnp.max(cur, axis=-1, keepdims=True)
        first = jnp.min(jnp.where(cur == mx, lane_f, float(LANES)), axis=-1, keepdims=True)
        cur = jnp.where(lane_f == first, LOWEST, cur)
        if k == 0:
            top0 = mx
        e = jnp.exp(mx - top0)
        denom = e if k == 0 else denom + e
        idx_out = jnp.where(lane == k, first, idx_out)
        e_out = jnp.where(lane == k, e, e_out)
    idx_ref[...] = idx_out.astype(jnp.int32)
    gate_ref[...] = e_out / denom


def _router(x2d, sc, sh, gain, wr_pad, br_pad, T):
    N, D = x2d.shape
    nt = T // TM_ROUTE
    return pl.pallas_call(
        _router_kernel,
        grid=(N // TM_ROUTE,),
        in_specs=[pl.BlockSpec((TM_ROUTE, D), lambda i: (i, 0)),
                  pl.BlockSpec((1, 1, D), lambda i: (i // nt, 0, 0)),
                  pl.BlockSpec((1, 1, D), lambda i: (i // nt, 0, 0)),
                  pl.BlockSpec((1, D), lambda i: (0, 0)),
                  pl.BlockSpec((D, LANES), lambda i: (0, 0)),
                  pl.BlockSpec((1, LANES), lambda i: (0, 0))],
        out_specs=[pl.BlockSpec((TM_ROUTE, D), lambda i: (i, 0)),
                   pl.BlockSpec((TM_ROUTE, LANES), lambda i: (i, 0)),
                   pl.BlockSpec((TM_ROUTE, LANES), lambda i: (i, 0))],
        out_shape=[jax.ShapeDtypeStruct((N, D), F32),
                   jax.ShapeDtypeStruct((N, LANES), jnp.int32),
                   jax.ShapeDtypeStruct((N, LANES), F32)],
        compiler_params=_cparams(1),
        name="moe_router",
    )(x2d, sc, sh, gain, wr_pad, br_pad)


DMA_UNROLL = 8


def _row_gather(src_hbm, idx_ref, buf, sem, slot, n_rows, dst_row, wait):
    if wait:
        pltpu.make_async_copy(src_hbm.at[pl.ds(0, n_rows), :], buf.at[slot], sem.at[slot]).wait()
        return

    def body(a8, c):
        for u in range(DMA_UNROLL):
            a = a8 * DMA_UNROLL + u
            pltpu.make_async_copy(src_hbm.at[pl.ds(idx_ref[0, 0, a], 1), :],
                                  buf.at[slot, pl.ds(dst_row(a), 1), :], sem.at[slot]).start(priority=u % 2)
        return c

    lax.fori_loop(0, n_rows // DMA_UNROLL, body, 0)


def _expert_changed(be_ref, i):
    prev = be_ref[jnp.maximum(i - 1, 0)]
    return (i == 0) | (be_ref[i] != prev)


def _stage_expert_weights(be_ref, meta_ref, nxt_ref, w_hbm, stage, w_scr, sem, layer, windows, n_pass):
    j = pl.program_id(0)
    i = pl.program_id(1)
    e = be_ref[i]
    width = stage.shape[-1]

    def copies(ee, jj):
        return [pltpu.make_async_copy(w_hbm.at[layer, ee, :, pl.ds(pl.multiple_of(c0, LANES), width)],
                                      stage.at[p], sem.at[p])
                for p, c0 in enumerate(windows(jj))]

    @pl.when(_expert_changed(be_ref, i))
    def _():
        @pl.when((j == 0) & (i == 0))
        def _():
            for cp in copies(e, j):
                cp.start()

        for cp in copies(e, j):
            cp.wait()
        for p in range(stage.shape[0]):
            w_scr[p] = stage[p].astype(BF16)
        nxt = nxt_ref[e]
        nj = jnp.where(nxt >= 0, j, j + 1)
        ne = jnp.where(nxt >= 0, nxt, meta_ref[1])

        @pl.when(nj < n_pass)
        def _():
            for cp in copies(ne, nj):
                cp.start()


def _gmm1_kernel(be_ref, meta_ref, nxt_ref, tok_ref, tok_next_ref, h_hbm, w_hbm, bg_ref, bl_ref, o_ref,
                 xbuf, xsem, stage, w_scr, sem, *, layer, n_pass, d_ff):
    i = pl.program_id(1)
    n_used = meta_ref[0]
    n = xbuf.shape[1]
    slot = i % 2
    ident = lambda a: a

    @pl.when(i == 0)
    def _():
        _row_gather(h_hbm, tok_ref, xbuf, xsem, 0, n, ident, False)

    @pl.when(i + 1 < n_used)
    def _():
        _row_gather(h_hbm, tok_next_ref, xbuf, xsem, 1 - slot, n, ident, False)

    @pl.when(i < n_used)
    def _():
        _stage_expert_weights(be_ref, meta_ref, nxt_ref, w_hbm, stage, w_scr, sem, layer,
                              lambda jj: (jj * TF, d_ff + jj * TF), n_pass)
        _row_gather(h_hbm, tok_ref, xbuf, xsem, slot, n, ident, True)
        x = xbuf[slot].astype(BF16)
        glu = jnp.dot(x, w_scr[0], preferred_element_type=F32) + bg_ref[0, 0]
        lin = jnp.dot(x, w_scr[1], preferred_element_type=F32) + bl_ref[0, 0]
        glu = jnp.minimum(glu, SWIGLU_LIMIT)
        lin = jnp.clip(lin, -SWIGLU_LIMIT, SWIGLU_LIMIT)
        o_ref[...] = (glu * jax.nn.sigmoid(SWIGLU_ALPHA * glu) * (lin + 1.0)).astype(BF16)

    @pl.when(i >= meta_ref[0])
    def _():
        o_ref[...] = jnp.zeros(o_ref.shape, o_ref.dtype)


def _gmm2_kernel(be_ref, meta_ref, nxt_ref, a_ref, w_hbm, b_ref, o_ref, stage, w_scr, sem, *, layer, n_pass):
    i = pl.program_id(1)

    @pl.when(i < meta_ref[0])
    def _():
        _stage_expert_weights(be_ref, meta_ref, nxt_ref, w_hbm, stage, w_scr, sem, layer,
                              lambda jj: (jj * TN_MOE,), n_pass)
        o_ref[...] = jnp.dot(a_ref[...], w_scr[0], preferred_element_type=F32) + b_ref[0, 0]

    @pl.when(i >= meta_ref[0])
    def _():
        o_ref[...] = jnp.zeros(o_ref.shape, o_ref.dtype)


def _experts(layer, blk_expert, meta, nxt, buf_tok, h, w1, b1, w2, b2):
    N, D = h.shape
    L, E, _, F2 = w1.shape
    F = F2 // 2
    n_blk = buf_tok.shape[0]
    P = n_blk * MOE_ROWS
    nf = F // TF
    nd = D // TN_MOE
    b1r = b1.reshape(L, E, 1, F2)
    act = pl.pallas_call(
        functools.partial(_gmm1_kernel, layer=layer, n_pass=nf, d_ff=F),
        grid_spec=pltpu.PrefetchScalarGridSpec(
            num_scalar_prefetch=3,
            grid=(nf, n_blk),
            in_specs=[pl.BlockSpec((1, 1, MOE_ROWS), lambda j, i, be, mt, nx: (i, 0, 0),
                                   memory_space=pltpu.SMEM),
                      pl.BlockSpec((1, 1, MOE_ROWS), lambda j, i, be, mt, nx: (jnp.minimum(i + 1, n_blk - 1), 0, 0),
                                   memory_space=pltpu.SMEM),
                      pl.BlockSpec(memory_space=pl.ANY),
                      pl.BlockSpec(memory_space=pl.ANY),
                      pl.BlockSpec((1, 1, 1, TF), lambda j, i, be, mt, nx: (layer, be[i], 0, j)),
                      pl.BlockSpec((1, 1, 1, TF), lambda j, i, be, mt, nx: (layer, be[i], 0, nf + j))],
            out_specs=pl.BlockSpec((MOE_ROWS, TF), lambda j, i, be, mt, nx: (i, j)),
            scratch_shapes=[pltpu.VMEM((2, MOE_ROWS, D), F32), pltpu.SemaphoreType.DMA((2,)),
                            pltpu.VMEM((2, D, TF), F32), pltpu.VMEM((2, D, TF), BF16),
                            pltpu.SemaphoreType.DMA((2,))]),
        out_shape=jax.ShapeDtypeStruct((P, F), BF16),
        compiler_params=_cparams(2),
        name="moe_up",
    )(blk_expert, meta, nxt, buf_tok, buf_tok, h, w1, b1r, b1r)
    return pl.pallas_call(
        functools.partial(_gmm2_kernel, layer=layer, n_pass=nd),
        grid_spec=pltpu.PrefetchScalarGridSpec(
            num_scalar_prefetch=3,
            grid=(nd, n_blk),
            in_specs=[pl.BlockSpec((MOE_ROWS, F), lambda j, i, be, mt, nx: (i, 0)),
                      pl.BlockSpec(memory_space=pl.ANY),
                      pl.BlockSpec((1, 1, 1, TN_MOE), lambda j, i, be, mt, nx: (layer, be[i], 0, j))],
            out_specs=pl.BlockSpec((MOE_ROWS, TN_MOE), lambda j, i, be, mt, nx: (i, j)),
            scratch_shapes=[pltpu.VMEM((1, F, TN_MOE), F32), pltpu.VMEM((1, F, TN_MOE), BF16),
                            pltpu.SemaphoreType.DMA((1,))]),
        out_shape=jax.ShapeDtypeStruct((P, D), F32),
        compiler_params=_cparams(2),
        name="moe_down",
    )(blk_expert, meta, nxt, act, w2, b2.reshape(L, E, 1, D))


def _combine_kernel(pos_ref, nxt_ref, y_hbm, x_ref, gate_ref, g2_ref, nf_ref, o_ref, buf, sem, *, final):
    i = pl.program_id(0)
    n_steps = pl.num_programs(0)
    n_tok = x_ref.shape[0]
    n_rows = TOP_K * n_tok
    slot = i % 2
    by_choice = lambda a: (a % TOP_K) * n_tok + a // TOP_K

    @pl.when(i == 0)
    def _():
        _row_gather(y_hbm, pos_ref, buf, sem, 0, n_rows, by_choice, False)

    @pl.when(i + 1 < n_steps)
    def _():
        _row_gather(y_hbm, nxt_ref, buf, sem, 1 - slot, n_rows, by_choice, False)

    _row_gather(y_hbm, pos_ref, buf, sem, slot, n_rows, by_choice, True)
    gate = gate_ref[...]
    moe = gate[:, 0:1] * buf[slot, 0:n_tok, :]
    for k in range(1, TOP_K):
        moe = moe + gate[:, k:k + 1] * buf[slot, k * n_tok:(k + 1) * n_tok, :]
    x = x_ref[...] + g2_ref[0] * moe
    if final:
        ms = jnp.mean(x * x, axis=-1, keepdims=True)
        x = x * lax.rsqrt(ms + EPS) * nf_ref[...]
    o_ref[...] = x


def _combine(pos, y, x2d, gate, g2, norm_final, T, final):
    N, D = x2d.shape
    nt = T // TC
    n_steps = N // TC
    return pl.pallas_call(
        functools.partial(_combine_kernel, final=final),
        grid=(n_steps,),
        in_specs=[pl.BlockSpec((1, 1, TOP_K * TC), lambda i: (i, 0, 0), memory_space=pltpu.SMEM),
                  pl.BlockSpec((1, 1, TOP_K * TC), lambda i: (jnp.minimum(i + 1, n_steps - 1), 0, 0),
                               memory_space=pltpu.SMEM),
                  pl.BlockSpec(memory_space=pl.ANY),
                  pl.BlockSpec((TC, D), lambda i: (i, 0)),
                  pl.BlockSpec((TC, LANES), lambda i: (i, 0)),
                  pl.BlockSpec((1, 1, D), lambda i: (i // nt, 0, 0)),
                  pl.BlockSpec((1, D), lambda i: (0, 0))],
        out_specs=pl.BlockSpec((TC, D), lambda i: (i, 0)),
        out_shape=jax.ShapeDtypeStruct((N, D), F32),
        scratch_shapes=[pltpu.VMEM((2, TOP_K * TC, D), F32), pltpu.SemaphoreType.DMA((2,))],
        compiler_params=_cparams(1),
        name="moe_combine_final" if final else "moe_combine",
    )(pos, pos, y, x2d, gate, g2, norm_final)


def _routing_tables(top_idx):
    N = top_idx.shape[0]
    NK = N * TOP_K
    e_flat = top_idx.reshape(NK)
    onehot = (e_flat[:, None] == jnp.arange(N_EXPERTS, dtype=jnp.int32)[None, :]).astype(jnp.int32)
    csum = jnp.cumsum(onehot, axis=0)
    counts = csum[-1]
    rank = jnp.sum(csum * onehot, axis=1) - 1
    padded = (counts + MOE_ROWS - 1) // MOE_ROWS * MOE_ROWS
    pad_end = jnp.cumsum(padded)
    pad_start = pad_end - padded
    dest = jnp.sum(onehot * pad_start[None, :], axis=1) + rank
    P = NK + N_EXPERTS * MOE_ROWS
    n_blk = P // MOE_ROWS
    tok = jnp.arange(NK, dtype=jnp.int32) // TOP_K
    buf_tok = jnp.zeros((P,), jnp.int32).at[dest].set(tok)
    blk_start = jnp.arange(n_blk, dtype=jnp.int32) * MOE_ROWS
    blk_expert = jnp.minimum(jnp.sum(blk_start[:, None] >= pad_end[None, :], axis=-1), N_EXPERTS - 1)
    n_used = pad_end[-1] // MOE_ROWS
    ids = jnp.arange(N_EXPERTS, dtype=jnp.int32)
    at_or_after = lax.cummin(jnp.where(counts > 0, ids, N_EXPERTS), reverse=True)
    nxt = jnp.concatenate([at_or_after[1:], jnp.full((1,), N_EXPERTS, jnp.int32)])
    nxt = jnp.where(nxt < N_EXPERTS, nxt, -1)
    meta = jnp.stack([n_used, at_or_after[0]])
    return (dest.astype(jnp.int32), buf_tok.reshape(n_blk, 1, MOE_ROWS), blk_expert.astype(jnp.int32),
            meta.astype(jnp.int32), nxt.astype(jnp.int32))


def _rope_tables(T):
    inv = 1.0 / (ROPE_THETA ** (jnp.arange(0, HEAD_DIM, 2, dtype=F32) / HEAD_DIM))
    ang = jnp.arange(T, dtype=F32)[:, None] * inv[None, :]
    return jnp.cos(ang), jnp.sin(ang)


def kernel(x, c, w_mod, b_mod, norm_attn, norm_ffn, w_in, cmp_pe, cmp_w1, cmp_w2, mix_norm, w_out,
           w_router, b_router, w_exp1, b_exp1, w_exp2, b_exp2, norm_final):
    B, T, D = x.shape
    N = B * T
    G = B_KV_HEADS
    assert D == D_MODEL and T % TM_PROJ == 0 and T % TM_ROUTE == 0
    assert all(T % n == 0 for n in (TQ_A, TK_A, TQ_B, TK_B, TQ_C, TK_C)) and TK_B % TQ_B == 0 and TQ_C % TK_C == 0

    cos, sin = _rope_tables(T)
    cos128 = jnp.tile(cos, (1, LANES // (HEAD_DIM // 2)))
    sin128 = jnp.tile(jnp.concatenate([-sin, sin], axis=1), (1, LANES // HEAD_DIM))
    n_cmp = (T - CMP_BLOCK) // CMP_STRIDE + 1
    n_chunk = T // CMP_STRIDE
    assert n_chunk == n_cmp + 1
    cmp_end = np.arange(n_chunk) * CMP_STRIDE + CMP_BLOCK - 1
    cmp_end = np.minimum(cmp_end, T - 1)
    cosc = jnp.stack([jnp.tile(cos[cmp_end], (1, 4)), jnp.ones((n_chunk, LANES), F32)])
    sinc = jnp.stack([jnp.tile(sin[cmp_end], (1, 4)), jnp.zeros((n_chunk, LANES), F32)])
    tbl = jnp.asarray(_dilated_bias_table())
    ov = jnp.asarray(_overlap_table_t(T))

    mod = _modulation(c, w_mod, b_mod).reshape(DEPTH, B, 6, 1, D)
    w_in_re = _permute_cols(w_in).astype(BF16)
    w_out_bf = w_out.astype(BF16)
    wr_pad = jnp.zeros((DEPTH, D, LANES), F32).at[:, :, :N_EXPERTS].set(w_router)
    br_pad = jnp.zeros((DEPTH, 1, LANES), F32).at[:, 0, :N_EXPERTS].set(b_router)
    half = HEAD_DIM // 2
    w2d = jnp.concatenate([cmp_w2, cmp_w2], axis=-1)
    w2rot = jnp.concatenate([-cmp_w2[..., half:], cmp_w2[..., :half]], axis=-1)
    w2r = jnp.concatenate([w2rot, w2rot], axis=-1)
    pe = cmp_pe.reshape(DEPTH, 2, 2, CMP_STRIDE * HEAD_DIM)

    xf = x.reshape(N, D)
    for i in range(DEPTH):
        sh1, sc1, g1, sh2, sc2, g2 = [mod[i, :, k] for k in range(6)]
        proj = _in_projection(xf, sc1, sh1, norm_attn[i][None, :], w_in_re[i], cos128, sin128, T)
        proj3 = proj.reshape(B, T, N_COLS)
        o_a = _attention_a(proj3, tbl)
        kcvc = proj3[:, :, BLK_KCVC * LANES:(BLK_KCVC + 3) * LANES]
        chunks = kcvc.reshape(B, T, 2, G, HEAD_DIM).transpose(0, 2, 3, 1, 4).reshape(
            B, 2, G, n_chunk, CMP_STRIDE * HEAD_DIM)
        cmp_kv = _compress(chunks, pe[i], cmp_w1[i], w2d[i], w2r[i], cosc, sinc)
        o_b = _attention_b(proj3, cmp_kv, ov)
        o_c = _attention_c(proj3)
        xf = _out_projection(o_a.reshape(N, -1), o_b.reshape(N, -1), o_c.reshape(N, -1),
                             mix_norm[i][None, :], w_out_bf[i], xf, g1, T)
        h2, idx128, gate128 = _router(xf, sc2, sh2, norm_ffn[i][None, :], wr_pad[i], br_pad[i], T)
        dest, buf_tok, blk_expert, meta, nxt = _routing_tables(idx128[:, :TOP_K])
        y = _experts(i, blk_expert, meta, nxt, buf_tok, h2, w_exp1, b_exp1, w_exp2, b_exp2)
        xf = _combine(dest.reshape(N // TC, 1, TOP_K * TC), y, xf, gate128, g2,
                      norm_final[None, :], T, final=(i == DEPTH - 1))
    return xf.reshape(B, T, D)
```

```python
import functools

import numpy as np
import jax
import jax.numpy as jnp
from jax import lax
from jax.experimental import pallas as pl
from jax.experimental.pallas import tpu as pltpu

F32 = jnp.float32
BF16 = jnp.bfloat16
HIGHEST = lax.Precision.HIGHEST

D_MODEL = 2048
DEPTH = 2
HEAD_DIM = 64
A_HEADS = 12
A_PATTERNS = ((128, 1), (512, 4), (2048, 16))
B_HEADS = 12
B_KV_HEADS = 3
B_BRANCHES = 3
CMP_BLOCK = 32
CMP_STRIDE = 16
CMP_HIDDEN = 128
SLC_BLOCK = 64
SLC_TOPK = 16
WIN = 512
C_HEADS = 8
A_QKV_W = 3 * A_HEADS * HEAD_DIM
B_Q_W = B_HEADS * HEAD_DIM
B_KV_W = 2 * B_BRANCHES * B_KV_HEADS * HEAD_DIM
B_GATE_W = B_BRANCHES * B_HEADS
C_QKV_W = 3 * C_HEADS * HEAD_DIM
N_EXPERTS = 32
TOP_K = 4
D_FF = 2048
SWIGLU_LIMIT = 7.0
SWIGLU_ALPHA = 1.702
ROPE_THETA = 10000.0
EPS = 1e-6
NEG_BIG = -1e30
TINY = 1e-30
FORCE = 1e4
LOWEST = -3.0e38
RUN_DEAD = 110.0

LANES = 128
VMEM_LIMIT = 56 * 1024 * 1024

TQ_A, TK_A = 512, 512
TQ_B, TK_B = 256, 512
TQ_C, TK_C = 512, 256
TM_PROJ = 512
TN_IN = 768
TN_OUT = 512
TM_ROUTE = 256
MOE_ROWS = 256
TF = 1024
TN_MOE = 2048
TC = 128

BLK_AQ, BLK_AK, BLK_BQ, BLK_KS, BLK_KW = 0, 6, 12, 18, 21
N_ROPE_BLKS = 24
BLK_AV, BLK_KCVC, BLK_VS, BLK_VW = 24, 30, 33, 36
BLK_CQ, BLK_CK, BLK_CV = 39, 43, 47
BLK_GATE = 51
N_BLKS = 54
N_COLS = N_BLKS * LANES
assert N_ROPE_BLKS * LANES % TN_IN == 0 and N_COLS % TN_IN == 0


def _cparams(n_axes):
    return pltpu.CompilerParams(dimension_semantics=("arbitrary",) * n_axes,
                                vmem_limit_bytes=VMEM_LIMIT)


def _layout():
    a0 = 0
    bq0 = A_QKV_W
    bkv0 = bq0 + B_Q_W
    bg0 = bkv0 + B_KV_W
    c0 = bg0 + B_GATE_W
    scale = HEAD_DIM ** -0.5
    runs, colscale = [], []

    def add(start, stop, s, blk):
        assert len(colscale) == blk * LANES
        runs.append((start, stop))
        colscale.extend([s] * (stop - start))

    def add_dup(arr, blk):
        for g in range(B_KV_HEADS):
            s0 = bkv0 + arr * gw + g * HEAD_DIM
            add(s0, s0 + HEAD_DIM, 1.0, blk + g)
            runs.append((s0, s0 + HEAD_DIM))
            colscale.extend([1.0] * HEAD_DIM)

    hw = A_HEADS * HEAD_DIM
    gw = B_KV_HEADS * HEAD_DIM
    cw = C_HEADS * HEAD_DIM
    add(a0, a0 + hw, scale, BLK_AQ)
    add(a0 + hw, a0 + 2 * hw, 1.0, BLK_AK)
    add(bq0, bq0 + B_Q_W, scale, BLK_BQ)
    add_dup(2, BLK_KS)
    add_dup(4, BLK_KW)
    add(a0 + 2 * hw, a0 + 3 * hw, 1.0, BLK_AV)
    add(bkv0, bkv0 + 2 * gw, 1.0, BLK_KCVC)
    add_dup(3, BLK_VS)
    add_dup(5, BLK_VW)
    add(c0, c0 + cw, scale, BLK_CQ)
    add(c0 + cw, c0 + 3 * cw, 1.0, BLK_CK)
    per_g = B_GATE_W // B_KV_HEADS
    for g in range(B_KV_HEADS):
        add(bg0 + g * per_g, bg0 + (g + 1) * per_g, 1.0, BLK_GATE + g)
        runs.append((None, LANES - per_g))
        colscale.extend([1.0] * (LANES - per_g))
    assert len(colscale) == N_COLS
    return runs, np.asarray(colscale, np.float32)[None, :]


_RUNS, _COL_SCALE = _layout()


def _permute_cols(w):
    parts = []
    for start, stop in _RUNS:
        if start is None:
            parts.append(jnp.zeros(w.shape[:-1] + (stop,), w.dtype))
        else:
            parts.append(w[..., start:stop])
    return jnp.concatenate(parts, axis=-1)


def _dilated_bias_table():
    assert TQ_A == TK_A
    n = -(-A_PATTERNS[-1][0] // TK_A) + 1
    d = (np.arange(n)[:, None, None] * TK_A + np.arange(TQ_A)[None, :, None] - np.arange(TK_A)[None, None, :])
    m = np.zeros(d.shape, np.float64)
    for window, dil in A_PATTERNS:
        m += ((d >= 0) & (d <= window) & (d % dil == 0))
    return np.where(m > 0, np.log(np.maximum(m, 1.0)), NEG_BIG).astype(np.float32)


def _overlap_table_t(T):
    n_cmp = (T - CMP_BLOCK) // CMP_STRIDE + 1
    n_slc = T // SLC_BLOCK
    cs = np.arange(n_cmp) * CMP_STRIDE
    ss = np.arange(n_slc) * SLC_BLOCK
    ov = (cs[None, :] < ss[:, None] + SLC_BLOCK) & (cs[None, :] + CMP_BLOCK > ss[:, None])
    out = np.zeros((n_slc, n_cmp + 1), np.float32)
    out[:n_slc, :n_cmp] = ov
    return out


def _mod_kernel(c_ref, w_ref, b_ref, o_ref):
    c = c_ref[...]
    ca = (c * jax.nn.sigmoid(c)).astype(BF16)
    y = jnp.dot(ca, w_ref[0].astype(BF16), preferred_element_type=F32)
    o_ref[0] = y + b_ref[0]


def _modulation(c, w_mod, b_mod):
    B, D = c.shape
    L, _, W = w_mod.shape
    tn = 1024
    rows = 8
    cp = jnp.zeros((rows, D), F32).at[:B].set(c)
    out = pl.pallas_call(
        _mod_kernel,
        grid=(L, W // tn),
        in_specs=[pl.BlockSpec((rows, D), lambda l, j: (0, 0)),
                  pl.BlockSpec((1, D, tn), lambda l, j: (l, 0, j)),
                  pl.BlockSpec((1, 1, tn), lambda l, j: (l, 0, j))],
        out_specs=pl.BlockSpec((1, rows, tn), lambda l, j: (l, 0, j)),
        out_shape=jax.ShapeDtypeStruct((L, rows, W), F32),
        compiler_params=_cparams(2),
        name="adaln_mod",
    )(cp, w_mod, b_mod.reshape(L, 1, W))
    return out[:, :B]


def _norm_mod(x, gain, sc, sh):
    ms = jnp.mean(x * x, axis=-1, keepdims=True)
    y = x * lax.rsqrt(ms + EPS) * gain
    return y * (1.0 + sc) + sh


def _rope128(y, cos, sin_signed):
    lane = lax.broadcasted_iota(jnp.int32, y.shape, 1)
    first_half = (lane % HEAD_DIM) < (HEAD_DIM // 2)
    rot = jnp.where(first_half, pltpu.roll(y, LANES - HEAD_DIM // 2, 1), pltpu.roll(y, HEAD_DIM // 2, 1))
    return y * cos + rot * sin_signed


def _inproj_kernel(x_ref, sc_ref, sh_ref, g_ref, w_ref, cs_ref, cos_ref, sin_ref, o_ref, h_scr):
    @pl.when(pl.program_id(1) == 0)
    def _():
        h_scr[...] = _norm_mod(x_ref[...], g_ref[...], sc_ref[0], sh_ref[0]).astype(BF16)

    y = jnp.dot(h_scr[...], w_ref[...], preferred_element_type=F32)
    n_rope_tiles = N_ROPE_BLKS * LANES // TN_IN

    @pl.when(pl.program_id(1) < n_rope_tiles)
    def _():
        cos = cos_ref[...]
        sin = sin_ref[...]
        for c in range(TN_IN // LANES):
            sl = slice(c * LANES, (c + 1) * LANES)
            o_ref[:, sl] = _rope128(y[:, sl], cos, sin) * cs_ref[:, sl]

    @pl.when(pl.program_id(1) >= n_rope_tiles)
    def _():
        o_ref[...] = y * cs_ref[...]


def _in_projection(x2d, sc, sh, gain, w_re, cos128, sin128, T):
    N, D = x2d.shape
    nt = T // TM_PROJ
    return pl.pallas_call(
        _inproj_kernel,
        grid=(N // TM_PROJ, N_COLS // TN_IN),
        in_specs=[pl.BlockSpec((TM_PROJ, D), lambda i, j: (i, 0)),
                  pl.BlockSpec((1, 1, D), lambda i, j: (i // nt, 0, 0)),
                  pl.BlockSpec((1, 1, D), lambda i, j: (i // nt, 0, 0)),
                  pl.BlockSpec((1, D), lambda i, j: (0, 0)),
                  pl.BlockSpec((D, TN_IN), lambda i, j: (0, j)),
                  pl.BlockSpec((1, TN_IN), lambda i, j: (0, j)),
                  pl.BlockSpec((TM_PROJ, LANES), lambda i, j: (i % nt, 0)),
                  pl.BlockSpec((TM_PROJ, LANES), lambda i, j: (i % nt, 0))],
        out_specs=pl.BlockSpec((TM_PROJ, TN_IN), lambda i, j: (i, j)),
        out_shape=jax.ShapeDtypeStruct((N, N_COLS), F32),
        scratch_shapes=[pltpu.VMEM((TM_PROJ, D), BF16)],
        compiler_params=_cparams(2),
        name="in_proj",
    )(x2d, sc, sh, gain, w_re, jnp.asarray(_COL_SCALE), cos128, sin128)


def _split_heads_q(q):
    lane = lax.broadcasted_iota(jnp.int32, q.shape, 1)
    lo = lane < HEAD_DIM
    return jnp.concatenate([jnp.where(lo, q, 0.0), jnp.where(lo, 0.0, q)], axis=0)


def _scores(q2, k):
    return lax.dot_general(q2, k, (((1,), (1,)), ((), ())), preferred_element_type=F32)


def _flash_step(s, m, acc, v_ones):
    m_new = jnp.maximum(m, jnp.max(s, axis=-1, keepdims=True))
    p = jnp.exp(s - m_new).astype(BF16)
    acc = jnp.exp(m - m_new) * acc + jnp.dot(p, v_ones, preferred_element_type=F32)
    return m_new, acc


def _flash_init(rows):
    return jnp.full((rows, 1), NEG_BIG, F32), jnp.zeros((rows, LANES), F32)


def _flash_finish(acc):
    return acc / jnp.maximum(pltpu.roll(acc, HEAD_DIM, 1), TINY)


def _high_half(x):
    bits = lax.bitcast_convert_type(x, jnp.int32)
    return lax.bitcast_convert_type(bits & jnp.int32(-65536), F32)


def _split_bf16(x):
    hi = x.astype(BF16)
    return hi, (x - hi.astype(F32)).astype(BF16)


def _tile(ref, kt, n):
    return ref[0, pl.ds(pl.multiple_of(kt * n, n), n), :]


def _two_head_norm(out):
    lane = lax.broadcasted_iota(jnp.int32, out.shape, 1)
    lo = lane < HEAD_DIM
    sq = out * out
    ms0 = jnp.sum(jnp.where(lo, sq, 0.0), axis=-1, keepdims=True) * (1.0 / HEAD_DIM)
    ms1 = jnp.sum(jnp.where(lo, 0.0, sq), axis=-1, keepdims=True) * (1.0 / HEAD_DIM)
    return out * jnp.where(lo, lax.rsqrt(ms0 + EPS), lax.rsqrt(ms1 + EPS))


def _attn_a_kernel(q_ref, k_ref, v_ref, tbl_ref, o_ref):
    qi = pl.program_id(2)
    q = q_ref[0]
    lane = lax.broadcasted_iota(jnp.int32, q.shape, 1)
    lo = lane < HEAD_DIM
    q0 = jnp.where(lo, q, 0.0).astype(BF16)
    q1 = jnp.where(lo, 0.0, q).astype(BF16)
    n_d = tbl_ref.shape[0]

    def body(i, carry):
        m0, acc0, m1, acc1 = carry
        k = _tile(k_ref, qi - i, TK_A).astype(BF16)
        v = _tile(v_ref, qi - i, TK_A)
        lo_k = lax.broadcasted_iota(jnp.int32, v.shape, 1) < HEAD_DIM
        bias = tbl_ref[i]
        m0, acc0 = _flash_step(_scores(q0, k) + bias, m0, acc0, jnp.where(lo_k, v, 1.0).astype(BF16))
        m1, acc1 = _flash_step(_scores(q1, k) + bias, m1, acc1, jnp.where(lo_k, 1.0, v).astype(BF16))
        return m0, acc0, m1, acc1

    init = _flash_init(TQ_A) + _flash_init(TQ_A)
    _, acc0, _, acc1 = lax.fori_loop(0, jnp.minimum(qi, n_d - 1) + 1, body, init)
    o_ref[0] = _two_head_norm(jnp.where(lo, _flash_finish(acc0), _flash_finish(acc1)))


def _attention_a(proj, tbl):
    B, T, _ = proj.shape
    n_pairs = A_HEADS // 2
    return pl.pallas_call(
        _attn_a_kernel,
        grid=(B, n_pairs, T // TQ_A),
        in_specs=[pl.BlockSpec((1, TQ_A, LANES), lambda b, p, i: (b, i, BLK_AQ + p)),
                  pl.BlockSpec((1, T, LANES), lambda b, p, i: (b, 0, BLK_AK + p)),
                  pl.BlockSpec((1, T, LANES), lambda b, p, i: (b, 0, BLK_AV + p)),
                  pl.BlockSpec(tbl.shape, lambda b, p, i: (0, 0, 0))],
        out_specs=pl.BlockSpec((1, TQ_A, LANES), lambda b, p, i: (b, i, p)),
        out_shape=jax.ShapeDtypeStruct((B, T, n_pairs * LANES), F32),
        compiler_params=_cparams(3),
        name="attn_dilated",
    )(proj, proj, proj, tbl)


def _attn_c_kernel(q_ref, k_ref, v_ref, o_ref):
    qi = pl.program_id(2)
    q = q_ref[0]
    lo = lax.broadcasted_iota(jnp.int32, q.shape, 1) < HEAD_DIM
    qs = (jnp.where(lo, q, 0.0).astype(BF16), jnp.where(lo, 0.0, q).astype(BF16))
    diff = (lax.broadcasted_iota(jnp.int32, (TQ_C, TK_C), 1) - lax.broadcasted_iota(jnp.int32, (TQ_C, TK_C), 0))
    uj = lax.broadcasted_iota(jnp.int32, (TK_C, TK_C), 0)
    us = lax.broadcasted_iota(jnp.int32, (TK_C, TK_C), 1)
    upper = (uj >= us).astype(BF16)
    tiles_per_q = TQ_C // TK_C

    def tile(kt, carry, masked):
        k = _tile(k_ref, kt, TK_C).astype(BF16)
        v = _tile(v_ref, kt, TK_C).astype(BF16)
        if masked:
            before = diff < qi * TQ_C - kt * TK_C
        out = []
        for qh, (run, acc) in zip(qs, carry):
            z = _scores(qh, k)
            sp = jnp.maximum(z, 0.0) + jnp.log(1.0 + jnp.exp(-jnp.abs(z)))
            if masked:
                sp = jnp.where(before, sp, 0.0)
            hi = _high_half(sp)
            suffix = (jnp.dot(hi.astype(BF16), upper, preferred_element_type=F32)
                      + jnp.dot((sp - hi).astype(BF16), upper, preferred_element_type=F32))
            a = jnp.exp(z - suffix - run)
            if masked:
                a = jnp.where(before, a, 0.0)
            out.append((run + suffix[:, 0:1], acc + jnp.dot(a.astype(BF16), v, preferred_element_type=F32)))
        return tuple(out)

    carry = ((jnp.zeros((TQ_C, 1), F32), jnp.zeros((TQ_C, LANES), F32)),) * 2
    last = (qi + 1) * tiles_per_q - 1
    for j in range(tiles_per_q):
        carry = tile(last - j, carry, True)
    def rest_is_zero(c):
        return (jnp.minimum(jnp.min(c[0][0]), jnp.min(c[1][0])) >= RUN_DEAD).astype(jnp.int32)

    def more(state):
        i, dead, _ = state
        return (i <= last) & (dead == 0)

    def step(state):
        i, _, c = state
        c = tile(last - i, c, False)
        return i + 1, rest_is_zero(c), c

    _, _, carry = lax.while_loop(more, step, (jnp.int32(tiles_per_q), rest_is_zero(carry), carry))
    o_ref[0] = _two_head_norm(jnp.where(lo, carry[0][1], carry[1][1]))


def _attention_c(proj):
    B, T, _ = proj.shape
    n_pairs = C_HEADS // 2
    return pl.pallas_call(
        _attn_c_kernel,
        grid=(B, n_pairs, T // TQ_C),
        in_specs=[pl.BlockSpec((1, TQ_C, LANES), lambda b, p, i: (b, i, BLK_CQ + p)),
                  pl.BlockSpec((1, T, LANES), lambda b, p, i: (b, 0, BLK_CK + p)),
                  pl.BlockSpec((1, T, LANES), lambda b, p, i: (b, 0, BLK_CV + p))],
        out_specs=pl.BlockSpec((1, TQ_C, LANES), lambda b, p, i: (b, i, p)),
        out_shape=jax.ShapeDtypeStruct((B, T, n_pairs * LANES), F32),
        compiler_params=_cparams(3),
        name="attn_stickbreak",
    )(proj, proj, proj)


def _compress_kernel(x_ref, pe_ref, w1_ref, w2_ref, w2r_ref, cos_ref, sin_ref, o_ref):
    x = x_ref[0, 0, 0]
    half = x.shape[1]
    ha = jnp.dot(x + pe_ref[0, 0:1, :], w1_ref[0, :half, :], precision=HIGHEST, preferred_element_type=F32)
    hb = jnp.dot(x + pe_ref[0, 1:2, :], w1_ref[0, half:, :], precision=HIGHEST, preferred_element_type=F32)
    n = x.shape[0]
    h = ha + pltpu.roll(hb, n - 1, 0)
    g = jax.nn.gelu(h)
    y = jnp.dot(g, w2_ref[0], precision=HIGHEST, preferred_element_type=F32)
    yr = jnp.dot(g, w2r_ref[0], precision=HIGHEST, preferred_element_type=F32)
    out = y * cos_ref[0] + yr * sin_ref[0]
    rowi = lax.broadcasted_iota(jnp.int32, out.shape, 0)
    o_ref[0, 0, 0] = jnp.where(rowi < n - 1, out, 0.0)


def _compress(chunks, pe, w1, w2d, w2r, cosc, sinc):
    B, _, G, n, cw = chunks.shape
    return pl.pallas_call(
        _compress_kernel,
        grid=(B, 2, G),
        in_specs=[pl.BlockSpec((1, 1, 1, n, cw), lambda b, w, g: (b, w, g, 0, 0)),
                  pl.BlockSpec((1, 2, cw), lambda b, w, g: (w, 0, 0)),
                  pl.BlockSpec((1, 2 * cw, CMP_HIDDEN), lambda b, w, g: (w, 0, 0)),
                  pl.BlockSpec((1, CMP_HIDDEN, LANES), lambda b, w, g: (w, 0, 0)),
                  pl.BlockSpec((1, CMP_HIDDEN, LANES), lambda b, w, g: (w, 0, 0)),
                  pl.BlockSpec((1, n, LANES), lambda b, w, g: (w, 0, 0)),
                  pl.BlockSpec((1, n, LANES), lambda b, w, g: (w, 0, 0))],
        out_specs=pl.BlockSpec((1, 1, 1, n, LANES), lambda b, w, g: (b, w, g, 0, 0)),
        out_shape=jax.ShapeDtypeStruct((B, 2, G, n, LANES), F32),
        compiler_params=_cparams(3),
        name="nsa_compress",
    )(chunks, pe, w1, w2d, w2r, cosc, sinc)


def _nsa_kernel(q_ref, cmp_k_ref, cmp_v_ref, ks_ref, vs_ref, kw_ref, vw_ref, gate_ref, ovt_ref, o_ref):
    qi = pl.program_id(2)
    R = B_HEADS // B_KV_HEADS
    TQ, TK = TQ_B, TK_B
    rows = R * TQ
    q0 = qi * TQ
    q = q_ref[0]
    qf = jnp.concatenate([_split_heads_q(q[:, :LANES]), _split_heads_q(q[:, LANES:])], axis=0)
    q4 = qf.astype(BF16)

    kc = cmp_k_ref[0, 0, 0]
    vc = cmp_v_ref[0, 0, 0].astype(BF16)
    n_c = kc.shape[0]
    last_end = (n_c - 1) * CMP_STRIDE + CMP_BLOCK - 1
    t_col = q0 + lax.broadcasted_iota(jnp.int32, (TQ, 1), 0)
    cmp_end = lax.broadcasted_iota(jnp.int32, (TQ, n_c), 1) * CMP_STRIDE + (CMP_BLOCK - 1)
    valid = (cmp_end <= t_col) & (cmp_end < last_end)
    t_lane = q0 + lax.broadcasted_iota(jnp.int32, (1, TQ), 1)
    nt_dims = (((1,), (1,)), ((), ()))
    p_sum = jnp.zeros((TQ, n_c), F32)
    o_cmp = []
    kc_hi, kc_lo = _split_bf16(kc)
    for r in range(R):
        q_hi, q_lo = _split_bf16(qf[r * TQ:(r + 1) * TQ])
        s = _scores(q_hi, kc_hi) + (_scores(q_hi, kc_lo) + _scores(q_lo, kc_hi))
        s = jnp.where(valid, s, NEG_BIG)
        p = jnp.exp(s - jnp.max(s, axis=-1, keepdims=True)) * valid.astype(F32)
        p = p / jnp.maximum(jnp.sum(p, axis=-1, keepdims=True), TINY)
        o_cmp.append(jnp.dot(p.astype(BF16), vc, preferred_element_type=F32))
        p_sum = p_sum + p
    imp = lax.dot_general(ovt_ref[...], p_sum, nt_dims, precision=HIGHEST, preferred_element_type=F32)
    n_slc = imp.shape[0]
    blk = lax.broadcasted_iota(jnp.int32, (n_slc, TQ), 0)
    blk_f = blk.astype(F32)
    tb = t_lane // SLC_BLOCK
    forced = (blk == 0) | (blk == tb) | (blk == tb - 1)
    imp = jnp.where(forced, imp + FORCE, imp)
    imp = jnp.where(blk > tb, -FORCE, imp)
    sel = jnp.zeros((n_slc, TQ), F32)
    for _ in range(min(SLC_TOPK, n_slc)):
        mx = jnp.max(imp, axis=0, keepdims=True)
        first = jnp.min(jnp.where(imp == mx, blk_f, float(n_slc)), axis=0, keepdims=True)
        pick = blk_f == first
        sel = jnp.where(pick, 1.0, sel)
        imp = jnp.where(pick, LOWEST, imp)
    sel_bias = jnp.concatenate([(sel - 1.0) * (-NEG_BIG), jnp.zeros((LANES - n_slc, TQ), F32)], axis=0)
    sel_bias = sel_bias.T.astype(BF16)

    diff = lax.broadcasted_iota(jnp.int32, (TQ, TK), 1) - lax.broadcasted_iota(jnp.int32, (TQ, TK), 0)
    e_row = lax.broadcasted_iota(jnp.int32, (LANES, TK), 0)
    e_col = lax.broadcasted_iota(jnp.int32, (LANES, TK), 1) // SLC_BLOCK
    lo_k = lax.broadcasted_iota(jnp.int32, (TK, LANES), 1) < HEAD_DIM
    last = q0 // TK

    def step(k_ref, v_ref, kt, bias, carry):
        k = _tile(k_ref, kt, TK).astype(BF16)
        v_ones = jnp.where(lo_k, _tile(v_ref, kt, TK), 1.0).astype(BF16)
        s = (_scores(q4, k).reshape(R, TQ, TK) + bias[None]).reshape(rows, TK)
        return _flash_step(s, *carry, v_ones)

    def slc_tile(kt, carry, causal):
        expand = (e_row == kt * (TK // SLC_BLOCK) + e_col).astype(BF16)
        bias = jnp.dot(sel_bias, expand, preferred_element_type=F32)
        if causal:
            bias = bias + jnp.where(diff <= q0 - kt * TK, 0.0, NEG_BIG)
        return step(ks_ref, vs_ref, kt, bias, carry)

    carry = lax.fori_loop(0, last, lambda kt, c: slc_tile(kt, c, False), _flash_init(rows))
    o_slc = _flash_finish(slc_tile(last, carry, True)[1])

    def win_tile(i, carry):
        kt = last - i
        off = q0 - kt * TK
        bias = jnp.where((diff <= off) & (diff > off - WIN), 0.0, NEG_BIG)
        return step(kw_ref, vw_ref, kt, bias, carry)

    first_win = jnp.maximum(q0 - (WIN - 1), 0) // TK
    o_win = _flash_finish(lax.fori_loop(0, last - first_win + 1, win_tile, _flash_init(rows))[1])

    gates = jax.nn.sigmoid(gate_ref[0])
    lo = lax.broadcasted_iota(jnp.int32, (TQ, LANES), 1) < HEAD_DIM
    outs = []
    for r in range(R):
        rs = slice(r * TQ, (r + 1) * TQ)
        o = (gates[:, 3 * r:3 * r + 1] * o_cmp[r] + gates[:, 3 * r + 1:3 * r + 2] * o_slc[rs]
             + gates[:, 3 * r + 2:3 * r + 3] * o_win[rs])
        ms = jnp.sum(jnp.where(lo, o * o, 0.0), axis=-1, keepdims=True) * (1.0 / HEAD_DIM)
        outs.append(o * lax.rsqrt(ms + EPS))
    pairs = [jnp.where(lo, outs[2 * j], pltpu.roll(outs[2 * j + 1], HEAD_DIM, 1)) for j in range(R // 2)]
    o_ref[0] = jnp.concatenate(pairs, axis=1)


def _attention_b(proj, cmp_kv, ov):
    B, T, _ = proj.shape
    G = B_KV_HEADS
    n_c = cmp_kv.shape[3]
    qw = 2 * LANES
    TQ = TQ_B
    return pl.pallas_call(
        _nsa_kernel,
        grid=(B, G, T // TQ),
        in_specs=[pl.BlockSpec((1, TQ, qw), lambda b, g, i: (b, i, BLK_BQ // 2 + g)),
                  pl.BlockSpec((1, 1, 1, n_c, LANES), lambda b, g, i: (b, 0, g, 0, 0)),
                  pl.BlockSpec((1, 1, 1, n_c, LANES), lambda b, g, i: (b, 1, g, 0, 0)),
                  pl.BlockSpec((1, T, LANES), lambda b, g, i: (b, 0, BLK_KS + g)),
                  pl.BlockSpec((1, T, LANES), lambda b, g, i: (b, 0, BLK_VS + g)),
                  pl.BlockSpec((1, T, LANES), lambda b, g, i: (b, 0, BLK_KW + g)),
                  pl.BlockSpec((1, T, LANES), lambda b, g, i: (b, 0, BLK_VW + g)),
                  pl.BlockSpec((1, TQ, LANES), lambda b, g, i: (b, i, BLK_GATE + g)),
                  pl.BlockSpec(ov.shape, lambda b, g, i: (0, 0))],
        out_specs=pl.BlockSpec((1, TQ, qw), lambda b, g, i: (b, i, g)),
        out_shape=jax.ShapeDtypeStruct((B, T, B_HEADS * HEAD_DIM), F32),
        compiler_params=_cparams(3),
        name="attn_nsa",
    )(proj, cmp_kv, cmp_kv, proj, proj, proj, proj, proj, ov)


def _outproj_kernel(oa_ref, ob_ref, oc_ref, mn_ref, w_ref, x_ref, g_ref, o_ref, h_scr):
    wa = oa_ref.shape[1]
    wb = ob_ref.shape[1]

    @pl.when(pl.program_id(1) == 0)
    def _():
        h_scr[:, :wa] = (oa_ref[...] * mn_ref[:, :wa]).astype(BF16)
        h_scr[:, wa:wa + wb] = (ob_ref[...] * mn_ref[:, wa:wa + wb]).astype(BF16)
        h_scr[:, wa + wb:] = (oc_ref[...] * mn_ref[:, wa + wb:]).astype(BF16)

    y = jnp.dot(h_scr[...], w_ref[...], preferred_element_type=F32)
    o_ref[...] = x_ref[...] + g_ref[0] * y


def _out_projection(oa, ob, oc, mix_norm, w_out_bf, x2d, g1, T):
    N, D = x2d.shape
    nt = T // TM_PROJ
    wa, wb, wc = oa.shape[1], ob.shape[1], oc.shape[1]
    return pl.pallas_call(
        _outproj_kernel,
        grid=(N // TM_PROJ, D // TN_OUT),
        in_specs=[pl.BlockSpec((TM_PROJ, wa), lambda i, j: (i, 0)),
                  pl.BlockSpec((TM_PROJ, wb), lambda i, j: (i, 0)),
                  pl.BlockSpec((TM_PROJ, wc), lambda i, j: (i, 0)),
                  pl.BlockSpec((1, D), lambda i, j: (0, 0)),
                  pl.BlockSpec((D, TN_OUT), lambda i, j: (0, j)),
                  pl.BlockSpec((TM_PROJ, TN_OUT), lambda i, j: (i, j)),
                  pl.BlockSpec((1, 1, TN_OUT), lambda i, j: (i // nt, 0, j))],
        out_specs=pl.BlockSpec((TM_PROJ, TN_OUT), lambda i, j: (i, j)),
        out_shape=jax.ShapeDtypeStruct((N, D), F32),
        scratch_shapes=[pltpu.VMEM((TM_PROJ, D), BF16)],
        compiler_params=_cparams(2),
        name="out_proj",
    )(oa, ob, oc, mix_norm, w_out_bf, x2d, g1)


def _router_kernel(x_ref, sc_ref, sh_ref, g_ref, wr_ref, br_ref, h_ref, idx_ref, gate_ref):
    h = _norm_mod(x_ref[...], g_ref[...], sc_ref[0], sh_ref[0])
    h_ref[...] = h
    logits = jnp.dot(h, wr_ref[...], precision=HIGHEST, preferred_element_type=F32) + br_ref[...]
    lane = lax.broadcasted_iota(jnp.int32, logits.shape, 1)
    lane_f = lane.astype(F32)
    cur = jnp.where(lane < N_EXPERTS, logits, LOWEST)
    idx_out = jnp.zeros(logits.shape, F32)
    e_out = jnp.zeros(logits.shape, F32)
    top0 = None
    denom = None
    for k in range(TOP_K):
        mx = jnp.max(cur, axis=-1, keepdims=True)
        first = jnp.min(jnp.where(cur == mx, lane_f, float(LANES)), axis=-1, keepdims=True)
        cur = jnp.where(lane_f == first, LOWEST, cur)
        if k == 0:
            top0 = mx
        e = jnp.exp(mx - top0)
        denom = e if k == 0 else denom + e
        idx_out = jnp.where(lane == k, first, idx_out)
        e_out = jnp.where(lane == k, e, e_out)
    idx_ref[...] = idx_out.astype(jnp.int32)
    gate_ref[...] = e_out / denom


def _router(x2d, sc, sh, gain, wr_pad, br_pad, T):
    N, D = x2d.shape
    nt = T // TM_ROUTE
    return pl.pallas_call(
        _router_kernel,
        grid=(N // TM_ROUTE,),
        in_specs=[pl.BlockSpec((TM_ROUTE, D), lambda i: (i, 0)),
                  pl.BlockSpec((1, 1, D), lambda i: (i // nt, 0, 0)),
                  pl.BlockSpec((1, 1, D), lambda i: (i // nt, 0, 0)),
                  pl.BlockSpec((1, D), lambda i: (0, 0)),
                  pl.BlockSpec((D, LANES), lambda i: (0, 0)),
                  pl.BlockSpec((1, LANES), lambda i: (0, 0))],
        out_specs=[pl.BlockSpec((TM_ROUTE, D), lambda i: (i, 0)),
                   pl.BlockSpec((TM_ROUTE, LANES), lambda i: (i, 0)),
                   pl.BlockSpec((TM_ROUTE, LANES), lambda i: (i, 0))],
        out_shape=[jax.ShapeDtypeStruct((N, D), F32),
                   jax.ShapeDtypeStruct((N, LANES), jnp.int32),
                   jax.ShapeDtypeStruct((N, LANES), F32)],
        compiler_params=_cparams(1),
        name="moe_router",
    )(x2d, sc, sh, gain, wr_pad, br_pad)


DMA_UNROLL = 8


def _row_gather(src_hbm, idx_ref, buf, sem, slot, n_rows, dst_row, wait):
    if wait:
        pltpu.make_async_copy(src_hbm.at[pl.ds(0, n_rows), :], buf.at[slot], sem.at[slot]).wait()
        return

    def body(a8, c):
        for u in range(DMA_UNROLL):
            a = a8 * DMA_UNROLL + u
            pltpu.make_async_copy(src_hbm.at[pl.ds(idx_ref[0, 0, a], 1), :],
                                  buf.at[slot, pl.ds(dst_row(a), 1), :], sem.at[slot]).start(priority=u % 2)
        return c

    lax.fori_loop(0, n_rows // DMA_UNROLL, body, 0)


def _expert_changed(be_ref, i):
    prev = be_ref[jnp.maximum(i - 1, 0)]
    return (i == 0) | (be_ref[i] != prev)


def _stage_expert_weights(be_ref, meta_ref, nxt_ref, w_hbm, stage, w_scr, sem, layer, windows, n_pass):
    j = pl.program_id(0)
    i = pl.program_id(1)
    e = be_ref[i]
    width = stage.shape[-1]

    def copies(ee, jj):
        return [pltpu.make_async_copy(w_hbm.at[layer, ee, :, pl.ds(pl.multiple_of(c0, LANES), width)],
                                      stage.at[p], sem.at[p])
                for p, c0 in enumerate(windows(jj))]

    @pl.when(_expert_changed(be_ref, i))
    def _():
        @pl.when((j == 0) & (i == 0))
        def _():
            for cp in copies(e, j):
                cp.start()

        for cp in copies(e, j):
            cp.wait()
        for p in range(stage.shape[0]):
            w_scr[p] = stage[p].astype(BF16)
        nxt = nxt_ref[e]
        nj = jnp.where(nxt >= 0, j, j + 1)
        ne = jnp.where(nxt >= 0, nxt, meta_ref[1])

        @pl.when(nj < n_pass)
        def _():
            for cp in copies(ne, nj):
                cp.start()


def _gather_kernel(nu_ref, tok_ref, nxt_ref, h_hbm, o_ref, buf, sem):
    i = pl.program_id(0)
    n_used = nu_ref[0]
    n = buf.shape[1]
    slot = i % 2
    ident = lambda a: a

    @pl.when(i == 0)
    def _():
        _row_gather(h_hbm, tok_ref, buf, sem, 0, n, ident, False)

    @pl.when(i + 1 < n_used)
    def _():
        _row_gather(h_hbm, nxt_ref, buf, sem, 1 - slot, n, ident, False)

    @pl.when(i < n_used)
    def _():
        _row_gather(h_hbm, tok_ref, buf, sem, slot, n, ident, True)
        o_ref[...] = buf[slot].astype(BF16)

    @pl.when(i >= n_used)
    def _():
        o_ref[...] = jnp.zeros(o_ref.shape, o_ref.dtype)


def _gather_rows(n_used, buf_tok, h):
    N, D = h.shape
    n_blk = buf_tok.shape[0]
    return pl.pallas_call(
        _gather_kernel,
        grid_spec=pltpu.PrefetchScalarGridSpec(
            num_scalar_prefetch=1,
            grid=(n_blk,),
            in_specs=[pl.BlockSpec((1, 1, MOE_ROWS), lambda i, nu: (i, 0, 0), memory_space=pltpu.SMEM),
                      pl.BlockSpec((1, 1, MOE_ROWS), lambda i, nu: (jnp.minimum(i + 1, n_blk - 1), 0, 0),
                                   memory_space=pltpu.SMEM),
                      pl.BlockSpec(memory_space=pl.ANY)],
            out_specs=pl.BlockSpec((MOE_ROWS, D), lambda i, nu: (i, 0)),
            scratch_shapes=[pltpu.VMEM((2, MOE_ROWS, D), F32), pltpu.SemaphoreType.DMA((2,))]),
        out_shape=jax.ShapeDtypeStruct((n_blk * MOE_ROWS, D), BF16),
        compiler_params=_cparams(1),
        name="moe_gather",
    )(n_used, buf_tok, buf_tok, h)


def _gmm1_kernel(be_ref, meta_ref, nxt_ref, x_ref, w_hbm, bg_ref, bl_ref, o_ref, stage, w_scr, sem, *,
                 layer, n_pass, d_ff):
    i = pl.program_id(1)

    @pl.when(i < meta_ref[0])
    def _():
        _stage_expert_weights(be_ref, meta_ref, nxt_ref, w_hbm, stage, w_scr, sem, layer,
                              lambda jj: (jj * TF, d_ff + jj * TF), n_pass)
        x = x_ref[...]
        glu = jnp.dot(x, w_scr[0], preferred_element_type=F32) + bg_ref[0, 0]
        lin = jnp.dot(x, w_scr[1], preferred_element_type=F32) + bl_ref[0, 0]
        glu = jnp.minimum(glu, SWIGLU_LIMIT)
        lin = jnp.clip(lin, -SWIGLU_LIMIT, SWIGLU_LIMIT)
        o_ref[...] = (glu * jax.nn.sigmoid(SWIGLU_ALPHA * glu) * (lin + 1.0)).astype(BF16)

    @pl.when(i >= meta_ref[0])
    def _():
        o_ref[...] = jnp.zeros(o_ref.shape, o_ref.dtype)


def _gmm2_kernel(be_ref, meta_ref, nxt_ref, a_ref, w_hbm, b_ref, o_ref, stage, w_scr, sem, *, layer, n_pass):
    i = pl.program_id(1)

    @pl.when(i < meta_ref[0])
    def _():
        _stage_expert_weights(be_ref, meta_ref, nxt_ref, w_hbm, stage, w_scr, sem, layer,
                              lambda jj: (jj * TN_MOE,), n_pass)
        o_ref[...] = jnp.dot(a_ref[...], w_scr[0], preferred_element_type=F32) + b_ref[0, 0]

    @pl.when(i >= meta_ref[0])
    def _():
        o_ref[...] = jnp.zeros(o_ref.shape, o_ref.dtype)


def _experts(layer, blk_expert, meta, nxt, xs, w1, b1, w2, b2):
    P, D = xs.shape
    L, E, _, F2 = w1.shape
    F = F2 // 2
    n_blk = P // MOE_ROWS
    nf = F // TF
    nd = D // TN_MOE
    b1r = b1.reshape(L, E, 1, F2)
    act = pl.pallas_call(
        functools.partial(_gmm1_kernel, layer=layer, n_pass=nf, d_ff=F),
        grid_spec=pltpu.PrefetchScalarGridSpec(
            num_scalar_prefetch=3,
            grid=(nf, n_blk),
            in_specs=[pl.BlockSpec((MOE_ROWS, D), lambda j, i, be, mt, nx: (i, 0)),
                      pl.BlockSpec(memory_space=pl.ANY),
                      pl.BlockSpec((1, 1, 1, TF), lambda j, i, be, mt, nx: (layer, be[i], 0, j)),
                      pl.BlockSpec((1, 1, 1, TF), lambda j, i, be, mt, nx: (layer, be[i], 0, nf + j))],
            out_specs=pl.BlockSpec((MOE_ROWS, TF), lambda j, i, be, mt, nx: (i, j)),
            scratch_shapes=[pltpu.VMEM((2, D, TF), F32), pltpu.VMEM((2, D, TF), BF16),
                            pltpu.SemaphoreType.DMA((2,))]),
        out_shape=jax.ShapeDtypeStruct((P, F), BF16),
        compiler_params=_cparams(2),
        name="moe_up",
    )(blk_expert, meta, nxt, xs, w1, b1r, b1r)
    return pl.pallas_call(
        functools.partial(_gmm2_kernel, layer=layer, n_pass=nd),
        grid_spec=pltpu.PrefetchScalarGridSpec(
            num_scalar_prefetch=3,
            grid=(nd, n_blk),
            in_specs=[pl.BlockSpec((MOE_ROWS, F), lambda j, i, be, mt, nx: (i, 0)),
                      pl.BlockSpec(memory_space=pl.ANY),
                      pl.BlockSpec((1, 1, 1, TN_MOE), lambda j, i, be, mt, nx: (layer, be[i], 0, j))],
            out_specs=pl.BlockSpec((MOE_ROWS, TN_MOE), lambda j, i, be, mt, nx: (i, j)),
            scratch_shapes=[pltpu.VMEM((1, F, TN_MOE), F32), pltpu.VMEM((1, F, TN_MOE), BF16),
                            pltpu.SemaphoreType.DMA((1,))]),
        out_shape=jax.ShapeDtypeStruct((P, D), F32),
        compiler_params=_cparams(2),
        name="moe_down",
    )(blk_expert, meta, nxt, act, w2, b2.reshape(L, E, 1, D))


def _combine_kernel(pos_ref, nxt_ref, y_hbm, x_ref, gate_ref, g2_ref, nf_ref, o_ref, buf, sem, *, final):
    i = pl.program_id(0)
    n_steps = pl.num_programs(0)
    n_tok = x_ref.shape[0]
    n_rows = TOP_K * n_tok
    slot = i % 2
    by_choice = lambda a: (a % TOP_K) * n_tok + a // TOP_K

    @pl.when(i == 0)
    def _():
        _row_gather(y_hbm, pos_ref, buf, sem, 0, n_rows, by_choice, False)

    @pl.when(i + 1 < n_steps)
    def _():
        _row_gather(y_hbm, nxt_ref, buf, sem, 1 - slot, n_rows, by_choice, False)

    _row_gather(y_hbm, pos_ref, buf, sem, slot, n_rows, by_choice, True)
    gate = gate_ref[...]
    moe = gate[:, 0:1] * buf[slot, 0:n_tok, :]
    for k in range(1, TOP_K):
        moe = moe + gate[:, k:k + 1] * buf[slot, k * n_tok:(k + 1) * n_tok, :]
    x = x_ref[...] + g2_ref[0] * moe
    if final:
        ms = jnp.mean(x * x, axis=-1, keepdims=True)
        x = x * lax.rsqrt(ms + EPS) * nf_ref[...]
    o_ref[...] = x


def _combine(pos, y, x2d, gate, g2, norm_final, T, final):
    N, D = x2d.shape
    nt = T // TC
    n_steps = N // TC
    return pl.pallas_call(
        functools.partial(_combine_kernel, final=final),
        grid=(n_steps,),
        in_specs=[pl.BlockSpec((1, 1, TOP_K * TC), lambda i: (i, 0, 0), memory_space=pltpu.SMEM),
                  pl.BlockSpec((1, 1, TOP_K * TC), lambda i: (jnp.minimum(i + 1, n_steps - 1), 0, 0),
                               memory_space=pltpu.SMEM),
                  pl.BlockSpec(memory_space=pl.ANY),
                  pl.BlockSpec((TC, D), lambda i: (i, 0)),
                  pl.BlockSpec((TC, LANES), lambda i: (i, 0)),
                  pl.BlockSpec((1, 1, D), lambda i: (i // nt, 0, 0)),
                  pl.BlockSpec((1, D), lambda i: (0, 0))],
        out_specs=pl.BlockSpec((TC, D), lambda i: (i, 0)),
        out_shape=jax.ShapeDtypeStruct((N, D), F32),
        scratch_shapes=[pltpu.VMEM((2, TOP_K * TC, D), F32), pltpu.SemaphoreType.DMA((2,))],
        compiler_params=_cparams(1),
        name="moe_combine_final" if final else "moe_combine",
    )(pos, pos, y, x2d, gate, g2, norm_final)


def _routing_tables(top_idx):
    N = top_idx.shape[0]
    NK = N * TOP_K
    e_flat = top_idx.reshape(NK)
    onehot = (e_flat[:, None] == jnp.arange(N_EXPERTS, dtype=jnp.int32)[None, :]).astype(jnp.int32)
    csum = jnp.cumsum(onehot, axis=0)
    counts = csum[-1]
    rank = jnp.sum(csum * onehot, axis=1) - 1
    padded = (counts + MOE_ROWS - 1) // MOE_ROWS * MOE_ROWS
    pad_end = jnp.cumsum(padded)
    pad_start = pad_end - padded
    dest = jnp.sum(onehot * pad_start[None, :], axis=1) + rank
    P = NK + N_EXPERTS * MOE_ROWS
    n_blk = P // MOE_ROWS
    tok = jnp.arange(NK, dtype=jnp.int32) // TOP_K
    buf_tok = jnp.zeros((P,), jnp.int32).at[dest].set(tok)
    blk_start = jnp.arange(n_blk, dtype=jnp.int32) * MOE_ROWS
    blk_expert = jnp.minimum(jnp.sum(blk_start[:, None] >= pad_end[None, :], axis=-1), N_EXPERTS - 1)
    n_used = pad_end[-1] // MOE_ROWS
    ids = jnp.arange(N_EXPERTS, dtype=jnp.int32)
    at_or_after = lax.cummin(jnp.where(counts > 0, ids, N_EXPERTS), reverse=True)
    nxt = jnp.concatenate([at_or_after[1:], jnp.full((1,), N_EXPERTS, jnp.int32)])
    nxt = jnp.where(nxt < N_EXPERTS, nxt, -1)
    meta = jnp.stack([n_used, at_or_after[0]])
    return (dest.astype(jnp.int32), buf_tok.reshape(n_blk, 1, MOE_ROWS), blk_expert.astype(jnp.int32),
            meta.astype(jnp.int32), nxt.astype(jnp.int32))


def _rope_tables(T):
    inv = 1.0 / (ROPE_THETA ** (jnp.arange(0, HEAD_DIM, 2, dtype=F32) / HEAD_DIM))
    ang = jnp.arange(T, dtype=F32)[:, None] * inv[None, :]
    return jnp.cos(ang), jnp.sin(ang)


def kernel(x, c, w_mod, b_mod, norm_attn, norm_ffn, w_in, cmp_pe, cmp_w1, cmp_w2, mix_norm, w_out,
           w_router, b_router, w_exp1, b_exp1, w_exp2, b_exp2, norm_final):
    B, T, D = x.shape
    N = B * T
    G = B_KV_HEADS
    assert D == D_MODEL and T % TM_PROJ == 0 and T % TM_ROUTE == 0
    assert all(T % n == 0 for n in (TQ_A, TK_A, TQ_B, TK_B, TQ_C, TK_C)) and TK_B % TQ_B == 0 and TQ_C % TK_C == 0

    cos, sin = _rope_tables(T)
    cos128 = jnp.tile(cos, (1, LANES // (HEAD_DIM // 2)))
    sin128 = jnp.tile(jnp.concatenate([-sin, sin], axis=1), (1, LANES // HEAD_DIM))
    n_cmp = (T - CMP_BLOCK) // CMP_STRIDE + 1
    n_chunk = T // CMP_STRIDE
    assert n_chunk == n_cmp + 1
    cmp_end = np.arange(n_chunk) * CMP_STRIDE + CMP_BLOCK - 1
    cmp_end = np.minimum(cmp_end, T - 1)
    cosc = jnp.stack([jnp.tile(cos[cmp_end], (1, 4)), jnp.ones((n_chunk, LANES), F32)])
    sinc = jnp.stack([jnp.tile(sin[cmp_end], (1, 4)), jnp.zeros((n_chunk, LANES), F32)])
    tbl = jnp.asarray(_dilated_bias_table())
    ov = jnp.asarray(_overlap_table_t(T))

    mod = _modulation(c, w_mod, b_mod).reshape(DEPTH, B, 6, 1, D)
    w_in_re = _permute_cols(w_in).astype(BF16)
    w_out_bf = w_out.astype(BF16)
    wr_pad = jnp.zeros((DEPTH, D, LANES), F32).at[:, :, :N_EXPERTS].set(w_router)
    br_pad = jnp.zeros((DEPTH, 1, LANES), F32).at[:, 0, :N_EXPERTS].set(b_router)
    half = HEAD_DIM // 2
    w2d = jnp.concatenate([cmp_w2, cmp_w2], axis=-1)
    w2rot = jnp.concatenate([-cmp_w2[..., half:], cmp_w2[..., :half]], axis=-1)
    w2r = jnp.concatenate([w2rot, w2rot], axis=-1)
    pe = cmp_pe.reshape(DEPTH, 2, 2, CMP_STRIDE * HEAD_DIM)

    xf = x.reshape(N, D)
    for i in range(DEPTH):
        sh1, sc1, g1, sh2, sc2, g2 = [mod[i, :, k] for k in range(6)]
        proj = _in_projection(xf, sc1, sh1, norm_attn[i][None, :], w_in_re[i], cos128, sin128, T)
        proj3 = proj.reshape(B, T, N_COLS)
        o_a = _attention_a(proj3, tbl)
        kcvc = proj3[:, :, BLK_KCVC * LANES:(BLK_KCVC + 3) * LANES]
        chunks = kcvc.reshape(B, T, 2, G, HEAD_DIM).transpose(0, 2, 3, 1, 4).reshape(
            B, 2, G, n_chunk, CMP_STRIDE * HEAD_DIM)
        cmp_kv = _compress(chunks, pe[i], cmp_w1[i], w2d[i], w2r[i], cosc, sinc)
        o_b = _attention_b(proj3, cmp_kv, ov)
        o_c = _attention_c(proj3)
        xf = _out_projection(o_a.reshape(N, -1), o_b.reshape(N, -1), o_c.reshape(N, -1),
                             mix_norm[i][None, :], w_out_bf[i], xf, g1, T)
        h2, idx128, gate128 = _router(xf, sc2, sh2, norm_ffn[i][None, :], wr_pad[i], br_pad[i], T)
        dest, buf_tok, blk_expert, meta, nxt = _routing_tables(idx128[:, :TOP_K])
        xs = _gather_rows(meta, buf_tok, h2)
        y = _experts(i, blk_expert, meta, nxt, xs, w_exp1, b_exp1, w_exp2, b_exp2)
        xf = _combine(dest.reshape(N // TC, 1, TOP_K * TC), y, xf, gate128, g2,
                      norm_final[None, :], T, final=(i == DEPTH - 1))
    return xf.reshape(B, T, D)
```

```python
import functools

import numpy as np
import jax
import jax.numpy as jnp
from jax import lax
from jax.experimental import pallas as pl
from jax.experimental.pallas import tpu as pltpu

F32 = jnp.float32
BF16 = jnp.bfloat16
HIGHEST = lax.Precision.HIGHEST

D_MODEL = 2048
DEPTH = 2
HEAD_DIM = 64
A_HEADS = 12
A_PATTERNS = ((128, 1), (512, 4), (2048, 16))
B_HEADS = 12
B_KV_HEADS = 3
B_BRANCHES = 3
CMP_BLOCK = 32
CMP_STRIDE = 16
CMP_HIDDEN = 128
SLC_BLOCK = 64
SLC_TOPK = 16
WIN = 512
C_HEADS = 8
A_QKV_W = 3 * A_HEADS * HEAD_DIM
B_Q_W = B_HEADS * HEAD_DIM
B_KV_W = 2 * B_BRANCHES * B_KV_HEADS * HEAD_DIM
B_GATE_W = B_BRANCHES * B_HEADS
C_QKV_W = 3 * C_HEADS * HEAD_DIM
N_EXPERTS = 32
TOP_K = 4
D_FF = 2048
SWIGLU_LIMIT = 7.0
SWIGLU_ALPHA = 1.702
ROPE_THETA = 10000.0
EPS = 1e-6
NEG_BIG = -1e30
TINY = 1e-30
FORCE = 1e4
LOWEST = -3.0e38
RUN_DEAD = 110.0

LANES = 128
VMEM_LIMIT = 56 * 1024 * 1024

TQ_A, TK_A = 512, 512
TQ_B, TK_B = 256, 512
TQ_C, TK_C = 512, 256
TM_PROJ = 512
TN_IN = 768
TN_OUT = 512
TM_ROUTE = 256
MOE_ROWS = 256
TF = 1024
TN_MOE = 2048
TC = 128

BLK_AQ, BLK_AK, BLK_BQ, BLK_KS, BLK_KW = 0, 6, 12, 18, 21
N_ROPE_BLKS = 24
BLK_AV, BLK_KCVC, BLK_VS, BLK_VW = 24, 30, 33, 36
BLK_CQ, BLK_CK, BLK_CV = 39, 43, 47
BLK_GATE = 51
N_BLKS = 54
N_COLS = N_BLKS * LANES
assert N_ROPE_BLKS * LANES % TN_IN == 0 and N_COLS % TN_IN == 0


def _cparams(n_axes):
    return pltpu.CompilerParams(dimension_semantics=("arbitrary",) * n_axes,
                                vmem_limit_bytes=VMEM_LIMIT)


def _layout():
    a0 = 0
    bq0 = A_QKV_W
    bkv0 = bq0 + B_Q_W
    bg0 = bkv0 + B_KV_W
    c0 = bg0 + B_GATE_W
    scale = HEAD_DIM ** -0.5
    runs, colscale = [], []

    def add(start, stop, s, blk):
        assert len(colscale) == blk * LANES
        runs.append((start, stop))
        colscale.extend([s] * (stop - start))

    def add_dup(arr, blk):
        for g in range(B_KV_HEADS):
            s0 = bkv0 + arr * gw + g * HEAD_DIM
            add(s0, s0 + HEAD_DIM, 1.0, blk + g)
            runs.append((s0, s0 + HEAD_DIM))
            colscale.extend([1.0] * HEAD_DIM)

    hw = A_HEADS * HEAD_DIM
    gw = B_KV_HEADS * HEAD_DIM
    cw = C_HEADS * HEAD_DIM
    add(a0, a0 + hw, scale, BLK_AQ)
    add(a0 + hw, a0 + 2 * hw, 1.0, BLK_AK)
    add(bq0, bq0 + B_Q_W, scale, BLK_BQ)
    add_dup(2, BLK_KS)
    add_dup(4, BLK_KW)
    add(a0 + 2 * hw, a0 + 3 * hw, 1.0, BLK_AV)
    add(bkv0, bkv0 + 2 * gw, 1.0, BLK_KCVC)
    add_dup(3, BLK_VS)
    add_dup(5, BLK_VW)
    add(c0, c0 + cw, scale, BLK_CQ)
    add(c0 + cw, c0 + 3 * cw, 1.0, BLK_CK)
    per_g = B_GATE_W // B_KV_HEADS
    for g in range(B_KV_HEADS):
        add(bg0 + g * per_g, bg0 + (g + 1) * per_g, 1.0, BLK_GATE + g)
        runs.append((None, LANES - per_g))
        colscale.extend([1.0] * (LANES - per_g))
    assert len(colscale) == N_COLS
    return runs, np.asarray(colscale, np.float32)[None, :]


_RUNS, _COL_SCALE = _layout()


def _permute_cols(w):
    parts = []
    for start, stop in _RUNS:
        if start is None:
            parts.append(jnp.zeros(w.shape[:-1] + (stop,), w.dtype))
        else:
            parts.append(w[..., start:stop])
    return jnp.concatenate(parts, axis=-1)


def _dilated_bias_table():
    assert TQ_A == TK_A
    n = -(-A_PATTERNS[-1][0] // TK_A) + 1
    d = (np.arange(n)[:, None, None] * TK_A + np.arange(TQ_A)[None, :, None] - np.arange(TK_A)[None, None, :])
    m = np.zeros(d.shape, np.float64)
    for window, dil in A_PATTERNS:
        m += ((d >= 0) & (d <= window) & (d % dil == 0))
    return np.where(m > 0, np.log(np.maximum(m, 1.0)), NEG_BIG).astype(np.float32)


def _overlap_table_t(T):
    n_cmp = (T - CMP_BLOCK) // CMP_STRIDE + 1
    n_slc = T // SLC_BLOCK
    cs = np.arange(n_cmp) * CMP_STRIDE
    ss = np.arange(n_slc) * SLC_BLOCK
    ov = (cs[None, :] < ss[:, None] + SLC_BLOCK) & (cs[None, :] + CMP_BLOCK > ss[:, None])
    out = np.zeros((n_slc, n_cmp + 1), np.float32)
    out[:n_slc, :n_cmp] = ov
    return out


def _mod_kernel(c_ref, w_ref, b_ref, o_ref):
    c = c_ref[...]
    ca = (c * jax.nn.sigmoid(c)).astype(BF16)
    y = jnp.dot(ca, w_ref[0].astype(BF16), preferred_element_type=F32)
    o_ref[0] = y + b_ref[0]


def _modulation(c, w_mod, b_mod):
    B, D = c.shape
    L, _, W = w_mod.shape
    tn = 1024
    rows = 8
    cp = jnp.zeros((rows, D), F32).at[:B].set(c)
    out = pl.pallas_call(
        _mod_kernel,
        grid=(L, W // tn),
        in_specs=[pl.BlockSpec((rows, D), lambda l, j: (0, 0)),
                  pl.BlockSpec((1, D, tn), lambda l, j: (l, 0, j)),
                  pl.BlockSpec((1, 1, tn), lambda l, j: (l, 0, j))],
        out_specs=pl.BlockSpec((1, rows, tn), lambda l, j: (l, 0, j)),
        out_shape=jax.ShapeDtypeStruct((L, rows, W), F32),
        compiler_params=_cparams(2),
        name="adaln_mod",
    )(cp, w_mod, b_mod.reshape(L, 1, W))
    return out[:, :B]


def _norm_mod(x, gain, sc, sh):
    ms = jnp.mean(x * x, axis=-1, keepdims=True)
    y = x * lax.rsqrt(ms + EPS) * gain
    return y * (1.0 + sc) + sh


def _rope128(y, cos, sin_signed):
    lane = lax.broadcasted_iota(jnp.int32, y.shape, 1)
    first_half = (lane % HEAD_DIM) < (HEAD_DIM // 2)
    rot = jnp.where(first_half, pltpu.roll(y, LANES - HEAD_DIM // 2, 1), pltpu.roll(y, HEAD_DIM // 2, 1))
    return y * cos + rot * sin_signed


def _inproj_kernel(x_ref, sc_ref, sh_ref, g_ref, w_ref, cs_ref, cos_ref, sin_ref, o_ref, h_scr):
    @pl.when(pl.program_id(1) == 0)
    def _():
        h_scr[...] = _norm_mod(x_ref[...], g_ref[...], sc_ref[0], sh_ref[0]).astype(BF16)

    y = jnp.dot(h_scr[...], w_ref[...], preferred_element_type=F32)
    n_rope_tiles = N_ROPE_BLKS * LANES // TN_IN

    @pl.when(pl.program_id(1) < n_rope_tiles)
    def _():
        cos = cos_ref[...]
        sin = sin_ref[...]
        for c in range(TN_IN // LANES):
            sl = slice(c * LANES, (c + 1) * LANES)
            o_ref[:, sl] = _rope128(y[:, sl], cos, sin) * cs_ref[:, sl]

    @pl.when(pl.program_id(1) >= n_rope_tiles)
    def _():
        o_ref[...] = y * cs_ref[...]


def _in_projection(x2d, sc, sh, gain, w_re, cos128, sin128, T):
    N, D = x2d.shape
    nt = T // TM_PROJ
    return pl.pallas_call(
        _inproj_kernel,
        grid=(N // TM_PROJ, N_COLS // TN_IN),
        in_specs=[pl.BlockSpec((TM_PROJ, D), lambda i, j: (i, 0)),
                  pl.BlockSpec((1, 1, D), lambda i, j: (i // nt, 0, 0)),
                  pl.BlockSpec((1, 1, D), lambda i, j: (i // nt, 0, 0)),
                  pl.BlockSpec((1, D), lambda i, j: (0, 0)),
                  pl.BlockSpec((D, TN_IN), lambda i, j: (0, j)),
                  pl.BlockSpec((1, TN_IN), lambda i, j: (0, j)),
                  pl.BlockSpec((TM_PROJ, LANES), lambda i, j: (i % nt, 0)),
                  pl.BlockSpec((TM_PROJ, LANES), lambda i, j: (i % nt, 0))],
        out_specs=pl.BlockSpec((TM_PROJ, TN_IN), lambda i, j: (i, j)),
        out_shape=jax.ShapeDtypeStruct((N, N_COLS), F32),
        scratch_shapes=[pltpu.VMEM((TM_PROJ, D), BF16)],
        compiler_params=_cparams(2),
        name="in_proj",
    )(x2d, sc, sh, gain, w_re, jnp.asarray(_COL_SCALE), cos128, sin128)


def _split_heads_q(q):
    lane = lax.broadcasted_iota(jnp.int32, q.shape, 1)
    lo = lane < HEAD_DIM
    return jnp.concatenate([jnp.where(lo, q, 0.0), jnp.where(lo, 0.0, q)], axis=0)


def _scores(q2, k):
    return lax.dot_general(q2, k, (((1,), (1,)), ((), ())), preferred_element_type=F32)


def _flash_step(s, m, acc, v_ones):
    m_new = jnp.maximum(m, jnp.max(s, axis=-1, keepdims=True))
    p = jnp.exp(s - m_new).astype(BF16)
    acc = jnp.exp(m - m_new) * acc + jnp.dot(p, v_ones, preferred_element_type=F32)
    return m_new, acc


def _flash_init(rows):
    return jnp.full((rows, 1), NEG_BIG, F32), jnp.zeros((rows, LANES), F32)


def _flash_finish(acc):
    return acc / jnp.maximum(pltpu.roll(acc, HEAD_DIM, 1), TINY)


def _high_half(x):
    bits = lax.bitcast_convert_type(x, jnp.int32)
    return lax.bitcast_convert_type(bits & jnp.int32(-65536), F32)


def _split_bf16(x):
    hi = x.astype(BF16)
    return hi, (x - hi.astype(F32)).astype(BF16)


def _tile(ref, kt, n):
    return ref[0, pl.ds(pl.multiple_of(kt * n, n), n), :]


def _two_head_norm(out):
    lane = lax.broadcasted_iota(jnp.int32, out.shape, 1)
    lo = lane < HEAD_DIM
    sq = out * out
    ms0 = jnp.sum(jnp.where(lo, sq, 0.0), axis=-1, keepdims=True) * (1.0 / HEAD_DIM)
    ms1 = jnp.sum(jnp.where(lo, 0.0, sq), axis=-1, keepdims=True) * (1.0 / HEAD_DIM)
    return out * jnp.where(lo, lax.rsqrt(ms0 + EPS), lax.rsqrt(ms1 + EPS))


def _attn_a_kernel(q_ref, k_ref, v_ref, tbl_ref, o_ref):
    qi = pl.program_id(2)
    q = q_ref[0]
    lane = lax.broadcasted_iota(jnp.int32, q.shape, 1)
    lo = lane < HEAD_DIM
    q0 = jnp.where(lo, q, 0.0).astype(BF16)
    q1 = jnp.where(lo, 0.0, q).astype(BF16)
    n_d = tbl_ref.shape[0]

    def body(i, carry):
        m0, acc0, m1, acc1 = carry
        k = _tile(k_ref, qi - i, TK_A).astype(BF16)
        v = _tile(v_ref, qi - i, TK_A)
        lo_k = lax.broadcasted_iota(jnp.int32, v.shape, 1) < HEAD_DIM
        bias = tbl_ref[i]
        m0, acc0 = _flash_step(_scores(q0, k) + bias, m0, acc0, jnp.where(lo_k, v, 1.0).astype(BF16))
        m1, acc1 = _flash_step(_scores(q1, k) + bias, m1, acc1, jnp.where(lo_k, 1.0, v).astype(BF16))
        return m0, acc0, m1, acc1

    init = _flash_init(TQ_A) + _flash_init(TQ_A)
    _, acc0, _, acc1 = lax.fori_loop(0, jnp.minimum(qi, n_d - 1) + 1, body, init)
    o_ref[0] = _two_head_norm(jnp.where(lo, _flash_finish(acc0), _flash_finish(acc1)))


def _attention_a(proj, tbl):
    B, T, _ = proj.shape
    n_pairs = A_HEADS // 2
    return pl.pallas_call(
        _attn_a_kernel,
        grid=(B, n_pairs, T // TQ_A),
        in_specs=[pl.BlockSpec((1, TQ_A, LANES), lambda b, p, i: (b, i, BLK_AQ + p)),
                  pl.BlockSpec((1, T, LANES), lambda b, p, i: (b, 0, BLK_AK + p)),
                  pl.BlockSpec((1, T, LANES), lambda b, p, i: (b, 0, BLK_AV + p)),
                  pl.BlockSpec(tbl.shape, lambda b, p, i: (0, 0, 0))],
        out_specs=pl.BlockSpec((1, TQ_A, LANES), lambda b, p, i: (b, i, p)),
        out_shape=jax.ShapeDtypeStruct((B, T, n_pairs * LANES), F32),
        compiler_params=_cparams(3),
        name="attn_dilated",
    )(proj, proj, proj, tbl)


def _attn_c_kernel(q_ref, k_ref, v_ref, o_ref):
    qi = pl.program_id(2)
    q = q_ref[0]
    lo = lax.broadcasted_iota(jnp.int32, q.shape, 1) < HEAD_DIM
    qs = (jnp.where(lo, q, 0.0).astype(BF16), jnp.where(lo, 0.0, q).astype(BF16))
    diff = (lax.broadcasted_iota(jnp.int32, (TQ_C, TK_C), 1) - lax.broadcasted_iota(jnp.int32, (TQ_C, TK_C), 0))
    uj = lax.broadcasted_iota(jnp.int32, (TK_C, TK_C), 0)
    us = lax.broadcasted_iota(jnp.int32, (TK_C, TK_C), 1)
    upper = (uj >= us).astype(BF16)
    tiles_per_q = TQ_C // TK_C

    def tile(kt, carry, masked):
        k = _tile(k_ref, kt, TK_C).astype(BF16)
        v = _tile(v_ref, kt, TK_C).astype(BF16)
        if masked:
            before = diff < qi * TQ_C - kt * TK_C
        out = []
        for qh, (run, acc) in zip(qs, carry):
            z = _scores(qh, k)
            sp = jnp.maximum(z, 0.0) + jnp.log(1.0 + jnp.exp(-jnp.abs(z)))
            if masked:
                sp = jnp.where(before, sp, 0.0)
            hi = _high_half(sp)
            suffix = (jnp.dot(hi.astype(BF16), upper, preferred_element_type=F32)
                      + jnp.dot((sp - hi).astype(BF16), upper, preferred_element_type=F32))
            a = jnp.exp(z - suffix - run)
            if masked:
                a = jnp.where(before, a, 0.0)
            out.append((run + suffix[:, 0:1], acc + jnp.dot(a.astype(BF16), v, preferred_element_type=F32)))
        return tuple(out)

    carry = ((jnp.zeros((TQ_C, 1), F32), jnp.zeros((TQ_C, LANES), F32)),) * 2
    last = (qi + 1) * tiles_per_q - 1
    for j in range(tiles_per_q):
        carry = tile(last - j, carry, True)
    def rest_is_zero(c):
        return (jnp.minimum(jnp.min(c[0][0]), jnp.min(c[1][0])) >= RUN_DEAD).astype(jnp.int32)

    def more(state):
        i, dead, _ = state
        return (i <= last) & (dead == 0)

    def step(state):
        i, _, c = state
        c = tile(last - i, c, False)
        return i + 1, rest_is_zero(c), c

    _, _, carry = lax.while_loop(more, step, (jnp.int32(tiles_per_q), rest_is_zero(carry), carry))
    o_ref[0] = _two_head_norm(jnp.where(lo, carry[0][1], carry[1][1]))


def _attention_c(proj):
    B, T, _ = proj.shape
    n_pairs = C_HEADS // 2
    return pl.pallas_call(
        _attn_c_kernel,
        grid=(B, n_pairs, T // TQ_C),
        in_specs=[pl.BlockSpec((1, TQ_C, LANES), lambda b, p, i: (b, i, BLK_CQ + p)),
                  pl.BlockSpec((1, T, LANES), lambda b, p, i: (b, 0, BLK_CK + p)),
                  pl.BlockSpec((1, T, LANES), lambda b, p, i: (b, 0, BLK_CV + p))],
        out_specs=pl.BlockSpec((1, TQ_C, LANES), lambda b, p, i: (b, i, p)),
        out_shape=jax.ShapeDtypeStruct((B, T, n_pairs * LANES), F32),
        compiler_params=_cparams(3),
        name="attn_stickbreak",
    )(proj, proj, proj)


def _compress_kernel(x_ref, pe_ref, w1_ref, w2_ref, w2r_ref, cos_ref, sin_ref, o_ref):
    x = x_ref[0, 0, 0]
    half = x.shape[1]
    ha = jnp.dot(x + pe_ref[0, 0:1, :], w1_ref[0, :half, :], precision=HIGHEST, preferred_element_type=F32)
    hb = jnp.dot(x + pe_ref[0, 1:2, :], w1_ref[0, half:, :], precision=HIGHEST, preferred_element_type=F32)
    n = x.shape[0]
    h = ha + pltpu.roll(hb, n - 1, 0)
    g = jax.nn.gelu(h)
    y = jnp.dot(g, w2_ref[0], precision=HIGHEST, preferred_element_type=F32)
    yr = jnp.dot(g, w2r_ref[0], precision=HIGHEST, preferred_element_type=F32)
    out = y * cos_ref[0] + yr * sin_ref[0]
    rowi = lax.broadcasted_iota(jnp.int32, out.shape, 0)
    o_ref[0, 0, 0] = jnp.where(rowi < n - 1, out, 0.0)


def _compress(chunks, pe, w1, w2d, w2r, cosc, sinc):
    B, _, G, n, cw = chunks.shape
    return pl.pallas_call(
        _compress_kernel,
        grid=(B, 2, G),
        in_specs=[pl.BlockSpec((1, 1, 1, n, cw), lambda b, w, g: (b, w, g, 0, 0)),
                  pl.BlockSpec((1, 2, cw), lambda b, w, g: (w, 0, 0)),
                  pl.BlockSpec((1, 2 * cw, CMP_HIDDEN), lambda b, w, g: (w, 0, 0)),
                  pl.BlockSpec((1, CMP_HIDDEN, LANES), lambda b, w, g: (w, 0, 0)),
                  pl.BlockSpec((1, CMP_HIDDEN, LANES), lambda b, w, g: (w, 0, 0)),
                  pl.BlockSpec((1, n, LANES), lambda b, w, g: (w, 0, 0)),
                  pl.BlockSpec((1, n, LANES), lambda b, w, g: (w, 0, 0))],
        out_specs=pl.BlockSpec((1, 1, 1, n, LANES), lambda b, w, g: (b, w, g, 0, 0)),
        out_shape=jax.ShapeDtypeStruct((B, 2, G, n, LANES), F32),
        compiler_params=_cparams(3),
        name="nsa_compress",
    )(chunks, pe, w1, w2d, w2r, cosc, sinc)


def _nsa_kernel(q_ref, cmp_k_ref, cmp_v_ref, ks_ref, vs_ref, kw_ref, vw_ref, gate_ref, ovt_ref, o_ref):
    qi = pl.program_id(2)
    R = B_HEADS // B_KV_HEADS
    TQ, TK = TQ_B, TK_B
    rows = R * TQ
    q0 = qi * TQ
    q = q_ref[0]
    qf = jnp.concatenate([_split_heads_q(q[:, :LANES]), _split_heads_q(q[:, LANES:])], axis=0)
    q4 = qf.astype(BF16)

    kc = cmp_k_ref[0, 0, 0]
    vc = cmp_v_ref[0, 0, 0].astype(BF16)
    n_c = kc.shape[0]
    last_end = (n_c - 1) * CMP_STRIDE + CMP_BLOCK - 1
    t_col = q0 + lax.broadcasted_iota(jnp.int32, (TQ, 1), 0)
    cmp_end = lax.broadcasted_iota(jnp.int32, (TQ, n_c), 1) * CMP_STRIDE + (CMP_BLOCK - 1)
    valid = (cmp_end <= t_col) & (cmp_end < last_end)
    t_lane = q0 + lax.broadcasted_iota(jnp.int32, (1, TQ), 1)
    nt_dims = (((1,), (1,)), ((), ()))
    kc_hi, kc_lo = _split_bf16(kc)
    q_hi, q_lo = _split_bf16(qf)
    s = _scores(q_hi, kc_hi) + (_scores(q_hi, kc_lo) + _scores(q_lo, kc_hi))
    s = jnp.where(valid[None], s.reshape(R, TQ, n_c), NEG_BIG)
    p = jnp.exp(s - jnp.max(s, axis=-1, keepdims=True)) * valid.astype(F32)[None]
    p = p / jnp.maximum(jnp.sum(p, axis=-1, keepdims=True), TINY)
    o_cmp_all = jnp.dot(p.reshape(rows, n_c).astype(BF16), vc, preferred_element_type=F32)
    o_cmp = [o_cmp_all[r * TQ:(r + 1) * TQ] for r in range(R)]
    p_sum = jnp.sum(p, axis=0)
    imp = lax.dot_general(ovt_ref[...], p_sum, nt_dims, precision=HIGHEST, preferred_element_type=F32)
    n_slc = imp.shape[0]
    blk = lax.broadcasted_iota(jnp.int32, (n_slc, TQ), 0)
    blk_f = blk.astype(F32)
    tb = t_lane // SLC_BLOCK
    forced = (blk == 0) | (blk == tb) | (blk == tb - 1)
    imp = jnp.where(forced, imp + FORCE, imp)
    imp = jnp.where(blk > tb, -FORCE, imp)
    sel = jnp.zeros((n_slc, TQ), F32)
    for _ in range(min(SLC_TOPK, n_slc)):
        mx = jnp.max(imp, axis=0, keepdims=True)
        first = jnp.min(jnp.where(imp == mx, blk_f, float(n_slc)), axis=0, keepdims=True)
        pick = blk_f == first
        sel = jnp.where(pick, 1.0, sel)
        imp = jnp.where(pick, LOWEST, imp)
    sel_bias = jnp.concatenate([(sel - 1.0) * (-NEG_BIG), jnp.zeros((LANES - n_slc, TQ), F32)], axis=0)
    sel_bias = sel_bias.T.astype(BF16)

    diff = lax.broadcasted_iota(jnp.int32, (TQ, TK), 1) - lax.broadcasted_iota(jnp.int32, (TQ, TK), 0)
    e_row = lax.broadcasted_iota(jnp.int32, (LANES, TK), 0)
    e_col = lax.broadcasted_iota(jnp.int32, (LANES, TK), 1) // SLC_BLOCK
    lo_k = lax.broadcasted_iota(jnp.int32, (TK, LANES), 1) < HEAD_DIM
    last = q0 // TK

    def step(k_ref, v_ref, kt, bias, carry):
        k = _tile(k_ref, kt, TK).astype(BF16)
        v_ones = jnp.where(lo_k, _tile(v_ref, kt, TK), 1.0).astype(BF16)
        s = (_scores(q4, k).reshape(R, TQ, TK) + bias[None]).reshape(rows, TK)
        return _flash_step(s, *carry, v_ones)

    def slc_tile(kt, carry, causal):
        expand = (e_row == kt * (TK // SLC_BLOCK) + e_col).astype(BF16)
        bias = jnp.dot(sel_bias, expand, preferred_element_type=F32)
        if causal:
            bias = bias + jnp.where(diff <= q0 - kt * TK, 0.0, NEG_BIG)
        return step(ks_ref, vs_ref, kt, bias, carry)

    carry = lax.fori_loop(0, last, lambda kt, c: slc_tile(kt, c, False), _flash_init(rows))
    o_slc = _flash_finish(slc_tile(last, carry, True)[1])

    def win_tile(i, carry):
        kt = last - i
        off = q0 - kt * TK
        bias = jnp.where((diff <= off) & (diff > off - WIN), 0.0, NEG_BIG)
        return step(kw_ref, vw_ref, kt, bias, carry)

    first_win = jnp.maximum(q0 - (WIN - 1), 0) // TK
    o_win = _flash_finish(lax.fori_loop(0, last - first_win + 1, win_tile, _flash_init(rows))[1])

    gates = jax.nn.sigmoid(gate_ref[0])
    lo = lax.broadcasted_iota(jnp.int32, (TQ, LANES), 1) < HEAD_DIM
    outs = []
    for r in range(R):
        rs = slice(r * TQ, (r + 1) * TQ)
        o = (gates[:, 3 * r:3 * r + 1] * o_cmp[r] + gates[:, 3 * r + 1:3 * r + 2] * o_slc[rs]
             + gates[:, 3 * r + 2:3 * r + 3] * o_win[rs])
        ms = jnp.sum(jnp.where(lo, o * o, 0.0), axis=-1, keepdims=True) * (1.0 / HEAD_DIM)
        outs.append(o * lax.rsqrt(ms + EPS))
    pairs = [jnp.where(lo, outs[2 * j], pltpu.roll(outs[2 * j + 1], HEAD_DIM, 1)) for j in range(R // 2)]
    o_ref[0] = jnp.concatenate(pairs, axis=1)


def _attention_b(proj, cmp_kv, ov):
    B, T, _ = proj.shape
    G = B_KV_HEADS
    n_c = cmp_kv.shape[3]
    qw = 2 * LANES
    TQ = TQ_B
    return pl.pallas_call(
        _nsa_kernel,
        grid=(B, G, T // TQ),
        in_specs=[pl.BlockSpec((1, TQ, qw), lambda b, g, i: (b, i, BLK_BQ // 2 + g)),
                  pl.BlockSpec((1, 1, 1, n_c, LANES), lambda b, g, i: (b, 0, g, 0, 0)),
                  pl.BlockSpec((1, 1, 1, n_c, LANES), lambda b, g, i: (b, 1, g, 0, 0)),
                  pl.BlockSpec((1, T, LANES), lambda b, g, i: (b, 0, BLK_KS + g)),
                  pl.BlockSpec((1, T, LANES), lambda b, g, i: (b, 0, BLK_VS + g)),
                  pl.BlockSpec((1, T, LANES), lambda b, g, i: (b, 0, BLK_KW + g)),
                  pl.BlockSpec((1, T, LANES), lambda b, g, i: (b, 0, BLK_VW + g)),
                  pl.BlockSpec((1, TQ, LANES), lambda b, g, i: (b, i, BLK_GATE + g)),
                  pl.BlockSpec(ov.shape, lambda b, g, i: (0, 0))],
        out_specs=pl.BlockSpec((1, TQ, qw), lambda b, g, i: (b, i, g)),
        out_shape=jax.ShapeDtypeStruct((B, T, B_HEADS * HEAD_DIM), F32),
        compiler_params=_cparams(3),
        name="attn_nsa",
    )(proj, cmp_kv, cmp_kv, proj, proj, proj, proj, proj, ov)


def _outproj_kernel(oa_ref, ob_ref, oc_ref, mn_ref, w_ref, x_ref, g_ref, o_ref, h_scr):
    wa = oa_ref.shape[1]
    wb = ob_ref.shape[1]

    @pl.when(pl.program_id(1) == 0)
    def _():
        h_scr[:, :wa] = (oa_ref[...] * mn_ref[:, :wa]).astype(BF16)
        h_scr[:, wa:wa + wb] = (ob_ref[...] * mn_ref[:, wa:wa + wb]).astype(BF16)
        h_scr[:, wa + wb:] = (oc_ref[...] * mn_ref[:, wa + wb:]).astype(BF16)

    y = jnp.dot(h_scr[...], w_ref[...], preferred_element_type=F32)
    o_ref[...] = x_ref[...] + g_ref[0] * y


def _out_projection(oa, ob, oc, mix_norm, w_out_bf, x2d, g1, T):
    N, D = x2d.shape
    nt = T // TM_PROJ
    wa, wb, wc = oa.shape[1], ob.shape[1], oc.shape[1]
    return pl.pallas_call(
        _outproj_kernel,
        grid=(N // TM_PROJ, D // TN_OUT),
        in_specs=[pl.BlockSpec((TM_PROJ, wa), lambda i, j: (i, 0)),
                  pl.BlockSpec((TM_PROJ, wb), lambda i, j: (i, 0)),
                  pl.BlockSpec((TM_PROJ, wc), lambda i, j: (i, 0)),
                  pl.BlockSpec((1, D), lambda i, j: (0, 0)),
                  pl.BlockSpec((D, TN_OUT), lambda i, j: (0, j)),
                  pl.BlockSpec((TM_PROJ, TN_OUT), lambda i, j: (i, j)),
                  pl.BlockSpec((1, 1, TN_OUT), lambda i, j: (i // nt, 0, j))],
        out_specs=pl.BlockSpec((TM_PROJ, TN_OUT), lambda i, j: (i, j)),
        out_shape=jax.ShapeDtypeStruct((N, D), F32),
        scratch_shapes=[pltpu.VMEM((TM_PROJ, D), BF16)],
        compiler_params=_cparams(2),
        name="out_proj",
    )(oa, ob, oc, mix_norm, w_out_bf, x2d, g1)


def _router_kernel(x_ref, sc_ref, sh_ref, g_ref, wr_ref, br_ref, h_ref, idx_ref, gate_ref):
    h = _norm_mod(x_ref[...], g_ref[...], sc_ref[0], sh_ref[0])
    h_ref[...] = h
    logits = jnp.dot(h, wr_ref[...], precision=HIGHEST, preferred_element_type=F32) + br_ref[...]
    lane = lax.broadcasted_iota(jnp.int32, logits.shape, 1)
    lane_f = lane.astype(F32)
    cur = jnp.where(lane < N_EXPERTS, logits, LOWEST)
    idx_out = jnp.zeros(logits.shape, F32)
    e_out = jnp.zeros(logits.shape, F32)
    top0 = None
    denom = None
    for k in range(TOP_K):
        mx = jnp.max(cur, axis=-1, keepdims=True)
        first = jnp.min(jnp.where(cur == mx, lane_f, float(LANES)), axis=-1, keepdims=True)
        cur = jnp.where(lane_f == first, LOWEST, cur)
        if k == 0:
            top0 = mx
        e = jnp.exp(mx - top0)
        denom = e if k == 0 else denom + e
        idx_out = jnp.where(lane == k, first, idx_out)
        e_out = jnp.where(lane == k, e, e_out)
    idx_ref[...] = idx_out.astype(jnp.int32)
    gate_ref[...] = e_out / denom


def _router(x2d, sc, sh, gain, wr_pad, br_pad, T):
    N, D = x2d.shape
    nt = T // TM_ROUTE
    return pl.pallas_call(
        _router_kernel,
        grid=(N // TM_ROUTE,),
        in_specs=[pl.BlockSpec((TM_ROUTE, D), lambda i: (i, 0)),
                  pl.BlockSpec((1, 1, D), lambda i: (i // nt, 0, 0)),
                  pl.BlockSpec((1, 1, D), lambda i: (i // nt, 0, 0)),
                  pl.BlockSpec((1, D), lambda i: (0, 0)),
                  pl.BlockSpec((D, LANES), lambda i: (0, 0)),
                  pl.BlockSpec((1, LANES), lambda i: (0, 0))],
        out_specs=[pl.BlockSpec((TM_ROUTE, D), lambda i: (i, 0)),
                   pl.BlockSpec((TM_ROUTE, LANES), lambda i: (i, 0)),
                   pl.BlockSpec((TM_ROUTE, LANES), lambda i: (i, 0))],
        out_shape=[jax.ShapeDtypeStruct((N, D), F32),
                   jax.ShapeDtypeStruct((N, LANES), jnp.int32),
                   jax.ShapeDtypeStruct((N, LANES), F32)],
        compiler_params=_cparams(1),
        name="moe_router",
    )(x2d, sc, sh, gain, wr_pad, br_pad)


DMA_UNROLL = 8


def _row_gather(src_hbm, idx_ref, buf, sem, slot, n_rows, dst_row, wait):
    if wait:
        pltpu.make_async_copy(src_hbm.at[pl.ds(0, n_rows), :], buf.at[slot], sem.at[slot]).wait()
        return

    def body(a8, c):
        for u in range(DMA_UNROLL):
            a = a8 * DMA_UNROLL + u
            pltpu.make_async_copy(src_hbm.at[pl.ds(idx_ref[0, 0, a], 1), :],
                                  buf.at[slot, pl.ds(dst_row(a), 1), :], sem.at[slot]).start(priority=u % 2)
        return c

    lax.fori_loop(0, n_rows // DMA_UNROLL, body, 0)


def _expert_changed(be_ref, i):
    prev = be_ref[jnp.maximum(i - 1, 0)]
    return (i == 0) | (be_ref[i] != prev)


def _stage_expert_weights(be_ref, meta_ref, nxt_ref, w_hbm, stage, w_scr, sem, layer, windows, n_pass):
    j = pl.program_id(0)
    i = pl.program_id(1)
    e = be_ref[i]
    width = stage.shape[-1]

    def copies(ee, jj):
        return [pltpu.make_async_copy(w_hbm.at[layer, ee, :, pl.ds(pl.multiple_of(c0, LANES), width)],
                                      stage.at[p], sem.at[p])
                for p, c0 in enumerate(windows(jj))]

    @pl.when(_expert_changed(be_ref, i))
    def _():
        @pl.when((j == 0) & (i == 0))
        def _():
            for cp in copies(e, j):
                cp.start()

        for cp in copies(e, j):
            cp.wait()
        for p in range(stage.shape[0]):
            w_scr[p] = stage[p].astype(BF16)
        nxt = nxt_ref[e]
        nj = jnp.where(nxt >= 0, j, j + 1)
        ne = jnp.where(nxt >= 0, nxt, meta_ref[1])

        @pl.when(nj < n_pass)
        def _():
            for cp in copies(ne, nj):
                cp.start()


def _gather_kernel(nu_ref, tok_ref, nxt_ref, h_hbm, o_ref, buf, sem):
    i = pl.program_id(0)
    n_used = nu_ref[0]
    n = buf.shape[1]
    slot = i % 2
    ident = lambda a: a

    @pl.when(i == 0)
    def _():
        _row_gather(h_hbm, tok_ref, buf, sem, 0, n, ident, False)

    @pl.when(i + 1 < n_used)
    def _():
        _row_gather(h_hbm, nxt_ref, buf, sem, 1 - slot, n, ident, False)

    @pl.when(i < n_used)
    def _():
        _row_gather(h_hbm, tok_ref, buf, sem, slot, n, ident, True)
        o_ref[...] = buf[slot].astype(BF16)

    @pl.when(i >= n_used)
    def _():
        o_ref[...] = jnp.zeros(o_ref.shape, o_ref.dtype)


def _gather_rows(n_used, buf_tok, h):
    N, D = h.shape
    n_blk = buf_tok.shape[0]
    return pl.pallas_call(
        _gather_kernel,
        grid_spec=pltpu.PrefetchScalarGridSpec(
            num_scalar_prefetch=1,
            grid=(n_blk,),
            in_specs=[pl.BlockSpec((1, 1, MOE_ROWS), lambda i, nu: (i, 0, 0), memory_space=pltpu.SMEM),
                      pl.BlockSpec((1, 1, MOE_ROWS), lambda i, nu: (jnp.minimum(i + 1, n_blk - 1), 0, 0),
                                   memory_space=pltpu.SMEM),
                      pl.BlockSpec(memory_space=pl.ANY)],
            out_specs=pl.BlockSpec((MOE_ROWS, D), lambda i, nu: (i, 0)),
            scratch_shapes=[pltpu.VMEM((2, MOE_ROWS, D), F32), pltpu.SemaphoreType.DMA((2,))]),
        out_shape=jax.ShapeDtypeStruct((n_blk * MOE_ROWS, D), BF16),
        compiler_params=_cparams(1),
        name="moe_gather",
    )(n_used, buf_tok, buf_tok, h)


def _gmm1_kernel(be_ref, meta_ref, nxt_ref, x_ref, w_hbm, bg_ref, bl_ref, o_ref, stage, w_scr, sem, *,
                 layer, n_pass, d_ff):
    i = pl.program_id(1)

    @pl.when(i < meta_ref[0])
    def _():
        _stage_expert_weights(be_ref, meta_ref, nxt_ref, w_hbm, stage, w_scr, sem, layer,
                              lambda jj: (jj * TF, d_ff + jj * TF), n_pass)
        x = x_ref[...]
        glu = jnp.dot(x, w_scr[0], preferred_element_type=F32) + bg_ref[0, 0]
        lin = jnp.dot(x, w_scr[1], preferred_element_type=F32) + bl_ref[0, 0]
        glu = jnp.minimum(glu, SWIGLU_LIMIT)
        lin = jnp.clip(lin, -SWIGLU_LIMIT, SWIGLU_LIMIT)
        o_ref[...] = (glu * jax.nn.sigmoid(SWIGLU_ALPHA * glu) * (lin + 1.0)).astype(BF16)

    @pl.when(i >= meta_ref[0])
    def _():
        o_ref[...] = jnp.zeros(o_ref.shape, o_ref.dtype)


def _gmm2_kernel(be_ref, meta_ref, nxt_ref, a_ref, w_hbm, b_ref, o_ref, stage, w_scr, sem, *, layer, n_pass):
    i = pl.program_id(1)

    @pl.when(i < meta_ref[0])
    def _():
        _stage_expert_weights(be_ref, meta_ref, nxt_ref, w_hbm, stage, w_scr, sem, layer,
                              lambda jj: (jj * TN_MOE,), n_pass)
        o_ref[...] = jnp.dot(a_ref[...], w_scr[0], preferred_element_type=F32) + b_ref[0, 0]

    @pl.when(i >= meta_ref[0])
    def _():
        o_ref[...] = jnp.zeros(o_ref.shape, o_ref.dtype)


def _experts(layer, blk_expert, meta, nxt, xs, w1, b1, w2, b2):
    P, D = xs.shape
    L, E, _, F2 = w1.shape
    F = F2 // 2
    n_blk = P // MOE_ROWS
    nf = F // TF
    nd = D // TN_MOE
    b1r = b1.reshape(L, E, 1, F2)
    act = pl.pallas_call(
        functools.partial(_gmm1_kernel, layer=layer, n_pass=nf, d_ff=F),
        grid_spec=pltpu.PrefetchScalarGridSpec(
            num_scalar_prefetch=3,
            grid=(nf, n_blk),
            in_specs=[pl.BlockSpec((MOE_ROWS, D), lambda j, i, be, mt, nx: (i, 0)),
                      pl.BlockSpec(memory_space=pl.ANY),
                      pl.BlockSpec((1, 1, 1, TF), lambda j, i, be, mt, nx: (layer, be[i], 0, j)),
                      pl.BlockSpec((1, 1, 1, TF), lambda j, i, be, mt, nx: (layer, be[i], 0, nf + j))],
            out_specs=pl.BlockSpec((MOE_ROWS, TF), lambda j, i, be, mt, nx: (i, j)),
            scratch_shapes=[pltpu.VMEM((2, D, TF), F32), pltpu.VMEM((2, D, TF), BF16),
                            pltpu.SemaphoreType.DMA((2,))]),
        out_shape=jax.ShapeDtypeStruct((P, F), BF16),
        compiler_params=_cparams(2),
        name="moe_up",
    )(blk_expert, meta, nxt, xs, w1, b1r, b1r)
    return pl.pallas_call(
        functools.partial(_gmm2_kernel, layer=layer, n_pass=nd),
        grid_spec=pltpu.PrefetchScalarGridSpec(
            num_scalar_prefetch=3,
            grid=(nd, n_blk),
            in_specs=[pl.BlockSpec((MOE_ROWS, F), lambda j, i, be, mt, nx: (i, 0)),
                      pl.BlockSpec(memory_space=pl.ANY),
                      pl.BlockSpec((1, 1, 1, TN_MOE), lambda j, i, be, mt, nx: (layer, be[i], 0, j))],
            out_specs=pl.BlockSpec((MOE_ROWS, TN_MOE), lambda j, i, be, mt, nx: (i, j)),
            scratch_shapes=[pltpu.VMEM((1, F, TN_MOE), F32), pltpu.VMEM((1, F, TN_MOE), BF16),
                            pltpu.SemaphoreType.DMA((1,))]),
        out_shape=jax.ShapeDtypeStruct((P, D), F32),
        compiler_params=_cparams(2),
        name="moe_down",
    )(blk_expert, meta, nxt, act, w2, b2.reshape(L, E, 1, D))


def _combine_kernel(pos_ref, nxt_ref, y_hbm, x_ref, gate_ref, g2_ref, nf_ref, o_ref, buf, sem, *, final):
    i = pl.program_id(0)
    n_steps = pl.num_programs(0)
    n_tok = x_ref.shape[0]
    n_rows = TOP_K * n_tok
    slot = i % 2
    by_choice = lambda a: (a % TOP_K) * n_tok + a // TOP_K

    @pl.when(i == 0)
    def _():
        _row_gather(y_hbm, pos_ref, buf, sem, 0, n_rows, by_choice, False)

    @pl.when(i + 1 < n_steps)
    def _():
        _row_gather(y_hbm, nxt_ref, buf, sem, 1 - slot, n_rows, by_choice, False)

    _row_gather(y_hbm, pos_ref, buf, sem, slot, n_rows, by_choice, True)
    gate = gate_ref[...]
    moe = gate[:, 0:1] * buf[slot, 0:n_tok, :]
    for k in range(1, TOP_K):
        moe = moe + gate[:, k:k + 1] * buf[slot, k * n_tok:(k + 1) * n_tok, :]
    x = x_ref[...] + g2_ref[0] * moe
    if final:
        ms = jnp.mean(x * x, axis=-1, keepdims=True)
        x = x * lax.rsqrt(ms + EPS) * nf_ref[...]
    o_ref[...] = x


def _combine(pos, y, x2d, gate, g2, norm_final, T, final):
    N, D = x2d.shape
    nt = T // TC
    n_steps = N // TC
    return pl.pallas_call(
        functools.partial(_combine_kernel, final=final),
        grid=(n_steps,),
        in_specs=[pl.BlockSpec((1, 1, TOP_K * TC), lambda i: (i, 0, 0), memory_space=pltpu.SMEM),
                  pl.BlockSpec((1, 1, TOP_K * TC), lambda i: (jnp.minimum(i + 1, n_steps - 1), 0, 0),
                               memory_space=pltpu.SMEM),
                  pl.BlockSpec(memory_space=pl.ANY),
                  pl.BlockSpec((TC, D), lambda i: (i, 0)),
                  pl.BlockSpec((TC, LANES), lambda i: (i, 0)),
                  pl.BlockSpec((1, 1, D), lambda i: (i // nt, 0, 0)),
                  pl.BlockSpec((1, D), lambda i: (0, 0))],
        out_specs=pl.BlockSpec((TC, D), lambda i: (i, 0)),
        out_shape=jax.ShapeDtypeStruct((N, D), F32),
        scratch_shapes=[pltpu.VMEM((2, TOP_K * TC, D), F32), pltpu.SemaphoreType.DMA((2,))],
        compiler_params=_cparams(1),
        name="moe_combine_final" if final else "moe_combine",
    )(pos, pos, y, x2d, gate, g2, norm_final)


def _routing_tables(top_idx):
    N = top_idx.shape[0]
    NK = N * TOP_K
    e_flat = top_idx.reshape(NK)
    onehot = (e_flat[:, None] == jnp.arange(N_EXPERTS, dtype=jnp.int32)[None, :]).astype(jnp.int32)
    csum = jnp.cumsum(onehot, axis=0)
    counts = csum[-1]
    rank = jnp.sum(csum * onehot, axis=1) - 1
    padded = (counts + MOE_ROWS - 1) // MOE_ROWS * MOE_ROWS
    pad_end = jnp.cumsum(padded)
    pad_start = pad_end - padded
    dest = jnp.sum(onehot * pad_start[None, :], axis=1) + rank
    P = NK + N_EXPERTS * MOE_ROWS
    n_blk = P // MOE_ROWS
    tok = jnp.arange(NK, dtype=jnp.int32) // TOP_K
    buf_tok = jnp.zeros((P,), jnp.int32).at[dest].set(tok)
    blk_start = jnp.arange(n_blk, dtype=jnp.int32) * MOE_ROWS
    blk_expert = jnp.minimum(jnp.sum(blk_start[:, None] >= pad_end[None, :], axis=-1), N_EXPERTS - 1)
    n_used = pad_end[-1] // MOE_ROWS
    ids = jnp.arange(N_EXPERTS, dtype=jnp.int32)
    at_or_after = lax.cummin(jnp.where(counts > 0, ids, N_EXPERTS), reverse=True)
    nxt = jnp.concatenate([at_or_after[1:], jnp.full((1,), N_EXPERTS, jnp.int32)])
    nxt = jnp.where(nxt < N_EXPERTS, nxt, -1)
    meta = jnp.stack([n_used, at_or_after[0]])
    return (dest.astype(jnp.int32), buf_tok.reshape(n_blk, 1, MOE_ROWS), blk_expert.astype(jnp.int32),
            meta.astype(jnp.int32), nxt.astype(jnp.int32))


def _rope_tables(T):
    inv = 1.0 / (ROPE_THETA ** (jnp.arange(0, HEAD_DIM, 2, dtype=F32) / HEAD_DIM))
    ang = jnp.arange(T, dtype=F32)[:, None] * inv[None, :]
    return jnp.cos(ang), jnp.sin(ang)


def kernel(x, c, w_mod, b_mod, norm_attn, norm_ffn, w_in, cmp_pe, cmp_w1, cmp_w2, mix_norm, w_out,
           w_router, b_router, w_exp1, b_exp1, w_exp2, b_exp2, norm_final):
    B, T, D = x.shape
    N = B * T
    G = B_KV_HEADS
    assert D == D_MODEL and T % TM_PROJ == 0 and T % TM_ROUTE == 0
    assert all(T % n == 0 for n in (TQ_A, TK_A, TQ_B, TK_B, TQ_C, TK_C)) and TK_B % TQ_B == 0 and TQ_C % TK_C == 0

    cos, sin = _rope_tables(T)
    cos128 = jnp.tile(cos, (1, LANES // (HEAD_DIM // 2)))
    sin128 = jnp.tile(jnp.concatenate([-sin, sin], axis=1), (1, LANES // HEAD_DIM))
    n_cmp = (T - CMP_BLOCK) // CMP_STRIDE + 1
    n_chunk = T // CMP_STRIDE
    assert n_chunk == n_cmp + 1
    cmp_end = np.arange(n_chunk) * CMP_STRIDE + CMP_BLOCK - 1
    cmp_end = np.minimum(cmp_end, T - 1)
    cosc = jnp.stack([jnp.tile(cos[cmp_end], (1, 4)), jnp.ones((n_chunk, LANES), F32)])
    sinc = jnp.stack([jnp.tile(sin[cmp_end], (1, 4)), jnp.zeros((n_chunk, LANES), F32)])
    tbl = jnp.asarray(_dilated_bias_table())
    ov = jnp.asarray(_overlap_table_t(T))

    mod = _modulation(c, w_mod, b_mod).reshape(DEPTH, B, 6, 1, D)
    w_in_re = _permute_cols(w_in).astype(BF16)
    w_out_bf = w_out.astype(BF16)
    wr_pad = jnp.zeros((DEPTH, D, LANES), F32).at[:, :, :N_EXPERTS].set(w_router)
    br_pad = jnp.zeros((DEPTH, 1, LANES), F32).at[:, 0, :N_EXPERTS].set(b_router)
    half = HEAD_DIM // 2
    w2d = jnp.concatenate([cmp_w2, cmp_w2], axis=-1)
    w2rot = jnp.concatenate([-cmp_w2[..., half:], cmp_w2[..., :half]], axis=-1)
    w2r = jnp.concatenate([w2rot, w2rot], axis=-1)
    pe = cmp_pe.reshape(DEPTH, 2, 2, CMP_STRIDE * HEAD_DIM)

    xf = x.reshape(N, D)
    for i in range(DEPTH):
        sh1, sc1, g1, sh2, sc2, g2 = [mod[i, :, k] for k in range(6)]
        proj = _in_projection(xf, sc1, sh1, norm_attn[i][None, :], w_in_re[i], cos128, sin128, T)
        proj3 = proj.reshape(B, T, N_COLS)
        o_a = _attention_a(proj3, tbl)
        kcvc = proj3[:, :, BLK_KCVC * LANES:(BLK_KCVC + 3) * LANES]
        chunks = kcvc.reshape(B, T, 2, G, HEAD_DIM).transpose(0, 2, 3, 1, 4).reshape(
            B, 2, G, n_chunk, CMP_STRIDE * HEAD_DIM)
        cmp_kv = _compress(chunks, pe[i], cmp_w1[i], w2d[i], w2r[i], cosc, sinc)
        o_b = _attention_b(proj3, cmp_kv, ov)
        o_c = _attention_c(proj3)
        xf = _out_projection(o_a.reshape(N, -1), o_b.reshape(N, -1), o_c.reshape(N, -1),
                             mix_norm[i][None, :], w_out_bf[i], xf, g1, T)
        h2, idx128, gate128 = _router(xf, sc2, sh2, norm_ffn[i][None, :], wr_pad[i], br_pad[i], T)
        dest, buf_tok, blk_expert, meta, nxt = _routing_tables(idx128[:, :TOP_K])
        xs = _gather_rows(meta, buf_tok, h2)
        y = _experts(i, blk_expert, meta, nxt, xs, w_exp1, b_exp1, w_exp2, b_exp2)
        xf = _combine(dest.reshape(N // TC, 1, TOP_K * TC), y, xf, gate128, g2,
                      norm_final[None, :], T, final=(i == DEPTH - 1))
    return xf.reshape(B, T, D)
```

```python
import functools

import numpy as np
import jax
import jax.numpy as jnp
from jax import lax
from jax.experimental import pallas as pl
from jax.experimental.pallas import tpu as pltpu

F32 = jnp.float32
BF16 = jnp.bfloat16
HIGHEST = lax.Precision.HIGHEST

D_MODEL = 2048
DEPTH = 2
HEAD_DIM = 64
A_HEADS = 12
A_PATTERNS = ((128, 1), (512, 4), (2048, 16))
B_HEADS = 12
B_KV_HEADS = 3
B_BRANCHES = 3
CMP_BLOCK = 32
CMP_STRIDE = 16
CMP_HIDDEN = 128
SLC_BLOCK = 64
SLC_TOPK = 16
WIN = 512
C_HEADS = 8
A_QKV_W = 3 * A_HEADS * HEAD_DIM
B_Q_W = B_HEADS * HEAD_DIM
B_KV_W = 2 * B_BRANCHES * B_KV_HEADS * HEAD_DIM
B_GATE_W = B_BRANCHES * B_HEADS
C_QKV_W = 3 * C_HEADS * HEAD_DIM
N_EXPERTS = 32
TOP_K = 4
D_FF = 2048
SWIGLU_LIMIT = 7.0
SWIGLU_ALPHA = 1.702
ROPE_THETA = 10000.0
EPS = 1e-6
NEG_BIG = -1e30
TINY = 1e-30
FORCE = 1e4
LOWEST = -3.0e38
RUN_DEAD = 110.0

LANES = 128
VMEM_LIMIT = 56 * 1024 * 1024

TQ_A, TK_A = 512, 512
TQ_B, TK_B = 512, 512
TQ_C, TK_C = 512, 256
TM_PROJ = 512
TN_IN = 768
TN_OUT = 512
TM_ROUTE = 256
MOE_ROWS = 256
TF = 1024
TN_MOE = 2048
TC = 128

BLK_AQ, BLK_AK, BLK_BQ, BLK_KS, BLK_KW = 0, 6, 12, 18, 21
N_ROPE_BLKS = 24
BLK_AV, BLK_KCVC, BLK_VS, BLK_VW = 24, 30, 33, 36
BLK_CQ, BLK_CK, BLK_CV = 39, 43, 47
BLK_GATE = 51
N_BLKS = 54
N_COLS = N_BLKS * LANES
assert N_ROPE_BLKS * LANES % TN_IN == 0 and N_COLS % TN_IN == 0


def _cparams(n_axes):
    return pltpu.CompilerParams(dimension_semantics=("arbitrary",) * n_axes,
                                vmem_limit_bytes=VMEM_LIMIT)


def _layout():
    a0 = 0
    bq0 = A_QKV_W
    bkv0 = bq0 + B_Q_W
    bg0 = bkv0 + B_KV_W
    c0 = bg0 + B_GATE_W
    scale = HEAD_DIM ** -0.5
    runs, colscale = [], []

    def add(start, stop, s, blk):
        assert len(colscale) == blk * LANES
        runs.append((start, stop))
        colscale.extend([s] * (stop - start))

    def add_dup(arr, blk):
        for g in range(B_KV_HEADS):
            s0 = bkv0 + arr * gw + g * HEAD_DIM
            add(s0, s0 + HEAD_DIM, 1.0, blk + g)
            runs.append((s0, s0 + HEAD_DIM))
            colscale.extend([1.0] * HEAD_DIM)

    hw = A_HEADS * HEAD_DIM
    gw = B_KV_HEADS * HEAD_DIM
    cw = C_HEADS * HEAD_DIM
    add(a0, a0 + hw, scale, BLK_AQ)
    add(a0 + hw, a0 + 2 * hw, 1.0, BLK_AK)
    add(bq0, bq0 + B_Q_W, scale, BLK_BQ)
    add_dup(2, BLK_KS)
    add_dup(4, BLK_KW)
    add(a0 + 2 * hw, a0 + 3 * hw, 1.0, BLK_AV)
    add(bkv0, bkv0 + 2 * gw, 1.0, BLK_KCVC)
    add_dup(3, BLK_VS)
    add_dup(5, BLK_VW)
    add(c0, c0 + cw, scale, BLK_CQ)
    add(c0 + cw, c0 + 3 * cw, 1.0, BLK_CK)
    per_g = B_GATE_W // B_KV_HEADS
    for g in range(B_KV_HEADS):
        add(bg0 + g * per_g, bg0 + (g + 1) * per_g, 1.0, BLK_GATE + g)
        runs.append((None, LANES - per_g))
        colscale.extend([1.0] * (LANES - per_g))
    assert len(colscale) == N_COLS
    return runs, np.asarray(colscale, np.float32)[None, :]


_RUNS, _COL_SCALE = _layout()


def _permute_cols(w):
    parts = []
    for start, stop in _RUNS:
        if start is None:
            parts.append(jnp.zeros(w.shape[:-1] + (stop,), w.dtype))
        else:
            parts.append(w[..., start:stop])
    return jnp.concatenate(parts, axis=-1)


def _dilated_bias_table():
    assert TQ_A == TK_A
    n = -(-A_PATTERNS[-1][0] // TK_A) + 1
    d = (np.arange(n)[:, None, None] * TK_A + np.arange(TQ_A)[None, :, None] - np.arange(TK_A)[None, None, :])
    m = np.zeros(d.shape, np.float64)
    for window, dil in A_PATTERNS:
        m += ((d >= 0) & (d <= window) & (d % dil == 0))
    return np.where(m > 0, np.log(np.maximum(m, 1.0)), NEG_BIG).astype(np.float32)


def _overlap_table_t(T):
    n_cmp = (T - CMP_BLOCK) // CMP_STRIDE + 1
    n_slc = T // SLC_BLOCK
    cs = np.arange(n_cmp) * CMP_STRIDE
    ss = np.arange(n_slc) * SLC_BLOCK
    ov = (cs[None, :] < ss[:, None] + SLC_BLOCK) & (cs[None, :] + CMP_BLOCK > ss[:, None])
    out = np.zeros((n_slc, n_cmp + 1), np.float32)
    out[:n_slc, :n_cmp] = ov
    return out


def _mod_kernel(c_ref, w_ref, b_ref, o_ref):
    c = c_ref[...]
    ca = (c * jax.nn.sigmoid(c)).astype(BF16)
    y = jnp.dot(ca, w_ref[0].astype(BF16), preferred_element_type=F32)
    o_ref[0] = y + b_ref[0]


def _modulation(c, w_mod, b_mod):
    B, D = c.shape
    L, _, W = w_mod.shape
    tn = 1024
    rows = 8
    cp = jnp.zeros((rows, D), F32).at[:B].set(c)
    out = pl.pallas_call(
        _mod_kernel,
        grid=(L, W // tn),
        in_specs=[pl.BlockSpec((rows, D), lambda l, j: (0, 0)),
                  pl.BlockSpec((1, D, tn), lambda l, j: (l, 0, j)),
                  pl.BlockSpec((1, 1, tn), lambda l, j: (l, 0, j))],
        out_specs=pl.BlockSpec((1, rows, tn), lambda l, j: (l, 0, j)),
        out_shape=jax.ShapeDtypeStruct((L, rows, W), F32),
        compiler_params=_cparams(2),
        name="adaln_mod",
    )(cp, w_mod, b_mod.reshape(L, 1, W))
    return out[:, :B]


def _norm_mod(x, gain, sc, sh):
    ms = jnp.mean(x * x, axis=-1, keepdims=True)
    y = x * lax.rsqrt(ms + EPS) * gain
    return y * (1.0 + sc) + sh


def _rope128(y, cos, sin_signed):
    lane = lax.broadcasted_iota(jnp.int32, y.shape, 1)
    first_half = (lane % HEAD_DIM) < (HEAD_DIM // 2)
    rot = jnp.where(first_half, pltpu.roll(y, LANES - HEAD_DIM // 2, 1), pltpu.roll(y, HEAD_DIM // 2, 1))
    return y * cos + rot * sin_signed


def _inproj_kernel(x_ref, sc_ref, sh_ref, g_ref, w_ref, cs_ref, cos_ref, sin_ref, o_ref, h_scr):
    @pl.when(pl.program_id(1) == 0)
    def _():
        h_scr[...] = _norm_mod(x_ref[...], g_ref[...], sc_ref[0], sh_ref[0]).astype(BF16)

    y = jnp.dot(h_scr[...], w_ref[...], preferred_element_type=F32)
    n_rope_tiles = N_ROPE_BLKS * LANES // TN_IN

    @pl.when(pl.program_id(1) < n_rope_tiles)
    def _():
        cos = cos_ref[...]
        sin = sin_ref[...]
        for c in range(TN_IN // LANES):
            sl = slice(c * LANES, (c + 1) * LANES)
            o_ref[:, sl] = _rope128(y[:, sl], cos, sin) * cs_ref[:, sl]

    @pl.when(pl.program_id(1) >= n_rope_tiles)
    def _():
        o_ref[...] = y * cs_ref[...]


def _in_projection(x2d, sc, sh, gain, w_re, cos128, sin128, T):
    N, D = x2d.shape
    nt = T // TM_PROJ
    return pl.pallas_call(
        _inproj_kernel,
        grid=(N // TM_PROJ, N_COLS // TN_IN),
        in_specs=[pl.BlockSpec((TM_PROJ, D), lambda i, j: (i, 0)),
                  pl.BlockSpec((1, 1, D), lambda i, j: (i // nt, 0, 0)),
                  pl.BlockSpec((1, 1, D), lambda i, j: (i // nt, 0, 0)),
                  pl.BlockSpec((1, D), lambda i, j: (0, 0)),
                  pl.BlockSpec((D, TN_IN), lambda i, j: (0, j)),
                  pl.BlockSpec((1, TN_IN), lambda i, j: (0, j)),
                  pl.BlockSpec((TM_PROJ, LANES), lambda i, j: (i % nt, 0)),
                  pl.BlockSpec((TM_PROJ, LANES), lambda i, j: (i % nt, 0))],
        out_specs=pl.BlockSpec((TM_PROJ, TN_IN), lambda i, j: (i, j)),
        out_shape=jax.ShapeDtypeStruct((N, N_COLS), F32),
        scratch_shapes=[pltpu.VMEM((TM_PROJ, D), BF16)],
        compiler_params=_cparams(2),
        name="in_proj",
    )(x2d, sc, sh, gain, w_re, jnp.asarray(_COL_SCALE), cos128, sin128)


def _split_heads_q(q):
    lane = lax.broadcasted_iota(jnp.int32, q.shape, 1)
    lo = lane < HEAD_DIM
    return jnp.concatenate([jnp.where(lo, q, 0.0), jnp.where(lo, 0.0, q)], axis=0)


def _scores(q2, k):
    return lax.dot_general(q2, k, (((1,), (1,)), ((), ())), preferred_element_type=F32)


def _flash_step(s, m, acc, v_ones):
    m_new = jnp.maximum(m, jnp.max(s, axis=-1, keepdims=True))
    p = jnp.exp(s - m_new).astype(BF16)
    acc = jnp.exp(m - m_new) * acc + jnp.dot(p, v_ones, preferred_element_type=F32)
    return m_new, acc


def _flash_init(rows):
    return jnp.full((rows, 1), NEG_BIG, F32), jnp.zeros((rows, LANES), F32)


def _flash_finish(acc):
    return acc / jnp.maximum(pltpu.roll(acc, HEAD_DIM, 1), TINY)


def _high_half(x):
    bits = lax.bitcast_convert_type(x, jnp.int32)
    return lax.bitcast_convert_type(bits & jnp.int32(-65536), F32)


def _split_bf16(x):
    hi = x.astype(BF16)
    return hi, (x - hi.astype(F32)).astype(BF16)


def _tile(ref, kt, n):
    return ref[0, pl.ds(pl.multiple_of(kt * n, n), n), :]


def _two_head_norm(out):
    lane = lax.broadcasted_iota(jnp.int32, out.shape, 1)
    lo = lane < HEAD_DIM
    sq = out * out
    ms0 = jnp.sum(jnp.where(lo, sq, 0.0), axis=-1, keepdims=True) * (1.0 / HEAD_DIM)
    ms1 = jnp.sum(jnp.where(lo, 0.0, sq), axis=-1, keepdims=True) * (1.0 / HEAD_DIM)
    return out * jnp.where(lo, lax.rsqrt(ms0 + EPS), lax.rsqrt(ms1 + EPS))


def _attn_a_kernel(q_ref, k_ref, v_ref, tbl_ref, o_ref):
    qi = pl.program_id(2)
    q = q_ref[0]
    lane = lax.broadcasted_iota(jnp.int32, q.shape, 1)
    lo = lane < HEAD_DIM
    q0 = jnp.where(lo, q, 0.0).astype(BF16)
    q1 = jnp.where(lo, 0.0, q).astype(BF16)
    n_d = tbl_ref.shape[0]

    def body(i, carry):
        m0, acc0, m1, acc1 = carry
        k = _tile(k_ref, qi - i, TK_A).astype(BF16)
        v = _tile(v_ref, qi - i, TK_A)
        lo_k = lax.broadcasted_iota(jnp.int32, v.shape, 1) < HEAD_DIM
        bias = tbl_ref[i]
        m0, acc0 = _flash_step(_scores(q0, k) + bias, m0, acc0, jnp.where(lo_k, v, 1.0).astype(BF16))
        m1, acc1 = _flash_step(_scores(q1, k) + bias, m1, acc1, jnp.where(lo_k, 1.0, v).astype(BF16))
        return m0, acc0, m1, acc1

    init = _flash_init(TQ_A) + _flash_init(TQ_A)
    _, acc0, _, acc1 = lax.fori_loop(0, jnp.minimum(qi, n_d - 1) + 1, body, init)
    o_ref[0] = _two_head_norm(jnp.where(lo, _flash_finish(acc0), _flash_finish(acc1)))


def _attention_a(proj, tbl):
    B, T, _ = proj.shape
    n_pairs = A_HEADS // 2
    return pl.pallas_call(
        _attn_a_kernel,
        grid=(B, n_pairs, T // TQ_A),
        in_specs=[pl.BlockSpec((1, TQ_A, LANES), lambda b, p, i: (b, i, BLK_AQ + p)),
                  pl.BlockSpec((1, T, LANES), lambda b, p, i: (b, 0, BLK_AK + p)),
                  pl.BlockSpec((1, T, LANES), lambda b, p, i: (b, 0, BLK_AV + p)),
                  pl.BlockSpec(tbl.shape, lambda b, p, i: (0, 0, 0))],
        out_specs=pl.BlockSpec((1, TQ_A, LANES), lambda b, p, i: (b, i, p)),
        out_shape=jax.ShapeDtypeStruct((B, T, n_pairs * LANES), F32),
        compiler_params=_cparams(3),
        name="attn_dilated",
    )(proj, proj, proj, tbl)


def _attn_c_kernel(q_ref, k_ref, v_ref, o_ref):
    qi = pl.program_id(2)
    q = q_ref[0]
    lo = lax.broadcasted_iota(jnp.int32, q.shape, 1) < HEAD_DIM
    qs = (jnp.where(lo, q, 0.0).astype(BF16), jnp.where(lo, 0.0, q).astype(BF16))
    diff = (lax.broadcasted_iota(jnp.int32, (TQ_C, TK_C), 1) - lax.broadcasted_iota(jnp.int32, (TQ_C, TK_C), 0))
    uj = lax.broadcasted_iota(jnp.int32, (TK_C, TK_C), 0)
    us = lax.broadcasted_iota(jnp.int32, (TK_C, TK_C), 1)
    upper = (uj >= us).astype(BF16)
    tiles_per_q = TQ_C // TK_C

    def tile(kt, carry, masked):
        k = _tile(k_ref, kt, TK_C).astype(BF16)
        v = _tile(v_ref, kt, TK_C).astype(BF16)
        if masked:
            before = diff < qi * TQ_C - kt * TK_C
        out = []
        for qh, (run, acc) in zip(qs, carry):
            z = _scores(qh, k)
            sp = jnp.maximum(z, 0.0) + jnp.log(1.0 + jnp.exp(-jnp.abs(z)))
            if masked:
                sp = jnp.where(before, sp, 0.0)
            hi = _high_half(sp)
            suffix = (jnp.dot(hi.astype(BF16), upper, preferred_element_type=F32)
                      + jnp.dot((sp - hi).astype(BF16), upper, preferred_element_type=F32))
            a = jnp.exp(z - suffix - run)
            if masked:
                a = jnp.where(before, a, 0.0)
            out.append((run + suffix[:, 0:1], acc + jnp.dot(a.astype(BF16), v, preferred_element_type=F32)))
        return tuple(out)

    carry = ((jnp.zeros((TQ_C, 1), F32), jnp.zeros((TQ_C, LANES), F32)),) * 2
    last = (qi + 1) * tiles_per_q - 1
    for j in range(tiles_per_q):
        carry = tile(last - j, carry, True)
    def rest_is_zero(c):
        return (jnp.minimum(jnp.min(c[0][0]), jnp.min(c[1][0])) >= RUN_DEAD).astype(jnp.int32)

    def more(state):
        i, dead, _ = state
        return (i <= last) & (dead == 0)

    def step(state):
        i, _, c = state
        c = tile(last - i, c, False)
        return i + 1, rest_is_zero(c), c

    _, _, carry = lax.while_loop(more, step, (jnp.int32(tiles_per_q), rest_is_zero(carry), carry))
    o_ref[0] = _two_head_norm(jnp.where(lo, carry[0][1], carry[1][1]))


def _attention_c(proj):
    B, T, _ = proj.shape
    n_pairs = C_HEADS // 2
    return pl.pallas_call(
        _attn_c_kernel,
        grid=(B, n_pairs, T // TQ_C),
        in_specs=[pl.BlockSpec((1, TQ_C, LANES), lambda b, p, i: (b, i, BLK_CQ + p)),
                  pl.BlockSpec((1, T, LANES), lambda b, p, i: (b, 0, BLK_CK + p)),
                  pl.BlockSpec((1, T, LANES), lambda b, p, i: (b, 0, BLK_CV + p))],
        out_specs=pl.BlockSpec((1, TQ_C, LANES), lambda b, p, i: (b, i, p)),
        out_shape=jax.ShapeDtypeStruct((B, T, n_pairs * LANES), F32),
        compiler_params=_cparams(3),
        name="attn_stickbreak",
    )(proj, proj, proj)


def _compress_kernel(x_ref, pe_ref, w1_ref, w2_ref, w2r_ref, cos_ref, sin_ref, o_ref):
    x = x_ref[0, 0, 0]
    half = x.shape[1]
    ha = jnp.dot(x + pe_ref[0, 0:1, :], w1_ref[0, :half, :], precision=HIGHEST, preferred_element_type=F32)
    hb = jnp.dot(x + pe_ref[0, 1:2, :], w1_ref[0, half:, :], precision=HIGHEST, preferred_element_type=F32)
    n = x.shape[0]
    h = ha + pltpu.roll(hb, n - 1, 0)
    g = jax.nn.gelu(h)
    y = jnp.dot(g, w2_ref[0], precision=HIGHEST, preferred_element_type=F32)
    yr = jnp.dot(g, w2r_ref[0], precision=HIGHEST, preferred_element_type=F32)
    out = y * cos_ref[0] + yr * sin_ref[0]
    rowi = lax.broadcasted_iota(jnp.int32, out.shape, 0)
    o_ref[0, 0, 0] = jnp.where(rowi < n - 1, out, 0.0)


def _compress(chunks, pe, w1, w2d, w2r, cosc, sinc):
    B, _, G, n, cw = chunks.shape
    return pl.pallas_call(
        _compress_kernel,
        grid=(B, 2, G),
        in_specs=[pl.BlockSpec((1, 1, 1, n, cw), lambda b, w, g: (b, w, g, 0, 0)),
                  pl.BlockSpec((1, 2, cw), lambda b, w, g: (w, 0, 0)),
                  pl.BlockSpec((1, 2 * cw, CMP_HIDDEN), lambda b, w, g: (w, 0, 0)),
                  pl.BlockSpec((1, CMP_HIDDEN, LANES), lambda b, w, g: (w, 0, 0)),
                  pl.BlockSpec((1, CMP_HIDDEN, LANES), lambda b, w, g: (w, 0, 0)),
                  pl.BlockSpec((1, n, LANES), lambda b, w, g: (w, 0, 0)),
                  pl.BlockSpec((1, n, LANES), lambda b, w, g: (w, 0, 0))],
        out_specs=pl.BlockSpec((1, 1, 1, n, LANES), lambda b, w, g: (b, w, g, 0, 0)),
        out_shape=jax.ShapeDtypeStruct((B, 2, G, n, LANES), F32),
        compiler_params=_cparams(3),
        name="nsa_compress",
    )(chunks, pe, w1, w2d, w2r, cosc, sinc)


def _nsa_kernel(q_ref, cmp_k_ref, cmp_v_ref, ks_ref, vs_ref, kw_ref, vw_ref, gate_ref, ovt_ref, o_ref):
    qi = pl.program_id(2)
    R = B_HEADS // B_KV_HEADS
    TQ, TK = TQ_B, TK_B
    rows = R * TQ
    q0 = qi * TQ
    q = q_ref[0]
    qf = jnp.concatenate([_split_heads_q(q[:, :LANES]), _split_heads_q(q[:, LANES:])], axis=0)
    q4 = qf.astype(BF16)

    kc = cmp_k_ref[0, 0, 0]
    vc = cmp_v_ref[0, 0, 0].astype(BF16)
    n_c = kc.shape[0]
    last_end = (n_c - 1) * CMP_STRIDE + CMP_BLOCK - 1
    t_col = q0 + lax.broadcasted_iota(jnp.int32, (TQ, 1), 0)
    cmp_end = lax.broadcasted_iota(jnp.int32, (TQ, n_c), 1) * CMP_STRIDE + (CMP_BLOCK - 1)
    valid = (cmp_end <= t_col) & (cmp_end < last_end)
    t_lane = q0 + lax.broadcasted_iota(jnp.int32, (1, TQ), 1)
    nt_dims = (((1,), (1,)), ((), ()))
    kc_hi, kc_lo = _split_bf16(kc)
    q_hi, q_lo = _split_bf16(qf)
    s = _scores(q_hi, kc_hi) + (_scores(q_hi, kc_lo) + _scores(q_lo, kc_hi))
    s = jnp.where(valid[None], s.reshape(R, TQ, n_c), NEG_BIG)
    p = jnp.exp(s - jnp.max(s, axis=-1, keepdims=True)) * valid.astype(F32)[None]
    p = p / jnp.maximum(jnp.sum(p, axis=-1, keepdims=True), TINY)
    o_cmp_all = jnp.dot(p.reshape(rows, n_c).astype(BF16), vc, preferred_element_type=F32)
    o_cmp = [o_cmp_all[r * TQ:(r + 1) * TQ] for r in range(R)]
    p_sum = jnp.sum(p, axis=0)
    imp = lax.dot_general(ovt_ref[...], p_sum, nt_dims, precision=HIGHEST, preferred_element_type=F32)
    n_slc = imp.shape[0]
    blk = lax.broadcasted_iota(jnp.int32, (n_slc, TQ), 0)
    blk_f = blk.astype(F32)
    tb = t_lane // SLC_BLOCK
    forced = (blk == 0) | (blk == tb) | (blk == tb - 1)
    imp = jnp.where(forced, imp + FORCE, imp)
    imp = jnp.where(blk > tb, -FORCE, imp)
    sel = jnp.zeros((n_slc, TQ), F32)
    for _ in range(min(SLC_TOPK, n_slc)):
        mx = jnp.max(imp, axis=0, keepdims=True)
        first = jnp.min(jnp.where(imp == mx, blk_f, float(n_slc)), axis=0, keepdims=True)
        pick = blk_f == first
        sel = jnp.where(pick, 1.0, sel)
        imp = jnp.where(pick, LOWEST, imp)
    sel_bias = jnp.concatenate([(sel - 1.0) * (-NEG_BIG), jnp.zeros((LANES - n_slc, TQ), F32)], axis=0)
    sel_bias = sel_bias.T.astype(BF16)

    diff = lax.broadcasted_iota(jnp.int32, (TQ, TK), 1) - lax.broadcasted_iota(jnp.int32, (TQ, TK), 0)
    e_row = lax.broadcasted_iota(jnp.int32, (LANES, TK), 0)
    e_col = lax.broadcasted_iota(jnp.int32, (LANES, TK), 1) // SLC_BLOCK
    lo_k = lax.broadcasted_iota(jnp.int32, (TK, LANES), 1) < HEAD_DIM
    last = q0 // TK

    def step(k_ref, v_ref, kt, bias, carry):
        k = _tile(k_ref, kt, TK).astype(BF16)
        v_ones = jnp.where(lo_k, _tile(v_ref, kt, TK), 1.0).astype(BF16)
        s = (_scores(q4, k).reshape(R, TQ, TK) + bias[None]).reshape(rows, TK)
        return _flash_step(s, *carry, v_ones)

    def slc_tile(kt, carry, causal):
        expand = (e_row == kt * (TK // SLC_BLOCK) + e_col).astype(BF16)
        bias = jnp.dot(sel_bias, expand, preferred_element_type=F32)
        if causal:
            bias = bias + jnp.where(diff <= q0 - kt * TK, 0.0, NEG_BIG)
        return step(ks_ref, vs_ref, kt, bias, carry)

    carry = lax.fori_loop(0, last, lambda kt, c: slc_tile(kt, c, False), _flash_init(rows))
    o_slc = _flash_finish(slc_tile(last, carry, True)[1])

    def win_tile(i, carry):
        kt = last - i
        off = q0 - kt * TK
        bias = jnp.where((diff <= off) & (diff > off - WIN), 0.0, NEG_BIG)
        return step(kw_ref, vw_ref, kt, bias, carry)

    first_win = jnp.maximum(q0 - (WIN - 1), 0) // TK
    o_win = _flash_finish(lax.fori_loop(0, last - first_win + 1, win_tile, _flash_init(rows))[1])

    gates = jax.nn.sigmoid(gate_ref[0])
    lo = lax.broadcasted_iota(jnp.int32, (TQ, LANES), 1) < HEAD_DIM
    outs = []
    for r in range(R):
        rs = slice(r * TQ, (r + 1) * TQ)
        o = (gates[:, 3 * r:3 * r + 1] * o_cmp[r] + gates[:, 3 * r + 1:3 * r + 2] * o_slc[rs]
             + gates[:, 3 * r + 2:3 * r + 3] * o_win[rs])
        ms = jnp.sum(jnp.where(lo, o * o, 0.0), axis=-1, keepdims=True) * (1.0 / HEAD_DIM)
        outs.append(o * lax.rsqrt(ms + EPS))
    pairs = [jnp.where(lo, outs[2 * j], pltpu.roll(outs[2 * j + 1], HEAD_DIM, 1)) for j in range(R // 2)]
    o_ref[0] = jnp.concatenate(pairs, axis=1)


def _attention_b(proj, cmp_kv, ov):
    B, T, _ = proj.shape
    G = B_KV_HEADS
    n_c = cmp_kv.shape[3]
    qw = 2 * LANES
    TQ = TQ_B
    return pl.pallas_call(
        _nsa_kernel,
        grid=(B, G, T // TQ),
        in_specs=[pl.BlockSpec((1, TQ, qw), lambda b, g, i: (b, i, BLK_BQ // 2 + g)),
                  pl.BlockSpec((1, 1, 1, n_c, LANES), lambda b, g, i: (b, 0, g, 0, 0)),
                  pl.BlockSpec((1, 1, 1, n_c, LANES), lambda b, g, i: (b, 1, g, 0, 0)),
                  pl.BlockSpec((1, T, LANES), lambda b, g, i: (b, 0, BLK_KS + g)),
                  pl.BlockSpec((1, T, LANES), lambda b, g, i: (b, 0, BLK_VS + g)),
                  pl.BlockSpec((1, T, LANES), lambda b, g, i: (b, 0, BLK_KW + g)),
                  pl.BlockSpec((1, T, LANES), lambda b, g, i: (b, 0, BLK_VW + g)),
                  pl.BlockSpec((1, TQ, LANES), lambda b, g, i: (b, i, BLK_GATE + g)),
                  pl.BlockSpec(ov.shape, lambda b, g, i: (0, 0))],
        out_specs=pl.BlockSpec((1, TQ, qw), lambda b, g, i: (b, i, g)),
        out_shape=jax.ShapeDtypeStruct((B, T, B_HEADS * HEAD_DIM), F32),
        compiler_params=_cparams(3),
        name="attn_nsa",
    )(proj, cmp_kv, cmp_kv, proj, proj, proj, proj, proj, ov)


def _outproj_kernel(oa_ref, ob_ref, oc_ref, mn_ref, w_ref, x_ref, g_ref, o_ref, h_scr):
    wa = oa_ref.shape[1]
    wb = ob_ref.shape[1]

    @pl.when(pl.program_id(1) == 0)
    def _():
        h_scr[:, :wa] = (oa_ref[...] * mn_ref[:, :wa]).astype(BF16)
        h_scr[:, wa:wa + wb] = (ob_ref[...] * mn_ref[:, wa:wa + wb]).astype(BF16)
        h_scr[:, wa + wb:] = (oc_ref[...] * mn_ref[:, wa + wb:]).astype(BF16)

    y = jnp.dot(h_scr[...], w_ref[...], preferred_element_type=F32)
    o_ref[...] = x_ref[...] + g_ref[0] * y


def _out_projection(oa, ob, oc, mix_norm, w_out_bf, x2d, g1, T):
    N, D = x2d.shape
    nt = T // TM_PROJ
    wa, wb, wc = oa.shape[1], ob.shape[1], oc.shape[1]
    return pl.pallas_call(
        _outproj_kernel,
        grid=(N // TM_PROJ, D // TN_OUT),
        in_specs=[pl.BlockSpec((TM_PROJ, wa), lambda i, j: (i, 0)),
                  pl.BlockSpec((TM_PROJ, wb), lambda i, j: (i, 0)),
                  pl.BlockSpec((TM_PROJ, wc), lambda i, j: (i, 0)),
                  pl.BlockSpec((1, D), lambda i, j: (0, 0)),
                  pl.BlockSpec((D, TN_OUT), lambda i, j: (0, j)),
                  pl.BlockSpec((TM_PROJ, TN_OUT), lambda i, j: (i, j)),
                  pl.BlockSpec((1, 1, TN_OUT), lambda i, j: (i // nt, 0, j))],
        out_specs=pl.BlockSpec((TM_PROJ, TN_OUT), lambda i, j: (i, j)),
        out_shape=jax.ShapeDtypeStruct((N, D), F32),
        scratch_shapes=[pltpu.VMEM((TM_PROJ, D), BF16)],
        compiler_params=_cparams(2),
        name="out_proj",
    )(oa, ob, oc, mix_norm, w_out_bf, x2d, g1)


def _router_kernel(x_ref, sc_ref, sh_ref, g_ref, wr_ref, br_ref, h_ref, idx_ref, gate_ref):
    h = _norm_mod(x_ref[...], g_ref[...], sc_ref[0], sh_ref[0])
    h_ref[...] = h
    logits = jnp.dot(h, wr_ref[...], precision=HIGHEST, preferred_element_type=F32) + br_ref[...]
    lane = lax.broadcasted_iota(jnp.int32, logits.shape, 1)
    lane_f = lane.astype(F32)
    cur = jnp.where(lane < N_EXPERTS, logits, LOWEST)
    idx_out = jnp.zeros(logits.shape, F32)
    e_out = jnp.zeros(logits.shape, F32)
    top0 = None
    denom = None
    for k in range(TOP_K):
        mx = jnp.max(cur, axis=-1, keepdims=True)
        first = jnp.min(jnp.where(cur == mx, lane_f, float(LANES)), axis=-1, keepdims=True)
        cur = jnp.where(lane_f == first, LOWEST, cur)
        if k == 0:
            top0 = mx
        e = jnp.exp(mx - top0)
        denom = e if k == 0 else denom + e
        idx_out = jnp.where(lane == k, first, idx_out)
        e_out = jnp.where(lane == k, e, e_out)
    idx_ref[...] = idx_out.astype(jnp.int32)
    gate_ref[...] = e_out / denom


def _router(x2d, sc, sh, gain, wr_pad, br_pad, T):
    N, D = x2d.shape
    nt = T // TM_ROUTE
    return pl.pallas_call(
        _router_kernel,
        grid=(N // TM_ROUTE,),
        in_specs=[pl.BlockSpec((TM_ROUTE, D), lambda i: (i, 0)),
                  pl.BlockSpec((1, 1, D), lambda i: (i // nt, 0, 0)),
                  pl.BlockSpec((1, 1, D), lambda i: (i // nt, 0, 0)),
                  pl.BlockSpec((1, D), lambda i: (0, 0)),
                  pl.BlockSpec((D, LANES), lambda i: (0, 0)),
                  pl.BlockSpec((1, LANES), lambda i: (0, 0))],
        out_specs=[pl.BlockSpec((TM_ROUTE, D), lambda i: (i, 0)),
                   pl.BlockSpec((TM_ROUTE, LANES), lambda i: (i, 0)),
                   pl.BlockSpec((TM_ROUTE, LANES), lambda i: (i, 0))],
        out_shape=[jax.ShapeDtypeStruct((N, D), F32),
                   jax.ShapeDtypeStruct((N, LANES), jnp.int32),
                   jax.ShapeDtypeStruct((N, LANES), F32)],
        compiler_params=_cparams(1),
        name="moe_router",
    )(x2d, sc, sh, gain, wr_pad, br_pad)


DMA_UNROLL = 8


def _row_gather(src_hbm, idx_ref, buf, sem, slot, n_rows, dst_row, wait):
    if wait:
        pltpu.make_async_copy(src_hbm.at[pl.ds(0, n_rows), :], buf.at[slot], sem.at[slot]).wait()
        return

    def body(a8, c):
        for u in range(DMA_UNROLL):
            a = a8 * DMA_UNROLL + u
            pltpu.make_async_copy(src_hbm.at[pl.ds(idx_ref[0, 0, a], 1), :],
                                  buf.at[slot, pl.ds(dst_row(a), 1), :], sem.at[slot]).start(priority=u % 2)
        return c

    lax.fori_loop(0, n_rows // DMA_UNROLL, body, 0)


def _expert_changed(be_ref, i):
    prev = be_ref[jnp.maximum(i - 1, 0)]
    return (i == 0) | (be_ref[i] != prev)


def _stage_expert_weights(be_ref, meta_ref, nxt_ref, w_hbm, stage, w_scr, sem, layer, windows, n_pass):
    j = pl.program_id(0)
    i = pl.program_id(1)
    e = be_ref[i]
    width = stage.shape[-1]

    def copies(ee, jj):
        return [pltpu.make_async_copy(w_hbm.at[layer, ee, :, pl.ds(pl.multiple_of(c0, LANES), width)],
                                      stage.at[p], sem.at[p])
                for p, c0 in enumerate(windows(jj))]

    @pl.when(_expert_changed(be_ref, i))
    def _():
        @pl.when((j == 0) & (i == 0))
        def _():
            for cp in copies(e, j):
                cp.start()

        for cp in copies(e, j):
            cp.wait()
        for p in range(stage.shape[0]):
            w_scr[p] = stage[p].astype(BF16)
        nxt = nxt_ref[e]
        nj = jnp.where(nxt >= 0, j, j + 1)
        ne = jnp.where(nxt >= 0, nxt, meta_ref[1])

        @pl.when(nj < n_pass)
        def _():
            for cp in copies(ne, nj):
                cp.start()


def _gather_kernel(nu_ref, tok_ref, nxt_ref, h_hbm, o_ref, buf, sem):
    i = pl.program_id(0)
    n_used = nu_ref[0]
    n = buf.shape[1]
    slot = i % 2
    ident = lambda a: a

    @pl.when(i == 0)
    def _():
        _row_gather(h_hbm, tok_ref, buf, sem, 0, n, ident, False)

    @pl.when(i + 1 < n_used)
    def _():
        _row_gather(h_hbm, nxt_ref, buf, sem, 1 - slot, n, ident, False)

    @pl.when(i < n_used)
    def _():
        _row_gather(h_hbm, tok_ref, buf, sem, slot, n, ident, True)
        o_ref[...] = buf[slot].astype(BF16)

    @pl.when(i >= n_used)
    def _():
        o_ref[...] = jnp.zeros(o_ref.shape, o_ref.dtype)


def _gather_rows(n_used, buf_tok, h):
    N, D = h.shape
    n_blk = buf_tok.shape[0]
    return pl.pallas_call(
        _gather_kernel,
        grid_spec=pltpu.PrefetchScalarGridSpec(
            num_scalar_prefetch=1,
            grid=(n_blk,),
            in_specs=[pl.BlockSpec((1, 1, MOE_ROWS), lambda i, nu: (i, 0, 0), memory_space=pltpu.SMEM),
                      pl.BlockSpec((1, 1, MOE_ROWS), lambda i, nu: (jnp.minimum(i + 1, n_blk - 1), 0, 0),
                                   memory_space=pltpu.SMEM),
                      pl.BlockSpec(memory_space=pl.ANY)],
            out_specs=pl.BlockSpec((MOE_ROWS, D), lambda i, nu: (i, 0)),
            scratch_shapes=[pltpu.VMEM((2, MOE_ROWS, D), F32), pltpu.SemaphoreType.DMA((2,))]),
        out_shape=jax.ShapeDtypeStruct((n_blk * MOE_ROWS, D), BF16),
        compiler_params=_cparams(1),
        name="moe_gather",
    )(n_used, buf_tok, buf_tok, h)


def _gmm1_kernel(be_ref, meta_ref, nxt_ref, x_ref, w_hbm, bg_ref, bl_ref, o_ref, stage, w_scr, sem, *,
                 layer, n_pass, d_ff):
    i = pl.program_id(1)

    @pl.when(i < meta_ref[0])
    def _():
        _stage_expert_weights(be_ref, meta_ref, nxt_ref, w_hbm, stage, w_scr, sem, layer,
                              lambda jj: (jj * TF, d_ff + jj * TF), n_pass)
        x = x_ref[...]
        glu = jnp.dot(x, w_scr[0], preferred_element_type=F32) + bg_ref[0, 0]
        lin = jnp.dot(x, w_scr[1], preferred_element_type=F32) + bl_ref[0, 0]
        glu = jnp.minimum(glu, SWIGLU_LIMIT)
        lin = jnp.clip(lin, -SWIGLU_LIMIT, SWIGLU_LIMIT)
        o_ref[...] = (glu * jax.nn.sigmoid(SWIGLU_ALPHA * glu) * (lin + 1.0)).astype(BF16)

    @pl.when(i >= meta_ref[0])
    def _():
        o_ref[...] = jnp.zeros(o_ref.shape, o_ref.dtype)


def _gmm2_kernel(be_ref, meta_ref, nxt_ref, a_ref, w_hbm, b_ref, o_ref, stage, w_scr, sem, *, layer, n_pass):
    i = pl.program_id(1)

    @pl.when(i < meta_ref[0])
    def _():
        _stage_expert_weights(be_ref, meta_ref, nxt_ref, w_hbm, stage, w_scr, sem, layer,
                              lambda jj: (jj * TN_MOE,), n_pass)
        o_ref[...] = jnp.dot(a_ref[...], w_scr[0], preferred_element_type=F32) + b_ref[0, 0]

    @pl.when(i >= meta_ref[0])
    def _():
        o_ref[...] = jnp.zeros(o_ref.shape, o_ref.dtype)


def _experts(layer, blk_expert, meta, nxt, xs, w1, b1, w2, b2):
    P, D = xs.shape
    L, E, _, F2 = w1.shape
    F = F2 // 2
    n_blk = P // MOE_ROWS
    nf = F // TF
    nd = D // TN_MOE
    b1r = b1.reshape(L, E, 1, F2)
    act = pl.pallas_call(
        functools.partial(_gmm1_kernel, layer=layer, n_pass=nf, d_ff=F),
        grid_spec=pltpu.PrefetchScalarGridSpec(
            num_scalar_prefetch=3,
            grid=(nf, n_blk),
            in_specs=[pl.BlockSpec((MOE_ROWS, D), lambda j, i, be, mt, nx: (i, 0)),
                      pl.BlockSpec(memory_space=pl.ANY),
                      pl.BlockSpec((1, 1, 1, TF), lambda j, i, be, mt, nx: (layer, be[i], 0, j)),
                      pl.BlockSpec((1, 1, 1, TF), lambda j, i, be, mt, nx: (layer, be[i], 0, nf + j))],
            out_specs=pl.BlockSpec((MOE_ROWS, TF), lambda j, i, be, mt, nx: (i, j)),
            scratch_shapes=[pltpu.VMEM((2, D, TF), F32), pltpu.VMEM((2, D, TF), BF16),
                            pltpu.SemaphoreType.DMA((2,))]),
        out_shape=jax.ShapeDtypeStruct((P, F), BF16),
        compiler_params=_cparams(2),
        name="moe_up",
    )(blk_expert, meta, nxt, xs, w1, b1r, b1r)
    return pl.pallas_call(
        functools.partial(_gmm2_kernel, layer=layer, n_pass=nd),
        grid_spec=pltpu.PrefetchScalarGridSpec(
            num_scalar_prefetch=3,
            grid=(nd, n_blk),
            in_specs=[pl.BlockSpec((MOE_ROWS, F), lambda j, i, be, mt, nx: (i, 0)),
                      pl.BlockSpec(memory_space=pl.ANY),
                      pl.BlockSpec((1, 1, 1, TN_MOE), lambda j, i, be, mt, nx: (layer, be[i], 0, j))],
            out_specs=pl.BlockSpec((MOE_ROWS, TN_MOE), lambda j, i, be, mt, nx: (i, j)),
            scratch_shapes=[pltpu.VMEM((1, F, TN_MOE), F32), pltpu.VMEM((1, F, TN_MOE), BF16),
                            pltpu.SemaphoreType.DMA((1,))]),
        out_shape=jax.ShapeDtypeStruct((P, D), F32),
        compiler_params=_cparams(2),
        name="moe_down",
    )(blk_expert, meta, nxt, act, w2, b2.reshape(L, E, 1, D))


def _combine_kernel(pos_ref, nxt_ref, y_hbm, x_ref, gate_ref, g2_ref, nf_ref, o_ref, buf, sem, *, final):
    i = pl.program_id(0)
    n_steps = pl.num_programs(0)
    n_tok = x_ref.shape[0]
    n_rows = TOP_K * n_tok
    slot = i % 2
    by_choice = lambda a: (a % TOP_K) * n_tok + a // TOP_K

    @pl.when(i == 0)
    def _():
        _row_gather(y_hbm, pos_ref, buf, sem, 0, n_rows, by_choice, False)

    @pl.when(i + 1 < n_steps)
    def _():
        _row_gather(y_hbm, nxt_ref, buf, sem, 1 - slot, n_rows, by_choice, False)

    _row_gather(y_hbm, pos_ref, buf, sem, slot, n_rows, by_choice, True)
    gate = gate_ref[...]
    moe = gate[:, 0:1] * buf[slot, 0:n_tok, :]
    for k in range(1, TOP_K):
        moe = moe + gate[:, k:k + 1] * buf[slot, k * n_tok:(k + 1) * n_tok, :]
    x = x_ref[...] + g2_ref[0] * moe
    if final:
        ms = jnp.mean(x * x, axis=-1, keepdims=True)
        x = x * lax.rsqrt(ms + EPS) * nf_ref[...]
    o_ref[...] = x


def _combine(pos, y, x2d, gate, g2, norm_final, T, final):
    N, D = x2d.shape
    nt = T // TC
    n_steps = N // TC
    return pl.pallas_call(
        functools.partial(_combine_kernel, final=final),
        grid=(n_steps,),
        in_specs=[pl.BlockSpec((1, 1, TOP_K * TC), lambda i: (i, 0, 0), memory_space=pltpu.SMEM),
                  pl.BlockSpec((1, 1, TOP_K * TC), lambda i: (jnp.minimum(i + 1, n_steps - 1), 0, 0),
                               memory_space=pltpu.SMEM),
                  pl.BlockSpec(memory_space=pl.ANY),
                  pl.BlockSpec((TC, D), lambda i: (i, 0)),
                  pl.BlockSpec((TC, LANES), lambda i: (i, 0)),
                  pl.BlockSpec((1, 1, D), lambda i: (i // nt, 0, 0)),
                  pl.BlockSpec((1, D), lambda i: (0, 0))],
        out_specs=pl.BlockSpec((TC, D), lambda i: (i, 0)),
        out_shape=jax.ShapeDtypeStruct((N, D), F32),
        scratch_shapes=[pltpu.VMEM((2, TOP_K * TC, D), F32), pltpu.SemaphoreType.DMA((2,))],
        compiler_params=_cparams(1),
        name="moe_combine_final" if final else "moe_combine",
    )(pos, pos, y, x2d, gate, g2, norm_final)


def _routing_tables(top_idx):
    N = top_idx.shape[0]
    NK = N * TOP_K
    e_flat = top_idx.reshape(NK)
    onehot = (e_flat[:, None] == jnp.arange(N_EXPERTS, dtype=jnp.int32)[None, :]).astype(jnp.int32)
    csum = jnp.cumsum(onehot, axis=0)
    counts = csum[-1]
    rank = jnp.sum(csum * onehot, axis=1) - 1
    padded = (counts + MOE_ROWS - 1) // MOE_ROWS * MOE_ROWS
    pad_end = jnp.cumsum(padded)
    pad_start = pad_end - padded
    dest = jnp.sum(onehot * pad_start[None, :], axis=1) + rank
    P = NK + N_EXPERTS * MOE_ROWS
    n_blk = P // MOE_ROWS
    tok = jnp.arange(NK, dtype=jnp.int32) // TOP_K
    buf_tok = jnp.zeros((P,), jnp.int32).at[dest].set(tok)
    blk_start = jnp.arange(n_blk, dtype=jnp.int32) * MOE_ROWS
    blk_expert = jnp.minimum(jnp.sum(blk_start[:, None] >= pad_end[None, :], axis=-1), N_EXPERTS - 1)
    n_used = pad_end[-1] // MOE_ROWS
    ids = jnp.arange(N_EXPERTS, dtype=jnp.int32)
    at_or_after = lax.cummin(jnp.where(counts > 0, ids, N_EXPERTS), reverse=True)
    nxt = jnp.concatenate([at_or_after[1:], jnp.full((1,), N_EXPERTS, jnp.int32)])
    nxt = jnp.where(nxt < N_EXPERTS, nxt, -1)
    meta = jnp.stack([n_used, at_or_after[0]])
    return (dest.astype(jnp.int32), buf_tok.reshape(n_blk, 1, MOE_ROWS), blk_expert.astype(jnp.int32),
            meta.astype(jnp.int32), nxt.astype(jnp.int32))


def _rope_tables(T):
    inv = 1.0 / (ROPE_THETA ** (jnp.arange(0, HEAD_DIM, 2, dtype=F32) / HEAD_DIM))
    ang = jnp.arange(T, dtype=F32)[:, None] * inv[None, :]
    return jnp.cos(ang), jnp.sin(ang)


def kernel(x, c, w_mod, b_mod, norm_attn, norm_ffn, w_in, cmp_pe, cmp_w1, cmp_w2, mix_norm, w_out,
           w_router, b_router, w_exp1, b_exp1, w_exp2, b_exp2, norm_final):
    B, T, D = x.shape
    N = B * T
    G = B_KV_HEADS
    assert D == D_MODEL and T % TM_PROJ == 0 and T % TM_ROUTE == 0
    assert all(T % n == 0 for n in (TQ_A, TK_A, TQ_B, TK_B, TQ_C, TK_C)) and TK_B % TQ_B == 0 and TQ_C % TK_C == 0

    cos, sin = _rope_tables(T)
    cos128 = jnp.tile(cos, (1, LANES // (HEAD_DIM // 2)))
    sin128 = jnp.tile(jnp.concatenate([-sin, sin], axis=1), (1, LANES // HEAD_DIM))
    n_cmp = (T - CMP_BLOCK) // CMP_STRIDE + 1
    n_chunk = T // CMP_STRIDE
    assert n_chunk == n_cmp + 1
    cmp_end = np.arange(n_chunk) * CMP_STRIDE + CMP_BLOCK - 1
    cmp_end = np.minimum(cmp_end, T - 1)
    cosc = jnp.stack([jnp.tile(cos[cmp_end], (1, 4)), jnp.ones((n_chunk, LANES), F32)])
    sinc = jnp.stack([jnp.tile(sin[cmp_end], (1, 4)), jnp.zeros((n_chunk, LANES), F32)])
    tbl = jnp.asarray(_dilated_bias_table())
    ov = jnp.asarray(_overlap_table_t(T))

    mod = _modulation(c, w_mod, b_mod).reshape(DEPTH, B, 6, 1, D)
    w_in_re = _permute_cols(w_in).astype(BF16)
    w_out_bf = w_out.astype(BF16)
    wr_pad = jnp.zeros((DEPTH, D, LANES), F32).at[:, :, :N_EXPERTS].set(w_router)
    br_pad = jnp.zeros((DEPTH, 1, LANES), F32).at[:, 0, :N_EXPERTS].set(b_router)
    half = HEAD_DIM // 2
    w2d = jnp.concatenate([cmp_w2, cmp_w2], axis=-1)
    w2rot = jnp.concatenate([-cmp_w2[..., half:], cmp_w2[..., :half]], axis=-1)
    w2r = jnp.concatenate([w2rot, w2rot], axis=-1)
    pe = cmp_pe.reshape(DEPTH, 2, 2, CMP_STRIDE * HEAD_DIM)

    xf = x.reshape(N, D)
    for i in range(DEPTH):
        sh1, sc1, g1, sh2, sc2, g2 = [mod[i, :, k] for k in range(6)]
        proj = _in_projection(xf, sc1, sh1, norm_attn[i][None, :], w_in_re[i], cos128, sin128, T)
        proj3 = proj.reshape(B, T, N_COLS)
        o_a = _attention_a(proj3, tbl)
        kcvc = proj3[:, :, BLK_KCVC * LANES:(BLK_KCVC + 3) * LANES]
        chunks = kcvc.reshape(B, T, 2, G, HEAD_DIM).transpose(0, 2, 3, 1, 4).reshape(
            B, 2, G, n_chunk, CMP_STRIDE * HEAD_DIM)
        cmp_kv = _compress(chunks, pe[i], cmp_w1[i], w2d[i], w2r[i], cosc, sinc)
        o_b = _attention_b(proj3, cmp_kv, ov)
        o_c = _attention_c(proj3)
        xf = _out_projection(o_a.reshape(N, -1), o_b.reshape(N, -1), o_c.reshape(N, -1),
                             mix_norm[i][None, :], w_out_bf[i], xf, g1, T)
        h2, idx128, gate128 = _router(xf, sc2, sh2, norm_ffn[i][None, :], wr_pad[i], br_pad[i], T)
        dest, buf_tok, blk_expert, meta, nxt = _routing_tables(idx128[:, :TOP_K])
        xs = _gather_rows(meta, buf_tok, h2)
        y = _experts(i, blk_expert, meta, nxt, xs, w_exp1, b_exp1, w_exp2, b_exp2)
        xf = _combine(dest.reshape(N // TC, 1, TOP_K * TC), y, xf, gate128, g2,
                      norm_final[None, :], T, final=(i == DEPTH - 1))
    return xf.reshape(B, T, D)
```
